```python
import itertools
import jax, jax.numpy as jnp
from jax import lax
import numpy as np

D_MODEL = 2048
BATCH = 2
SEQ = 16384
DEPTH = 2
DEC_BATCH = 8
DEC_SEQ = 4096
PAST_LEN = 128

A_HEADS = 8
Q_LORA = 512
KV_LORA = 512
QK_NOPE = 128
QK_ROPE = 64
V_HEAD = 128
ROPE_THETA = 10000.0
Q_BLOCK = 128
A_WIDTH = A_HEADS * V_HEAD
M_HEADS = 8
M_QK = 128
M_V = 128
CHUNK = 64
M_QK_WIDTH = M_HEADS * M_QK
M_WIDTH = M_HEADS * M_V
N_GATES = 4 * M_HEADS
EPS = 1e-6
SPLIT_SIZES = (Q_LORA, KV_LORA, QK_ROPE, A_WIDTH,
               M_QK_WIDTH, M_QK_WIDTH, M_WIDTH, M_WIDTH, M_WIDTH, N_GATES,
               D_MODEL, D_MODEL)
IN_COLS = Q_LORA + KV_LORA + QK_ROPE + A_WIDTH + 2 * M_QK_WIDTH + 3 * M_WIDTH + N_GATES + 2 * D_MODEL

kernel_name = "hybrid_mla_mlstm_bidir_encoder"


def rms_norm(x, g):
    xf = x.astype(jnp.float32)
    y = xf * lax.rsqrt(jnp.mean(xf * xf, axis=-1, keepdims=True) + EPS)
    return (y * g.astype(jnp.float32)).astype(x.dtype)


def split_cols(t):
    offs = list(itertools.accumulate(SPLIT_SIZES))[:-1]
    return jnp.split(t, offs, axis=-1)


def rope_tables(seq_len):
    inv = ROPE_THETA ** (-jnp.arange(0, QK_ROPE, 2, dtype=jnp.float32) / QK_ROPE)
    ang = jnp.arange(seq_len, dtype=jnp.float32)[:, None] * inv[None, :]
    return jnp.cos(ang), jnp.sin(ang)


def apply_rope(x, cos, sin):
    extra = x.ndim - 3
    c = cos.reshape(cos.shape[0], *([1] * extra), cos.shape[1])
    s = sin.reshape(sin.shape[0], *([1] * extra), sin.shape[1])
    xf = x.astype(jnp.float32)
    x1, x2 = xf[..., :QK_ROPE // 2], xf[..., QK_ROPE // 2:]
    out = jnp.concatenate([x1 * c - x2 * s, x2 * c + x1 * s], axis=-1)
    return out.astype(x.dtype)


def mla_branch(c_q, c_kv, k_rope, q_a_norm, w_uq, kv_a_norm, w_ukv, cos, sin):
    B, S, _ = c_q.shape
    q = (rms_norm(c_q, q_a_norm) @ w_uq).reshape(B, S, A_HEADS, QK_NOPE + QK_ROPE)
    q_nope = q[..., :QK_NOPE]
    q_rope = apply_rope(q[..., QK_NOPE:], cos, sin)
    kv = (rms_norm(c_kv, kv_a_norm) @ w_ukv).reshape(B, S, A_HEADS, QK_NOPE + V_HEAD)
    k_nope, v = kv[..., :QK_NOPE], kv[..., QK_NOPE:]
    k_r = apply_rope(k_rope, cos, sin)
    scale = (QK_NOPE + QK_ROPE) ** -0.5
    nb = S // Q_BLOCK

    def to_blocks(t):
        return t.reshape(B, nb, Q_BLOCK, *t.shape[2:]).swapaxes(0, 1)

    def attend(blk):
        qn, qr = blk
        s = (jnp.einsum('bqhd,bkhd->bhqk', qn, k_nope, preferred_element_type=jnp.float32)
             + jnp.einsum('bqhd,bkd->bhqk', qr, k_r, preferred_element_type=jnp.float32))
        p = jax.nn.softmax(s * scale, axis=-1)
        return jnp.einsum('bhqk,bkhd->bqhd', p.astype(v.dtype), v)

    o = lax.map(attend, (to_blocks(q_nope), to_blocks(q_rope)))
    return o.swapaxes(0, 1).reshape(B, S, A_WIDTH)


def mlstm_chunkwise(q, k, v, i_pre, f_pre):
    B, S, H, DK = q.shape
    DV = v.shape[-1]
    L = CHUNK
    NC = S // L
    f32 = jnp.float32
    q = q.astype(f32).reshape(B, NC, L, H, DK) * (DK ** -0.5)
    k = k.astype(f32).reshape(B, NC, L, H, DK)
    v = v.astype(f32).reshape(B, NC, L, H, DV)
    log_f = jax.nn.log_sigmoid(f_pre.astype(f32)).reshape(B, NC, L, H).transpose(0, 1, 3, 2)
    log_i = i_pre.astype(f32).reshape(B, NC, L, H).transpose(0, 1, 3, 2)
    b = jnp.cumsum(log_f, axis=-1)
    g = b[..., -1]
    a = g[..., None] - b + log_i
    m_loc = a.max(axis=-1)
    w = jnp.exp(a - m_loc[..., None])
    kw = k * w.transpose(0, 1, 3, 2)[..., None]
    C_loc = jnp.einsum('bclhk,bclhv->bchkv', kw, v)
    n_loc = kw.sum(axis=2)

    def step(carry, xs):
        C, n, m = carry
        Cl, nl, ml, gl = xs
        m_new = jnp.maximum(gl + m, ml)
        a_old = jnp.exp(gl + m - m_new)
        a_new = jnp.exp(ml - m_new)
        C_new = a_old[..., None, None] * C + a_new[..., None, None] * Cl
        n_new = a_old[..., None] * n + a_new[..., None] * nl
        return (C_new, n_new, m_new), (C, n, m)

    init = (jnp.zeros((B, H, DK, DV), f32), jnp.zeros((B, H, DK), f32), jnp.zeros((B, H), f32))
    xs = tuple(jnp.moveaxis(t, 1, 0) for t in (C_loc, n_loc, m_loc, g))
    _, (C_prev, n_prev, m_prev) = lax.scan(step, init, xs)
    C_prev = jnp.moveaxis(C_prev, 0, 1)
    n_prev = jnp.moveaxis(n_prev, 0, 1)
    m_prev = jnp.moveaxis(m_prev, 0, 1)

    lower = jnp.tril(jnp.ones((L, L), dtype=bool))
    Dm = jnp.where(lower, b[..., :, None] - b[..., None, :] + log_i[..., None, :], -jnp.inf)
    e = b + m_prev[..., None]
    m_t = jnp.maximum(e, Dm.max(axis=-1))
    P = jnp.exp(Dm - m_t[..., None]) * jnp.einsum('bcjhd,bclhd->bchjl', q, k)
    w_inter = jnp.exp(e - m_t)
    num = (jnp.einsum('bchjl,bclhv->bchjv', P, v)
           + w_inter[..., None] * jnp.einsum('bcjhd,bchdv->bchjv', q, C_prev))
    den = P.sum(axis=-1) + w_inter * jnp.einsum('bcjhd,bchd->bchj', q, n_prev)
    h = num / jnp.maximum(jnp.abs(den), jnp.exp(-m_t))[..., None]
    return h.transpose(0, 1, 3, 2, 4).reshape(B, S, H, DV)


def mlstm_bidir(q, k, v, gates):
    i_fw, i_bw, f_fw, f_bw = jnp.split(gates, 4, axis=-1)
    h_fw = mlstm_chunkwise(q, k, v, i_fw, f_fw)
    flip = lambda t: jnp.flip(t, axis=1)
    h_bw = flip(mlstm_chunkwise(flip(q), flip(k), flip(v), flip(i_bw), flip(f_bw)))
    return h_fw + h_bw


def encoder_layer(x, cos, sin, norm_in, w_in, b_gates, q_a_norm, w_uq, kv_a_norm, w_ukv,
                  w_oa, m_head_norm, w_ob, w_out):
    B, S, _ = x.shape
    h = rms_norm(x, norm_in)
    (c_q, c_kv, k_rope, z_a, q_m, k_m, v_m, o_m, z_m, gates, g_a, g_b) = split_cols(h @ w_in)
    y_a = mla_branch(c_q, c_kv, k_rope, q_a_norm, w_uq, kv_a_norm, w_ukv, cos, sin) * jax.nn.silu(z_a)
    hm = mlstm_bidir(q_m.reshape(B, S, M_HEADS, M_QK), k_m.reshape(B, S, M_HEADS, M_QK),
                     v_m.reshape(B, S, M_HEADS, M_V), gates.astype(jnp.float32) + b_gates.astype(jnp.float32))
    hm = hm * lax.rsqrt(jnp.mean(hm * hm, axis=-1, keepdims=True) + EPS)
    hm = (hm * m_head_norm.astype(jnp.float32).reshape(M_HEADS, M_V)).reshape(B, S, M_WIDTH).astype(x.dtype)
    y_b = hm * jax.nn.sigmoid(o_m) * jax.nn.silu(z_m)
    merged = jax.nn.sigmoid(g_a) * (y_a @ w_oa) + jax.nn.sigmoid(g_b) * (y_b @ w_ob)
    return x + merged @ w_out


def trunk(x, norm_in, w_in, b_gates, q_a_norm, w_uq, kv_a_norm, w_ukv, w_oa, m_head_norm,
          w_ob, w_out, norm_f):
    cos, sin = rope_tables(x.shape[1])
    for l in range(DEPTH):
        x = encoder_layer(x, cos, sin, norm_in[l], w_in[l], b_gates[l], q_a_norm[l], w_uq[l],
                          kv_a_norm[l], w_ukv[l], w_oa[l], m_head_norm[l], w_ob[l], w_out[l])
    return rms_norm(x, norm_f)


def setup_inputs(seed: int = 0) -> dict:
    key = jax.random.key(seed)
    ks = jax.random.split(key, 16)
    f32 = jnp.float32
    nrm = lambda k, shape: jax.random.normal(k, shape, f32)
    gain = lambda k, shape: 1.0 + 0.02 * nrm(k, shape)
    b_i = 0.1 * nrm(ks[2], (DEPTH, 2 * M_HEADS))
    b_f = 3.0 + 0.5 * nrm(ks[3], (DEPTH, 2 * M_HEADS))
    return {
        "x_prompt": nrm(ks[0], (BATCH, SEQ, D_MODEL)),
        "x_sample": nrm(ks[1], (DEC_BATCH, DEC_SEQ, D_MODEL)),
        "norm_in": gain(ks[4], (DEPTH, D_MODEL)),
        "w_in": nrm(ks[5], (DEPTH, D_MODEL, IN_COLS)) * D_MODEL ** -0.5,
        "b_gates": jnp.concatenate([b_i, b_f], axis=-1),
        "q_a_norm": gain(ks[6], (DEPTH, Q_LORA)),
        "w_uq": nrm(ks[7], (DEPTH, Q_LORA, A_HEADS * (QK_NOPE + QK_ROPE))) * Q_LORA ** -0.5,
        "kv_a_norm": gain(ks[8], (DEPTH, KV_LORA)),
        "w_ukv": nrm(ks[9], (DEPTH, KV_LORA, A_HEADS * (QK_NOPE + V_HEAD))) * KV_LORA ** -0.5,
        "w_oa": nrm(ks[10], (DEPTH, A_WIDTH, D_MODEL)) * A_WIDTH ** -0.5,
        "m_head_norm": gain(ks[11], (DEPTH, M_WIDTH)),
        "w_ob": nrm(ks[12], (DEPTH, M_WIDTH, D_MODEL)) * M_WIDTH ** -0.5,
        "w_out": nrm(ks[13], (DEPTH, D_MODEL, D_MODEL)) * D_MODEL ** -0.5,
        "norm_f": gain(ks[14], (D_MODEL,)),
    }


def reference(x_prompt, x_sample, norm_in, w_in, b_gates, q_a_norm, w_uq, kv_a_norm, w_ukv,
              w_oa, m_head_norm, w_ob, w_out, norm_f):
    y_prompt = trunk(x_prompt, norm_in, w_in, b_gates, q_a_norm, w_uq, kv_a_norm, w_ukv,
                     w_oa, m_head_norm, w_ob, w_out, norm_f)
    y_sample = trunk(x_sample, norm_in, w_in, b_gates, q_a_norm, w_uq, kv_a_norm, w_ukv,
                     w_oa, m_head_norm, w_ob, w_out, norm_f)
    return (y_prompt, y_sample)
```

```python
import functools

import jax
import jax.numpy as jnp
from jax import lax
from jax.experimental import pallas as pl
from jax.experimental.pallas import tpu as pltpu

D_MODEL = 2048
A_HEADS = 8
Q_LORA = 512
KV_LORA = 512
QK_NOPE = 128
QK_ROPE = 64
V_HEAD = 128
ROPE_THETA = 10000.0
A_WIDTH = A_HEADS * V_HEAD
M_HEADS = 8
M_QK = 128
M_V = 128
CHUNK = 64
M_WIDTH = M_HEADS * M_V
EPS = 1e-6

LANES = 128
QK_PAD = 256
VMEM_LIMIT = 56 * 1024 * 1024

SMALL_COLS = Q_LORA + KV_LORA + 2 * LANES
BIG_COLS = 6 * 1024 + 2 * D_MODEL

F32 = jnp.float32
BF16 = jnp.bfloat16


def _rms(xf, g):
    ms = jnp.mean(xf * xf, axis=-1, keepdims=True)
    return xf * lax.rsqrt(ms + EPS) * g


def _params(sem):
    return pltpu.CompilerParams(dimension_semantics=sem, vmem_limit_bytes=VMEM_LIMIT)


def _norm_matmul_kernel(x_ref, g_ref, w_ref, o_ref, h_scr, *, rows):
    @pl.when(pl.program_id(1) == 0)
    def _():
        def body(r, c):
            sl = pl.ds(pl.multiple_of(r * rows, rows), rows)
            h_scr[sl, :] = _rms(x_ref[sl, :], g_ref[...]).astype(h_scr.dtype)
            return c
        lax.fori_loop(0, x_ref.shape[0] // rows, body, 0)

    o_ref[...] = jnp.dot(h_scr[...], w_ref[...],
                         preferred_element_type=F32).astype(o_ref.dtype)


def _norm_matmul(x2, gain, w, out_dtype, tm, tn):
    t, k = x2.shape
    n = w.shape[1]
    tm = min(tm, t)
    rows = min(256, tm)
    return pl.pallas_call(
        functools.partial(_norm_matmul_kernel, rows=rows),
        grid=(t // tm, n // tn),
        in_specs=[pl.BlockSpec((tm, k), lambda i, j: (i, 0)),
                  pl.BlockSpec((1, k), lambda i, j: (0, 0)),
                  pl.BlockSpec((k, tn), lambda i, j: (0, j))],
        out_specs=pl.BlockSpec((tm, tn), lambda i, j: (i, j)),
        out_shape=jax.ShapeDtypeStruct((t, n), out_dtype),
        scratch_shapes=[pltpu.VMEM((tm, k), BF16)],
        compiler_params=_params(("parallel", "arbitrary")),
        name="in_proj",
    )(x2, gain, w)


def _mla_prep_kernel(cq_ref, ckv_ref, rest_ref, cs_ref, gq_ref, gkv_ref, wq_ref, wkv_ref,
                     q_ref, k_ref, v_ref, *, q_scale):
    hq = _rms(cq_ref[...], gq_ref[...]).astype(BF16)
    hkv = _rms(ckv_ref[...], gkv_ref[...]).astype(BF16)
    q = jnp.dot(hq, wq_ref[...], preferred_element_type=F32)
    kv = jnp.dot(hkv, wkv_ref[...], preferred_element_type=F32)
    cs = cs_ref[...]
    lane = lax.broadcasted_iota(jnp.int32, cs.shape, 1)
    keep = lane < QK_ROPE

    def rope(t):
        t = t * cs
        return jnp.where(keep, t + pltpu.roll(t, QK_ROPE, 1), 0.0)

    k_r = rope(rest_ref[...])
    for h in range(A_HEADS):
        qh = q[:, h * QK_PAD:(h + 1) * QK_PAD]
        q_ref[h] = jnp.concatenate(
            [qh[:, :QK_NOPE] * q_scale, rope(qh[:, QK_NOPE:]) * q_scale], axis=1).astype(BF16)
        kvh = kv[:, h * QK_PAD:(h + 1) * QK_PAD]
        k_ref[h] = jnp.concatenate([kvh[:, :QK_NOPE], k_r], axis=1).astype(BF16)
        v_ref[h] = kvh[:, QK_NOPE:].astype(BF16)


def _mla_prep(small, cs, gq, gkv, wq, wkv, b, s, q_scale):
    tm = min(512, s)
    nb = s // tm
    row = lambda bi, i: (bi * nb + i, 0)
    full = lambda bi, i: (0, 0)
    hspec = lambda w: pl.BlockSpec((None, A_HEADS, tm, w), lambda bi, i: (bi, 0, i, 0))
    return pl.pallas_call(
        functools.partial(_mla_prep_kernel, q_scale=q_scale),
        grid=(b, nb),
        in_specs=[pl.BlockSpec((tm, Q_LORA), row),
                  pl.BlockSpec((tm, KV_LORA), lambda bi, i: (bi * nb + i, 1)),
                  pl.BlockSpec((tm, LANES), lambda bi, i: (bi * nb + i, (Q_LORA + KV_LORA) // LANES)),
                  pl.BlockSpec((tm, LANES), lambda bi, i: (i, 0)),
                  pl.BlockSpec((1, Q_LORA), full),
                  pl.BlockSpec((1, KV_LORA), full),
                  pl.BlockSpec((Q_LORA, A_HEADS * QK_PAD), full),
                  pl.BlockSpec((KV_LORA, A_HEADS * QK_PAD), full)],
        out_specs=[hspec(QK_PAD), hspec(QK_PAD), hspec(V_HEAD)],
        out_shape=[jax.ShapeDtypeStruct((b, A_HEADS, s, QK_PAD), BF16),
                   jax.ShapeDtypeStruct((b, A_HEADS, s, QK_PAD), BF16),
                   jax.ShapeDtypeStruct((b, A_HEADS, s, V_HEAD), BF16)],
        compiler_params=_params(("parallel", "parallel")),
        name="mla_prep",
    )(small, small, small, cs, gq, gkv, wq, wkv)


def _flash_kernel(q_ref, k_ref, v_ref, z_ref, o_ref, m_scr, l_scr, acc_scr, *, tk):
    q = q_ref[...]
    m_scr[...] = jnp.full(m_scr.shape, -jnp.inf, F32)
    l_scr[...] = jnp.zeros(l_scr.shape, F32)
    acc_scr[...] = jnp.zeros(acc_scr.shape, F32)

    def body(kb, c):
        sl = pl.ds(pl.multiple_of(kb * tk, tk), tk)
        k = k_ref[sl, :]
        v = v_ref[sl, :]
        s = lax.dot_general(q, k, (((1,), (1,)), ((), ())), preferred_element_type=F32)
        m_prev = m_scr[...]
        m_new = jnp.maximum(m_prev, jnp.max(s, axis=1, keepdims=True))
        alpha = jnp.exp2(m_prev - m_new)
        p = jnp.exp2(s - m_new)
        l_scr[...] = alpha * l_scr[...] + jnp.sum(p, axis=1, keepdims=True)
        acc_scr[...] = alpha * acc_scr[...] + jnp.dot(p.astype(BF16), v,
                                                      preferred_element_type=F32)
        m_scr[...] = m_new
        return c

    lax.fori_loop(0, k_ref.shape[0] // tk, body, 0)
    z = z_ref[...].astype(F32)
    o = acc_scr[...] / l_scr[...]
    o_ref[...] = (o * (z * jax.nn.sigmoid(z))).astype(o_ref.dtype)


def _flash(qc, kc, v, big, b, s):
    tq = min(512, s)
    tk = min(512, s)
    nq = s // tq
    return pl.pallas_call(
        functools.partial(_flash_kernel, tk=tk),
        grid=(b, A_HEADS, nq),
        in_specs=[pl.BlockSpec((None, None, tq, QK_PAD), lambda bi, h, i: (bi, h, i, 0)),
                  pl.BlockSpec((None, None, s, QK_PAD), lambda bi, h, i: (bi, h, 0, 0)),
                  pl.BlockSpec((None, None, s, V_HEAD), lambda bi, h, i: (bi, h, 0, 0)),
                  pl.BlockSpec((tq, V_HEAD), lambda bi, h, i: (bi * nq + i, h))],
        out_specs=pl.BlockSpec((tq, V_HEAD), lambda bi, h, i: (bi * nq + i, h)),
        out_shape=jax.ShapeDtypeStruct((b * s, A_WIDTH), BF16),
        scratch_shapes=[pltpu.VMEM((tq, 1), F32), pltpu.VMEM((tq, 1), F32),
                        pltpu.VMEM((tq, V_HEAD), F32)],
        compiler_params=_params(("parallel", "parallel", "arbitrary")),
        name="mla_flash",
    )(qc, kc, v, big)


GATE_CH = 2 * M_HEADS
GATE_GROUPS = 8


def _gate_kernel(x_ref, bi_ref, bf_ref, col_ref, row_ref):
    shape = (GATE_CH, LANES)
    pos = lax.broadcasted_iota(jnp.int32, shape, 1) % CHUNK
    is_fw = lax.broadcasted_iota(jnp.int32, shape, 0) < M_HEADS

    def seg_scan(y, op, ident):
        for sft in (1, 2, 4, 8, 16, 32):
            y_f = jnp.where(pos >= sft, pltpu.roll(y, sft, 1), ident)
            y_b = jnp.where(pos < CHUNK - sft, pltpu.roll(y, LANES - sft, 1), ident)
            y = op(y, jnp.where(is_fw, y_f, y_b))
        return y

    for g in range(row_ref.shape[0]):
        rs = slice(g * LANES, (g + 1) * LANES)
        xt = x_ref[rs, :].T
        log_i = xt[0:GATE_CH] + bi_ref[...]
        log_f = jax.nn.log_sigmoid(xt[GATE_CH:2 * GATE_CH] + bf_ref[...])
        b = seg_scan(log_f, jnp.add, 0.0)
        r = log_i - b
        cm = seg_scan(r, jnp.maximum, -jnp.inf)
        row_ref[g] = r
        pack = jnp.concatenate([b, r, cm, jnp.zeros((LANES - 3 * GATE_CH, LANES), F32)], axis=0)
        col_ref[rs, :] = pack.T


def _gates(small, b_i, b_f):
    t = small.shape[0]
    ng = t // LANES
    gpb = min(GATE_GROUPS, ng)
    return pl.pallas_call(
        _gate_kernel,
        grid=(ng // gpb,),
        in_specs=[pl.BlockSpec((gpb * LANES, LANES), lambda i: (i, SMALL_COLS // LANES - 1)),
                  pl.BlockSpec((GATE_CH, 1), lambda i: (0, 0)),
                  pl.BlockSpec((GATE_CH, 1), lambda i: (0, 0))],
        out_specs=[pl.BlockSpec((gpb * LANES, LANES), lambda i: (i, 0)),
                   pl.BlockSpec((gpb, GATE_CH, LANES), lambda i: (i, 0, 0))],
        out_shape=[jax.ShapeDtypeStruct((t, LANES), F32),
                   jax.ShapeDtypeStruct((ng, GATE_CH, LANES), F32)],
        compiler_params=_params(("parallel",)),
        name="mlstm_gates",
    )(small, b_i, b_f)


def _mlstm_kernel(q_ref, k_ref, v_ref, col_ref, row_ref, h_ref, c_scr, m_scr, *, reverse):
    @pl.when(pl.program_id(1) == 0)
    def _():
        c_scr[...] = jnp.zeros(c_scr.shape, F32)
        m_scr[...] = jnp.zeros(m_scr.shape, F32)

    d = 1 if reverse else 0
    jj = lax.broadcasted_iota(jnp.int32, (CHUNK, CHUNK), 0)
    ll = lax.broadcasted_iota(jnp.int32, (CHUNK, CHUNK), 1)
    mask = (ll >= jj) if reverse else (ll <= jj)
    ones_blk = (lax.broadcasted_iota(jnp.int32, (CHUNK, LANES), 1) == 0).astype(BF16)
    last = 0 if reverse else CHUNK - 1
    inv_scale = float(M_QK) ** 0.5

    for half in ((1, 0) if reverse else (0, 1)):
        rs = slice(half * CHUNK, (half + 1) * CHUNK)
        col = col_ref[rs, :]
        for h in range(M_HEADS):
            ch = d * M_HEADS + h
            hs = slice(h * M_QK, (h + 1) * M_QK)
            b_c = col[:, ch:ch + 1]
            r_c = col[:, GATE_CH + ch:GATE_CH + ch + 1]
            cm_c = col[:, 2 * GATE_CH + ch:2 * GATE_CH + ch + 1]
            r_row = row_ref[ch:ch + 1, rs]
            g_tot = b_c[last:last + 1]
            r_max = cm_c[last:last + 1]
            m_prev = m_scr[h:h + 1, 0:1]
            mm = jnp.maximum(m_prev, cm_c)
            qh, kh, vh = q_ref[rs, hs], k_ref[rs, hs], v_ref[rs, hs]
            qk = lax.dot_general(qh, kh, (((1,), (1,)), ((), ())), preferred_element_type=F32)
            p = (jnp.exp(jnp.where(mask, r_row - mm, -jnp.inf)) * qk).astype(BF16)
            v_ext = jnp.concatenate([vh, ones_blk], axis=1)
            c_prev = c_scr[h]
            tot = (jnp.dot(p, v_ext, preferred_element_type=F32)
                   + jnp.exp(m_prev - mm) * jnp.dot(qh, c_prev.astype(BF16),
                                                    preferred_element_type=F32))
            num = tot[:, :M_V]
            den = tot[:, M_V:M_V + 1]
            floor = jnp.exp(-b_c - mm) * inv_scale
            h_ref[rs, hs] = num / jnp.maximum(jnp.abs(den), floor)
            mx = jnp.maximum(m_prev, r_max)
            kw = (kh.astype(F32) * jnp.exp(r_c - mx)).astype(BF16)
            c_scr[h] = (jnp.exp(m_prev - mx) * c_prev
                        + lax.dot_general(kw, v_ext, (((0,), (0,)), ((), ())),
                                          preferred_element_type=F32))
            m_scr[h:h + 1, :] = jnp.broadcast_to(g_tot + mx, (1, LANES))


def _mlstm(big, col, row, b, s, reverse):
    ng = s // LANES
    if reverse:
        blk = lambda bi, i: bi * ng + (ng - 1 - i)
    else:
        blk = lambda bi, i: bi * ng + i
    qkv = lambda c: pl.BlockSpec((LANES, M_WIDTH), lambda bi, i: (blk(bi, i), c))
    return pl.pallas_call(
        functools.partial(_mlstm_kernel, reverse=reverse),
        grid=(b, ng),
        in_specs=[qkv(1), qkv(2), qkv(3),
                  pl.BlockSpec((LANES, LANES), lambda bi, i: (blk(bi, i), 0)),
                  pl.BlockSpec((None, GATE_CH, LANES), lambda bi, i: (blk(bi, i), 0, 0))],
        out_specs=pl.BlockSpec((LANES, M_WIDTH), lambda bi, i: (blk(bi, i), 0)),
        out_shape=jax.ShapeDtypeStruct((b * s, M_WIDTH), F32),
        scratch_shapes=[pltpu.VMEM((M_HEADS, M_QK, 2 * M_V), F32),
                        pltpu.VMEM((M_HEADS, LANES), F32)],
        compiler_params=_params(("parallel", "arbitrary")),
        name="mlstm_bwd" if reverse else "mlstm_fwd",
    )(big, big, big, col, row)


def _merge_kernel(x_ref, ya_ref, hf_ref, hb_ref, om_ref, zm_ref, ga_ref, gb_ref, hn_ref,
                  woa_ref, wob_ref, wout_ref, nf_ref, o_ref, *, final_norm):
    hm = hf_ref[...] + hb_ref[...]
    parts = []
    for h in range(M_HEADS):
        hh = hm[:, h * M_V:(h + 1) * M_V]
        ms = jnp.mean(hh * hh, axis=-1, keepdims=True)
        parts.append(hh * lax.rsqrt(ms + EPS))
    hn = jnp.concatenate(parts, axis=1) * hn_ref[...]
    zm = zm_ref[...].astype(F32)
    yb = hn * jax.nn.sigmoid(om_ref[...].astype(F32)) * (zm * jax.nn.sigmoid(zm))
    pa = jnp.dot(ya_ref[...], woa_ref[...], preferred_element_type=F32)
    pb = jnp.dot(yb.astype(BF16), wob_ref[...], preferred_element_type=F32)
    merged = (jax.nn.sigmoid(ga_ref[...].astype(F32)) * pa
              + jax.nn.sigmoid(gb_ref[...].astype(F32)) * pb)
    out = x_ref[...] + jnp.dot(merged.astype(BF16), wout_ref[...], preferred_element_type=F32)
    if final_norm:
        out = _rms(out, nf_ref[...])
    o_ref[...] = out


def _merge(x2, ya, hf, hb, big, hn, woa, wob, wout, nf, final_norm):
    t = x2.shape[0]
    tm = min(256, t)
    row = lambda c: (lambda i: (i, c))
    const = lambda shape: pl.BlockSpec(shape, lambda i: (0, 0), pipeline_mode=pl.Buffered(1))
    return pl.pallas_call(
        functools.partial(_merge_kernel, final_norm=final_norm),
        grid=(t // tm,),
        in_specs=[pl.BlockSpec((tm, D_MODEL), row(0)),
                  pl.BlockSpec((tm, A_WIDTH), row(0)),
                  pl.BlockSpec((tm, M_WIDTH), row(0)),
                  pl.BlockSpec((tm, M_WIDTH), row(0)),
                  pl.BlockSpec((tm, M_WIDTH), row(4)),
                  pl.BlockSpec((tm, M_WIDTH), row(5)),
                  pl.BlockSpec((tm, D_MODEL), row(3)),
                  pl.BlockSpec((tm, D_MODEL), row(4)),
                  const((1, M_WIDTH)),
                  const((A_WIDTH, D_MODEL)),
                  const((M_WIDTH, D_MODEL)),
                  const((D_MODEL, D_MODEL)),
                  const((1, D_MODEL))],
        out_specs=pl.BlockSpec((tm, D_MODEL), row(0)),
        out_shape=jax.ShapeDtypeStruct((t, D_MODEL), F32),
        compiler_params=_params(("parallel",)),
        name="merge_out",
    )(x2, ya, hf, hb, big, big, big, big, hn, woa, wob, wout, nf)


def _rot_cols(w):
    half = QK_ROPE // 2
    return jnp.concatenate([-w[:, half:], w[:, :half]], axis=1)


def _pack_layer(w_in, w_uq, w_ukv):
    o = 0
    seg = {}
    for name, width in (("c_q", Q_LORA), ("c_kv", KV_LORA), ("k_rope", QK_ROPE), ("z_a", A_WIDTH),
                        ("q_m", M_WIDTH), ("k_m", M_WIDTH), ("v_m", M_WIDTH), ("o_m", M_WIDTH),
                        ("z_m", M_WIDTH), ("gates", 4 * M_HEADS), ("g_a", D_MODEL), ("g_b", D_MODEL)):
        seg[name] = w_in[:, o:o + width]
        o += width
    pad = jnp.zeros((D_MODEL, LANES - 4 * M_HEADS), w_in.dtype)
    w_small = jnp.concatenate([seg["c_q"], seg["c_kv"], seg["k_rope"], _rot_cols(seg["k_rope"]),
                               seg["gates"], pad], axis=1).astype(BF16)
    w_big = jnp.concatenate([seg[n] for n in ("z_a", "q_m", "k_m", "v_m", "o_m", "z_m", "g_a", "g_b")],
                            axis=1).astype(BF16)
    wq = w_uq.reshape(Q_LORA, A_HEADS, QK_NOPE + QK_ROPE)
    rope = wq[:, :, QK_NOPE:]
    rot = jnp.concatenate([-rope[:, :, QK_ROPE // 2:], rope[:, :, :QK_ROPE // 2]], axis=2)
    wq = jnp.concatenate([wq, rot], axis=2).reshape(Q_LORA, A_HEADS * QK_PAD).astype(BF16)
    return w_small, w_big, wq, w_ukv.astype(BF16)


def _rope_table(s):
    inv = ROPE_THETA ** (-jnp.arange(0, QK_ROPE, 2, dtype=F32) / QK_ROPE)
    ang = jnp.arange(s, dtype=F32)[:, None] * inv[None, :]
    cos, sin = jnp.cos(ang), jnp.sin(ang)
    return jnp.concatenate([cos, cos, sin, sin], axis=1)


def _trunk(x, layers, norm_f):
    b, s, _ = x.shape
    x2 = x.reshape(b * s, D_MODEL)
    cs = _rope_table(s)
    q_scale = float((QK_NOPE + QK_ROPE) ** -0.5 * 1.4426950408889634)
    for li, ly in enumerate(layers):
        small = _norm_matmul(x2, ly["norm_in"], ly["w_small"], F32, 512, SMALL_COLS)
        big = _norm_matmul(x2, ly["norm_in"], ly["w_big"], BF16, 1024, 1024)
        qc, kc, v = _mla_prep(small, cs, ly["q_a_norm"], ly["kv_a_norm"], ly["wq"], ly["wkv"],
                              b, s, q_scale)
        ya = _flash(qc, kc, v, big, b, s)
        col, row = _gates(small, ly["b_i"], ly["b_f"])
        hf = _mlstm(big, col, row, b, s, False)
        hb = _mlstm(big, col, row, b, s, True)
        x2 = _merge(x2, ya, hf, hb, big, ly["m_head_norm"], ly["w_oa"], ly["w_ob"], ly["w_out"],
                    norm_f, li == len(layers) - 1)
    return x2.reshape(b, s, D_MODEL)


def kernel(x_prompt, x_sample, norm_in, w_in, b_gates, q_a_norm, w_uq, kv_a_norm, w_ukv, w_oa,
           m_head_norm, w_ob, w_out, norm_f):
    layers = []
    for l in range(w_in.shape[0]):
        w_small, w_big, wq, wkv = _pack_layer(w_in[l], w_uq[l], w_ukv[l])
        layers.append(dict(
            norm_in=norm_in[l].reshape(1, D_MODEL), w_small=w_small, w_big=w_big, wq=wq, wkv=wkv,
            q_a_norm=q_a_norm[l].reshape(1, Q_LORA), kv_a_norm=kv_a_norm[l].reshape(1, KV_LORA),
            b_i=b_gates[l, :GATE_CH].reshape(GATE_CH, 1).astype(F32),
            b_f=b_gates[l, GATE_CH:].reshape(GATE_CH, 1).astype(F32),
            m_head_norm=m_head_norm[l].reshape(1, M_WIDTH),
            w_oa=w_oa[l].astype(BF16), w_ob=w_ob[l].astype(BF16), w_out=w_out[l].astype(BF16)))
    nf = norm_f.reshape(1, D_MODEL)
    return (_trunk(x_prompt, layers, nf), _trunk(x_sample, layers, nf))
```

```python
import functools

import jax
import jax.numpy as jnp
from jax import lax
from jax.experimental import pallas as pl
from jax.experimental.pallas import tpu as pltpu

D_MODEL = 2048
A_HEADS = 8
Q_LORA = 512
KV_LORA = 512
QK_NOPE = 128
QK_ROPE = 64
V_HEAD = 128
ROPE_THETA = 10000.0
A_WIDTH = A_HEADS * V_HEAD
M_HEADS = 8
M_QK = 128
M_V = 128
CHUNK = 64
M_WIDTH = M_HEADS * M_V
EPS = 1e-6

LANES = 128
QK_PAD = 256
KV_CHUNK = 512
FLASH_UNROLL = 6
VMEM_LIMIT = 56 * 1024 * 1024

SMALL_COLS = Q_LORA + KV_LORA + 2 * LANES
BIG_COLS = 6 * 1024 + 2 * D_MODEL

F32 = jnp.float32
BF16 = jnp.bfloat16


def _rms(xf, g):
    ms = jnp.mean(xf * xf, axis=-1, keepdims=True)
    return xf * lax.rsqrt(ms + EPS) * g


def _params(sem):
    return pltpu.CompilerParams(dimension_semantics=sem, vmem_limit_bytes=VMEM_LIMIT)


def _norm_matmul_kernel(x_ref, g_ref, w_ref, o_ref, h_scr, *, rows):
    @pl.when(pl.program_id(1) == 0)
    def _():
        def body(r, c):
            sl = pl.ds(pl.multiple_of(r * rows, rows), rows)
            h_scr[sl, :] = _rms(x_ref[sl, :], g_ref[...]).astype(h_scr.dtype)
            return c
        lax.fori_loop(0, x_ref.shape[0] // rows, body, 0)

    o_ref[...] = jnp.dot(h_scr[...], w_ref[...],
                         preferred_element_type=F32).astype(o_ref.dtype)


def _norm_matmul(x2, gain, w, out_dtype, tm, tn):
    t, k = x2.shape
    n = w.shape[1]
    tm = min(tm, t)
    rows = min(256, tm)
    return pl.pallas_call(
        functools.partial(_norm_matmul_kernel, rows=rows),
        grid=(t // tm, n // tn),
        in_specs=[pl.BlockSpec((tm, k), lambda i, j: (i, 0)),
                  pl.BlockSpec((1, k), lambda i, j: (0, 0)),
                  pl.BlockSpec((k, tn), lambda i, j: (0, j))],
        out_specs=pl.BlockSpec((tm, tn), lambda i, j: (i, j)),
        out_shape=jax.ShapeDtypeStruct((t, n), out_dtype),
        scratch_shapes=[pltpu.VMEM((tm, k), BF16)],
        compiler_params=_params(("parallel", "arbitrary")),
        name="in_proj",
    )(x2, gain, w)


_NT = (((1,), (1,)), ((), ()))


def _mla_prep_kernel(cq_ref, ckv_ref, rest_ref, cs_ref, cst_ref, gq_ref, gkv_ref,
                     wqt_ref, wk_ref, wvt_ref, qt_ref, k_ref, vt_ref, *, q_scale):
    hq = _rms(cq_ref[...], gq_ref[...]).astype(BF16)
    hkv = _rms(ckv_ref[...], gkv_ref[...]).astype(BF16)
    qt = lax.dot_general(wqt_ref[...], hq, _NT, preferred_element_type=F32)
    vt = lax.dot_general(wvt_ref[...], hkv, _NT, preferred_element_type=F32)
    kn = jnp.dot(hkv, wk_ref[...], preferred_element_type=F32)
    t = rest_ref[...] * cs_ref[...]
    lane = lax.broadcasted_iota(jnp.int32, t.shape, 1)
    k_r = jnp.where(lane < QK_ROPE, t + pltpu.roll(t, QK_ROPE, 1), 0.0)
    cst = cst_ref[...]
    pad = jnp.zeros((QK_PAD - QK_NOPE - QK_ROPE, qt.shape[1]), F32)
    for h in range(A_HEADS):
        qh = qt[h * QK_PAD:(h + 1) * QK_PAD]
        tq = qh[QK_NOPE:] * cst
        qt_ref[h] = jnp.concatenate(
            [qh[:QK_NOPE] * q_scale, (tq[:QK_ROPE] + tq[QK_ROPE:]) * q_scale, pad], axis=0).astype(BF16)
        k_ref[h] = jnp.concatenate([kn[:, h * QK_NOPE:(h + 1) * QK_NOPE], k_r], axis=1).astype(BF16)
        vt_ref[h] = vt[h * V_HEAD:(h + 1) * V_HEAD].astype(BF16)


def _mla_prep(small, cs, cst, gq, gkv, wqt, wk, wvt, b, s, q_scale):
    tm = min(KV_CHUNK, s // 2)
    nb = s // tm
    full = lambda bi, i: (0, 0)
    return pl.pallas_call(
        functools.partial(_mla_prep_kernel, q_scale=q_scale),
        grid=(b, nb),
        in_specs=[pl.BlockSpec((tm, Q_LORA), lambda bi, i: (bi * nb + i, 0)),
                  pl.BlockSpec((tm, KV_LORA), lambda bi, i: (bi * nb + i, 1)),
                  pl.BlockSpec((tm, LANES), lambda bi, i: (bi * nb + i, (Q_LORA + KV_LORA) // LANES)),
                  pl.BlockSpec((tm, LANES), lambda bi, i: (i, 0)),
                  pl.BlockSpec((LANES, tm), lambda bi, i: (0, i)),
                  pl.BlockSpec((1, Q_LORA), full),
                  pl.BlockSpec((1, KV_LORA), full),
                  pl.BlockSpec((A_HEADS * QK_PAD, Q_LORA), full),
                  pl.BlockSpec((KV_LORA, A_HEADS * QK_NOPE), full),
                  pl.BlockSpec((A_HEADS * V_HEAD, KV_LORA), full)],
        out_specs=[pl.BlockSpec((None, A_HEADS, QK_PAD, tm), lambda bi, i: (bi, 0, 0, i)),
                   pl.BlockSpec((None, A_HEADS, tm, QK_PAD), lambda bi, i: (bi, 0, i, 0)),
                   pl.BlockSpec((None, A_HEADS, None, V_HEAD, tm), lambda bi, i: (bi, 0, i, 0, 0))],
        out_shape=[jax.ShapeDtypeStruct((b, A_HEADS, QK_PAD, s), BF16),
                   jax.ShapeDtypeStruct((b, A_HEADS, s, QK_PAD), BF16),
                   jax.ShapeDtypeStruct((b, A_HEADS, nb, V_HEAD, tm), BF16)],
        compiler_params=_params(("parallel", "parallel")),
        name="mla_prep",
    )(small, small, small, cs, cst, gq, gkv, wqt, wk, wvt)


def _flash_kernel(qt_ref, k_ref, vt_ref, z_ref, o_ref, s_scr, p_scr, acc_scr):
    qt = qt_ref[...]
    tq = qt.shape[1]
    nkb = vt_ref.shape[0]
    tk = k_ref.shape[0] // nkb
    acc_scr[...] = jnp.zeros(acc_scr.shape, F32)

    def scores(kb, slot):
        start = kb * tk if isinstance(kb, int) else pl.multiple_of(kb * tk, tk)
        s_scr[slot] = jnp.dot(k_ref[pl.ds(start, tk), :], qt, preferred_element_type=F32)

    def softmax(slot, m_prev, l_prev):
        st = s_scr[slot]
        m_new = jnp.maximum(m_prev, jnp.max(st, axis=0, keepdims=True))
        alpha = jnp.exp2(m_prev - m_new)
        pt = jnp.exp2(st - m_new)
        p_scr[slot] = pt.astype(BF16)
        return m_new, alpha * l_prev + jnp.sum(pt, axis=0, keepdims=True), alpha

    def values(kb, slot, alpha):
        acc_scr[...] = alpha * acc_scr[...] + jnp.dot(vt_ref[kb], p_scr[slot],
                                                      preferred_element_type=F32)

    scores(0, 0)
    m, l, a = softmax(0, jnp.full((1, tq), -jnp.inf, F32), jnp.zeros((1, tq), F32))
    scores(1, 1)

    unroll = FLASH_UNROLL if (nkb - 2) % FLASH_UNROLL == 0 else 2

    def trip(j, carry):
        m, l, a = carry
        for u in range(unroll):
            kb = unroll * j + u
            scores(kb + 2, u % 2)
            m, l, a_next = softmax((u + 1) % 2, m, l)
            values(kb, u % 2, a)
            a = a_next
        return m, l, a

    m, l, a = lax.fori_loop(0, (nkb - 2) // unroll, trip, (m, l, a))
    m, l_fin, a1 = softmax(1, m, l)
    values(nkb - 2, 0, a)
    values(nkb - 1, 1, a1)
    z = z_ref[...].astype(F32)
    o = (acc_scr[...] / l_fin).T
    o_ref[...] = (o * (z * jax.nn.sigmoid(z))).astype(o_ref.dtype)


def _flash(qt, kc, vt, big, b, s):
    tq = min(512, s)
    nq = s // tq
    nkb, tk = vt.shape[2], vt.shape[4]
    return pl.pallas_call(
        _flash_kernel,
        grid=(b, A_HEADS, nq),
        in_specs=[pl.BlockSpec((None, None, QK_PAD, tq), lambda bi, h, i: (bi, h, 0, i)),
                  pl.BlockSpec((None, None, s, QK_PAD), lambda bi, h, i: (bi, h, 0, 0)),
                  pl.BlockSpec((None, None, nkb, V_HEAD, tk), lambda bi, h, i: (bi, h, 0, 0, 0)),
                  pl.BlockSpec((tq, V_HEAD), lambda bi, h, i: (bi * nq + i, h))],
        out_specs=pl.BlockSpec((tq, V_HEAD), lambda bi, h, i: (bi * nq + i, h)),
        out_shape=jax.ShapeDtypeStruct((b * s, A_WIDTH), BF16),
        scratch_shapes=[pltpu.VMEM((2, tk, tq), F32), pltpu.VMEM((2, tk, tq), BF16),
                        pltpu.VMEM((V_HEAD, tq), F32)],
        compiler_params=_params(("parallel", "parallel", "arbitrary")),
        name="mla_flash",
    )(qt, kc, vt, big)


GATE_CH = 2 * M_HEADS
GATE_GROUPS = 8


def _gate_kernel(x_ref, bi_ref, bf_ref, col_ref, row_ref):
    shape = (GATE_CH, LANES)
    pos = lax.broadcasted_iota(jnp.int32, shape, 1) % CHUNK
    is_fw = lax.broadcasted_iota(jnp.int32, shape, 0) < M_HEADS

    def seg_scan(y, op, ident):
        for sft in (1, 2, 4, 8, 16, 32):
            y_f = jnp.where(pos >= sft, pltpu.roll(y, sft, 1), ident)
            y_b = jnp.where(pos < CHUNK - sft, pltpu.roll(y, LANES - sft, 1), ident)
            y = op(y, jnp.where(is_fw, y_f, y_b))
        return y

    for g in range(row_ref.shape[0]):
        rs = slice(g * LANES, (g + 1) * LANES)
        xt = x_ref[rs, :].T
        log_i = xt[0:GATE_CH] + bi_ref[...]
        log_f = jax.nn.log_sigmoid(xt[GATE_CH:2 * GATE_CH] + bf_ref[...])
        b = seg_scan(log_f, jnp.add, 0.0)
        r = log_i - b
        cm = seg_scan(r, jnp.maximum, -jnp.inf)
        row_ref[g] = r
        pack = jnp.concatenate([b, r, cm, jnp.zeros((LANES - 3 * GATE_CH, LANES), F32)], axis=0)
        col_ref[rs, :] = pack.T


def _gates(small, b_i, b_f):
    t = small.shape[0]
    ng = t // LANES
    gpb = min(GATE_GROUPS, ng)
    return pl.pallas_call(
        _gate_kernel,
        grid=(ng // gpb,),
        in_specs=[pl.BlockSpec((gpb * LANES, LANES), lambda i: (i, SMALL_COLS // LANES - 1)),
                  pl.BlockSpec((GATE_CH, 1), lambda i: (0, 0)),
                  pl.BlockSpec((GATE_CH, 1), lambda i: (0, 0))],
        out_specs=[pl.BlockSpec((gpb * LANES, LANES), lambda i: (i, 0)),
                   pl.BlockSpec((gpb, GATE_CH, LANES), lambda i: (i, 0, 0))],
        out_shape=[jax.ShapeDtypeStruct((t, LANES), F32),
                   jax.ShapeDtypeStruct((ng, GATE_CH, LANES), F32)],
        compiler_params=_params(("parallel",)),
        name="mlstm_gates",
    )(small, b_i, b_f)


def _mlstm_kernel(q_ref, k_ref, v_ref, col_ref, row_ref, h_ref, c_scr, m_scr, *, reverse):
    @pl.when(pl.program_id(1) == 0)
    def _():
        c_scr[...] = jnp.zeros(c_scr.shape, F32)
        m_scr[...] = jnp.zeros(m_scr.shape, F32)

    d = 1 if reverse else 0
    jj = lax.broadcasted_iota(jnp.int32, (CHUNK, CHUNK), 0)
    ll = lax.broadcasted_iota(jnp.int32, (CHUNK, CHUNK), 1)
    mask = (ll >= jj) if reverse else (ll <= jj)
    ones_blk = (lax.broadcasted_iota(jnp.int32, (CHUNK, LANES), 1) == 0).astype(BF16)
    last = 0 if reverse else CHUNK - 1
    inv_scale = float(M_QK) ** 0.5

    for half in ((1, 0) if reverse else (0, 1)):
        rs = slice(half * CHUNK, (half + 1) * CHUNK)
        col = col_ref[rs, :]
        for h in range(M_HEADS):
            ch = d * M_HEADS + h
            hs = slice(h * M_QK, (h + 1) * M_QK)
            b_c = col[:, ch:ch + 1]
            r_c = col[:, GATE_CH + ch:GATE_CH + ch + 1]
            cm_c = col[:, 2 * GATE_CH + ch:2 * GATE_CH + ch + 1]
            r_row = row_ref[ch:ch + 1, rs]
            g_tot = b_c[last:last + 1]
            r_max = cm_c[last:last + 1]
            m_prev = m_scr[h:h + 1, 0:1]
            mm = jnp.maximum(m_prev, cm_c)
            qh, kh, vh = q_ref[rs, hs], k_ref[rs, hs], v_ref[rs, hs]
            qk = lax.dot_general(qh, kh, (((1,), (1,)), ((), ())), preferred_element_type=F32)
            p = (jnp.exp(jnp.where(mask, r_row - mm, -jnp.inf)) * qk).astype(BF16)
            v_ext = jnp.concatenate([vh, ones_blk], axis=1)
            c_prev = c_scr[h]
            tot = (jnp.dot(p, v_ext, preferred_element_type=F32)
                   + jnp.exp(m_prev - mm) * jnp.dot(qh, c_prev.astype(BF16),
                                                    preferred_element_type=F32))
            num = tot[:, :M_V]
            den = tot[:, M_V:M_V + 1]
            floor = jnp.exp(-b_c - mm) * inv_scale
            h_ref[rs, hs] = num / jnp.maximum(jnp.abs(den), floor)
            mx = jnp.maximum(m_prev, r_max)
            kw = (kh.astype(F32) * jnp.exp(r_c - mx)).astype(BF16)
            c_scr[h] = (jnp.exp(m_prev - mx) * c_prev
                        + lax.dot_general(kw, v_ext, (((0,), (0,)), ((), ())),
                                          preferred_element_type=F32))
            m_scr[h:h + 1, :] = jnp.broadcast_to(g_tot + mx, (1, LANES))


def _mlstm(big, col, row, b, s, reverse):
    ng = s // LANES
    if reverse:
        blk = lambda bi, i: bi * ng + (ng - 1 - i)
    else:
        blk = lambda bi, i: bi * ng + i
    qkv = lambda c: pl.BlockSpec((LANES, M_WIDTH), lambda bi, i: (blk(bi, i), c))
    return pl.pallas_call(
        functools.partial(_mlstm_kernel, reverse=reverse),
        grid=(b, ng),
        in_specs=[qkv(1), qkv(2), qkv(3),
                  pl.BlockSpec((LANES, LANES), lambda bi, i: (blk(bi, i), 0)),
                  pl.BlockSpec((None, GATE_CH, LANES), lambda bi, i: (blk(bi, i), 0, 0))],
        out_specs=pl.BlockSpec((LANES, M_WIDTH), lambda bi, i: (blk(bi, i), 0)),
        out_shape=jax.ShapeDtypeStruct((b * s, M_WIDTH), F32),
        scratch_shapes=[pltpu.VMEM((M_HEADS, M_QK, 2 * M_V), F32),
                        pltpu.VMEM((M_HEADS, LANES), F32)],
        compiler_params=_params(("parallel", "arbitrary")),
        name="mlstm_bwd" if reverse else "mlstm_fwd",
    )(big, big, big, col, row)


def _merge_kernel(x_ref, ya_ref, hf_ref, hb_ref, om_ref, zm_ref, ga_ref, gb_ref, hn_ref,
                  woa_ref, wob_ref, wout_ref, nf_ref, o_ref, *, final_norm):
    hm = hf_ref[...] + hb_ref[...]
    parts = []
    for h in range(M_HEADS):
        hh = hm[:, h * M_V:(h + 1) * M_V]
        ms = jnp.mean(hh * hh, axis=-1, keepdims=True)
        parts.append(hh * lax.rsqrt(ms + EPS))
    hn = jnp.concatenate(parts, axis=1) * hn_ref[...]
    zm = zm_ref[...].astype(F32)
    yb = hn * jax.nn.sigmoid(om_ref[...].astype(F32)) * (zm * jax.nn.sigmoid(zm))
    pa = jnp.dot(ya_ref[...], woa_ref[...], preferred_element_type=F32)
    pb = jnp.dot(yb.astype(BF16), wob_ref[...], preferred_element_type=F32)
    merged = (jax.nn.sigmoid(ga_ref[...].astype(F32)) * pa
              + jax.nn.sigmoid(gb_ref[...].astype(F32)) * pb)
    out = x_ref[...] + jnp.dot(merged.astype(BF16), wout_ref[...], preferred_element_type=F32)
    if final_norm:
        out = _rms(out, nf_ref[...])
    o_ref[...] = out


def _merge(x2, ya, hf, hb, big, hn, woa, wob, wout, nf, final_norm):
    t = x2.shape[0]
    tm = min(256, t)
    row = lambda c: (lambda i: (i, c))
    const = lambda shape: pl.BlockSpec(shape, lambda i: (0, 0), pipeline_mode=pl.Buffered(1))
    return pl.pallas_call(
        functools.partial(_merge_kernel, final_norm=final_norm),
        grid=(t // tm,),
        in_specs=[pl.BlockSpec((tm, D_MODEL), row(0)),
                  pl.BlockSpec((tm, A_WIDTH), row(0)),
                  pl.BlockSpec((tm, M_WIDTH), row(0)),
                  pl.BlockSpec((tm, M_WIDTH), row(0)),
                  pl.BlockSpec((tm, M_WIDTH), row(4)),
                  pl.BlockSpec((tm, M_WIDTH), row(5)),
                  pl.BlockSpec((tm, D_MODEL), row(3)),
                  pl.BlockSpec((tm, D_MODEL), row(4)),
                  const((1, M_WIDTH)),
                  const((A_WIDTH, D_MODEL)),
                  const((M_WIDTH, D_MODEL)),
                  const((D_MODEL, D_MODEL)),
                  const((1, D_MODEL))],
        out_specs=pl.BlockSpec((tm, D_MODEL), row(0)),
        out_shape=jax.ShapeDtypeStruct((t, D_MODEL), F32),
        compiler_params=_params(("parallel",)),
        name="merge_out",
    )(x2, ya, hf, hb, big, big, big, big, hn, woa, wob, wout, nf)


def _rot_cols(w):
    half = QK_ROPE // 2
    return jnp.concatenate([-w[:, half:], w[:, :half]], axis=1)


def _pack_layer(w_in, w_uq, w_ukv):
    o = 0
    seg = {}
    for name, width in (("c_q", Q_LORA), ("c_kv", KV_LORA), ("k_rope", QK_ROPE), ("z_a", A_WIDTH),
                        ("q_m", M_WIDTH), ("k_m", M_WIDTH), ("v_m", M_WIDTH), ("o_m", M_WIDTH),
                        ("z_m", M_WIDTH), ("gates", 4 * M_HEADS), ("g_a", D_MODEL), ("g_b", D_MODEL)):
        seg[name] = w_in[:, o:o + width]
        o += width
    pad = jnp.zeros((D_MODEL, LANES - 4 * M_HEADS), w_in.dtype)
    w_small = jnp.concatenate([seg["c_q"], seg["c_kv"], seg["k_rope"], _rot_cols(seg["k_rope"]),
                               seg["gates"], pad], axis=1).astype(BF16)
    w_big = jnp.concatenate([seg[n] for n in ("z_a", "q_m", "k_m", "v_m", "o_m", "z_m", "g_a", "g_b")],
                            axis=1).astype(BF16)
    wq = w_uq.reshape(Q_LORA, A_HEADS, QK_NOPE + QK_ROPE)
    rope = wq[:, :, QK_NOPE:]
    rot = jnp.concatenate([-rope[:, :, QK_ROPE // 2:], rope[:, :, :QK_ROPE // 2]], axis=2)
    wqt = jnp.concatenate([wq, rot], axis=2).reshape(Q_LORA, A_HEADS * QK_PAD).T.astype(BF16)
    wkv = w_ukv.reshape(KV_LORA, A_HEADS, QK_NOPE + V_HEAD)
    wk = wkv[:, :, :QK_NOPE].reshape(KV_LORA, A_HEADS * QK_NOPE).astype(BF16)
    wvt = wkv[:, :, QK_NOPE:].reshape(KV_LORA, A_HEADS * V_HEAD).T.astype(BF16)
    return w_small, w_big, wqt, wk, wvt


def _rope_table(s):
    inv = ROPE_THETA ** (-jnp.arange(0, QK_ROPE, 2, dtype=F32) / QK_ROPE)
    ang = jnp.arange(s, dtype=F32)[:, None] * inv[None, :]
    cos, sin = jnp.cos(ang), jnp.sin(ang)
    return jnp.concatenate([cos, cos, sin, sin], axis=1)


def _trunk(x, layers, norm_f):
    b, s, _ = x.shape
    x2 = x.reshape(b * s, D_MODEL)
    cs = _rope_table(s)
    cst = cs.T
    q_scale = float((QK_NOPE + QK_ROPE) ** -0.5 * 1.4426950408889634)
    for li, ly in enumerate(layers):
        small = _norm_matmul(x2, ly["norm_in"], ly["w_small"], F32, 512, SMALL_COLS)
        big = _norm_matmul(x2, ly["norm_in"], ly["w_big"], BF16, 1024, 1024)
        qt, kc, vt = _mla_prep(small, cs, cst, ly["q_a_norm"], ly["kv_a_norm"], ly["wqt"], ly["wk"],
                               ly["wvt"], b, s, q_scale)
        ya = _flash(qt, kc, vt, big, b, s)
        col, row = _gates(small, ly["b_i"], ly["b_f"])
        hf = _mlstm(big, col, row, b, s, False)
        hb = _mlstm(big, col, row, b, s, True)
        x2 = _merge(x2, ya, hf, hb, big, ly["m_head_norm"], ly["w_oa"], ly["w_ob"], ly["w_out"],
                    norm_f, li == len(layers) - 1)
    return x2.reshape(b, s, D_MODEL)


def kernel(x_prompt, x_sample, norm_in, w_in, b_gates, q_a_norm, w_uq, kv_a_norm, w_ukv, w_oa,
           m_head_norm, w_ob, w_out, norm_f):
    layers = []
    for l in range(w_in.shape[0]):
        w_small, w_big, wqt, wk, wvt = _pack_layer(w_in[l], w_uq[l], w_ukv[l])
        layers.append(dict(
            norm_in=norm_in[l].reshape(1, D_MODEL), w_small=w_small, w_big=w_big,
            wqt=wqt, wk=wk, wvt=wvt,
            q_a_norm=q_a_norm[l].reshape(1, Q_LORA), kv_a_norm=kv_a_norm[l].reshape(1, KV_LORA),
            b_i=b_gates[l, :GATE_CH].reshape(GATE_CH, 1).astype(F32),
            b_f=b_gates[l, GATE_CH:].reshape(GATE_CH, 1).astype(F32),
            m_head_norm=m_head_norm[l].reshape(1, M_WIDTH),
            w_oa=w_oa[l].astype(BF16), w_ob=w_ob[l].astype(BF16), w_out=w_out[l].astype(BF16)))
    nf = norm_f.reshape(1, D_MODEL)
    return (_trunk(x_prompt, layers, nf), _trunk(x_sample, layers, nf))
```

```python
import functools

import jax
import jax.numpy as jnp
from jax import lax
from jax.experimental import pallas as pl
from jax.experimental.pallas import tpu as pltpu

D_MODEL = 2048
A_HEADS = 8
Q_LORA = 512
KV_LORA = 512
QK_NOPE = 128
QK_ROPE = 64
V_HEAD = 128
ROPE_THETA = 10000.0
A_WIDTH = A_HEADS * V_HEAD
M_HEADS = 8
M_QK = 128
M_V = 128
CHUNK = 64
M_WIDTH = M_HEADS * M_V
EPS = 1e-6

LANES = 128
QK_PAD = 256
KV_CHUNK = 512
FLASH_UNROLL = 6
VMEM_LIMIT = 56 * 1024 * 1024

SMALL_COLS = Q_LORA + KV_LORA + 2 * LANES
BIG_COLS = 4 * 1024 + 2 * D_MODEL

F32 = jnp.float32
BF16 = jnp.bfloat16


def _rms(xf, g):
    ms = jnp.mean(xf * xf, axis=-1, keepdims=True)
    return xf * lax.rsqrt(ms + EPS) * g


def _params(sem):
    return pltpu.CompilerParams(dimension_semantics=sem, vmem_limit_bytes=VMEM_LIMIT)


_NT = (((1,), (1,)), ((), ()))


def _norm_matmul_kernel(x_ref, g_ref, w_ref, o_ref, h_scr, *, rows, feature_major):
    @pl.when(pl.program_id(1) == 0)
    def _():
        def body(r, c):
            sl = pl.ds(pl.multiple_of(r * rows, rows), rows)
            h_scr[sl, :] = _rms(x_ref[sl, :], g_ref[...]).astype(h_scr.dtype)
            return c
        lax.fori_loop(0, x_ref.shape[0] // rows, body, 0)

    if feature_major:
        out = lax.dot_general(w_ref[...], h_scr[...], _NT, preferred_element_type=F32)
    else:
        out = jnp.dot(h_scr[...], w_ref[...], preferred_element_type=F32)
    o_ref[...] = out.astype(o_ref.dtype)


def _norm_matmul(x2, gain, w, out_dtype, tm, tn, feature_major=False):
    t, k = x2.shape
    n = w.shape[0] if feature_major else w.shape[1]
    tm = min(tm, t)
    rows = min(256, tm)
    if feature_major:
        w_spec = pl.BlockSpec((tn, k), lambda i, j: (j, 0))
        o_spec = pl.BlockSpec((tn, tm), lambda i, j: (j, i))
        o_shape = (n, t)
    else:
        w_spec = pl.BlockSpec((k, tn), lambda i, j: (0, j))
        o_spec = pl.BlockSpec((tm, tn), lambda i, j: (i, j))
        o_shape = (t, n)
    return pl.pallas_call(
        functools.partial(_norm_matmul_kernel, rows=rows, feature_major=feature_major),
        grid=(t // tm, n // tn),
        in_specs=[pl.BlockSpec((tm, k), lambda i, j: (i, 0)),
                  pl.BlockSpec((1, k), lambda i, j: (0, 0)),
                  w_spec],
        out_specs=o_spec,
        out_shape=jax.ShapeDtypeStruct(o_shape, out_dtype),
        scratch_shapes=[pltpu.VMEM((tm, k), BF16)],
        compiler_params=_params(("parallel", "arbitrary")),
        name="in_proj_t" if feature_major else "in_proj",
    )(x2, gain, w)


def _mla_prep_kernel(cq_ref, ckv_ref, rest_ref, cs_ref, cst_ref, gq_ref, gkv_ref,
                     wqt_ref, wk_ref, wvt_ref, qt_ref, k_ref, vt_ref, *, q_scale):
    hq = _rms(cq_ref[...], gq_ref[...]).astype(BF16)
    hkv = _rms(ckv_ref[...], gkv_ref[...]).astype(BF16)
    qt = lax.dot_general(wqt_ref[...], hq, _NT, preferred_element_type=F32)
    vt = lax.dot_general(wvt_ref[...], hkv, _NT, preferred_element_type=F32)
    kn = jnp.dot(hkv, wk_ref[...], preferred_element_type=F32)
    t = rest_ref[...] * cs_ref[...]
    lane = lax.broadcasted_iota(jnp.int32, t.shape, 1)
    k_r = jnp.where(lane < QK_ROPE, t + pltpu.roll(t, QK_ROPE, 1), 0.0)
    cst = cst_ref[...]
    pad = jnp.zeros((QK_PAD - QK_NOPE - QK_ROPE, qt.shape[1]), F32)
    for h in range(A_HEADS):
        qh = qt[h * QK_PAD:(h + 1) * QK_PAD]
        tq = qh[QK_NOPE:] * cst
        qt_ref[h] = jnp.concatenate(
            [qh[:QK_NOPE] * q_scale, (tq[:QK_ROPE] + tq[QK_ROPE:]) * q_scale, pad], axis=0).astype(BF16)
        k_ref[h] = jnp.concatenate([kn[:, h * QK_NOPE:(h + 1) * QK_NOPE], k_r], axis=1).astype(BF16)
        vt_ref[h] = vt[h * V_HEAD:(h + 1) * V_HEAD].astype(BF16)


def _mla_prep(small, cs, cst, gq, gkv, wqt, wk, wvt, b, s, q_scale):
    tm = min(KV_CHUNK, s // 2)
    nb = s // tm
    full = lambda bi, i: (0, 0)
    return pl.pallas_call(
        functools.partial(_mla_prep_kernel, q_scale=q_scale),
        grid=(b, nb),
        in_specs=[pl.BlockSpec((tm, Q_LORA), lambda bi, i: (bi * nb + i, 0)),
                  pl.BlockSpec((tm, KV_LORA), lambda bi, i: (bi * nb + i, 1)),
                  pl.BlockSpec((tm, LANES), lambda bi, i: (bi * nb + i, (Q_LORA + KV_LORA) // LANES)),
                  pl.BlockSpec((tm, LANES), lambda bi, i: (i, 0)),
                  pl.BlockSpec((LANES, tm), lambda bi, i: (0, i)),
                  pl.BlockSpec((1, Q_LORA), full),
                  pl.BlockSpec((1, KV_LORA), full),
                  pl.BlockSpec((A_HEADS * QK_PAD, Q_LORA), full),
                  pl.BlockSpec((KV_LORA, A_HEADS * QK_NOPE), full),
                  pl.BlockSpec((A_HEADS * V_HEAD, KV_LORA), full)],
        out_specs=[pl.BlockSpec((None, A_HEADS, QK_PAD, tm), lambda bi, i: (bi, 0, 0, i)),
                   pl.BlockSpec((None, A_HEADS, tm, QK_PAD), lambda bi, i: (bi, 0, i, 0)),
                   pl.BlockSpec((None, A_HEADS, None, V_HEAD, tm), lambda bi, i: (bi, 0, i, 0, 0))],
        out_shape=[jax.ShapeDtypeStruct((b, A_HEADS, QK_PAD, s), BF16),
                   jax.ShapeDtypeStruct((b, A_HEADS, s, QK_PAD), BF16),
                   jax.ShapeDtypeStruct((b, A_HEADS, nb, V_HEAD, tm), BF16)],
        compiler_params=_params(("parallel", "parallel")),
        name="mla_prep",
    )(small, small, small, cs, cst, gq, gkv, wqt, wk, wvt)


def _flash_kernel(qt_ref, k_ref, vt_ref, z_ref, o_ref, s_scr, p_scr, acc_scr):
    qt = qt_ref[...]
    tq = qt.shape[1]
    nkb = vt_ref.shape[0]
    tk = k_ref.shape[0] // nkb
    acc_scr[...] = jnp.zeros(acc_scr.shape, F32)

    def scores(kb, slot):
        start = kb * tk if isinstance(kb, int) else pl.multiple_of(kb * tk, tk)
        s_scr[slot] = jnp.dot(k_ref[pl.ds(start, tk), :], qt, preferred_element_type=F32)

    def softmax(slot, m_prev, l_prev):
        st = s_scr[slot]
        m_new = jnp.maximum(m_prev, jnp.max(st, axis=0, keepdims=True))
        alpha = jnp.exp2(m_prev - m_new)
        pt = jnp.exp2(st - m_new)
        p_scr[slot] = pt.astype(BF16)
        return m_new, alpha * l_prev + jnp.sum(pt, axis=0, keepdims=True), alpha

    def values(kb, slot, alpha):
        acc_scr[...] = alpha * acc_scr[...] + jnp.dot(vt_ref[kb], p_scr[slot],
                                                      preferred_element_type=F32)

    scores(0, 0)
    m, l, a = softmax(0, jnp.full((1, tq), -jnp.inf, F32), jnp.zeros((1, tq), F32))
    scores(1, 1)

    unroll = FLASH_UNROLL if (nkb - 2) % FLASH_UNROLL == 0 else 2

    def trip(j, carry):
        m, l, a = carry
        for u in range(unroll):
            kb = unroll * j + u
            scores(kb + 2, u % 2)
            m, l, a_next = softmax((u + 1) % 2, m, l)
            values(kb, u % 2, a)
            a = a_next
        return m, l, a

    m, l, a = lax.fori_loop(0, (nkb - 2) // unroll, trip, (m, l, a))
    m, l_fin, a1 = softmax(1, m, l)
    values(nkb - 2, 0, a)
    values(nkb - 1, 1, a1)
    z = z_ref[...].astype(F32)
    o = (acc_scr[...] / l_fin).T
    o_ref[...] = (o * (z * jax.nn.sigmoid(z))).astype(o_ref.dtype)


def _flash(qt, kc, vt, big, b, s):
    tq = min(512, s)
    nq = s // tq
    nkb, tk = vt.shape[2], vt.shape[4]
    return pl.pallas_call(
        _flash_kernel,
        grid=(b, A_HEADS, nq),
        in_specs=[pl.BlockSpec((None, None, QK_PAD, tq), lambda bi, h, i: (bi, h, 0, i)),
                  pl.BlockSpec((None, None, s, QK_PAD), lambda bi, h, i: (bi, h, 0, 0)),
                  pl.BlockSpec((None, None, nkb, V_HEAD, tk), lambda bi, h, i: (bi, h, 0, 0, 0)),
                  pl.BlockSpec((tq, V_HEAD), lambda bi, h, i: (bi * nq + i, h))],
        out_specs=pl.BlockSpec((tq, V_HEAD), lambda bi, h, i: (bi * nq + i, h)),
        out_shape=jax.ShapeDtypeStruct((b * s, A_WIDTH), BF16),
        scratch_shapes=[pltpu.VMEM((2, tk, tq), F32), pltpu.VMEM((2, tk, tq), BF16),
                        pltpu.VMEM((V_HEAD, tq), F32)],
        compiler_params=_params(("parallel", "parallel", "arbitrary")),
        name="mla_flash",
    )(qt, kc, vt, big)


GATE_CH = 2 * M_HEADS
MCHUNK = 2 * LANES
GATE_CHUNKS = 4
ROW_B, ROW_R, ROW_CM, ROW_G, ROW_RMAX = (i * GATE_CH for i in range(5))
ROW_PACK = 5 * GATE_CH


def _gate_kernel(x_ref, bi_ref, bf_ref, col_ref, row_ref):
    shape = (GATE_CH, LANES)
    pos = lax.broadcasted_iota(jnp.int32, shape, 1)
    is_fw = lax.broadcasted_iota(jnp.int32, shape, 0) < M_HEADS
    shifts = (1, 2, 4, 8, 16, 32, 64)

    def scan(y, op, ident):
        for sft in shifts:
            y_f = jnp.where(pos >= sft, pltpu.roll(y, sft, 1), ident)
            y_b = jnp.where(pos < LANES - sft, pltpu.roll(y, LANES - sft, 1), ident)
            y = op(y, jnp.where(is_fw, y_f, y_b))
        return y

    def total(y, op):
        for sft in shifts:
            y = op(y, pltpu.roll(y, sft, 1))
        return y

    zpad = jnp.zeros((LANES - GATE_CH, LANES), F32)
    for c in range(row_ref.shape[0]):
        log_i, log_f = [], []
        for half in range(2):
            rs = slice(c * MCHUNK + half * LANES, c * MCHUNK + (half + 1) * LANES)
            xt = x_ref[rs, :].T
            log_i.append(xt[0:GATE_CH] + bi_ref[...])
            log_f.append(jax.nn.log_sigmoid(xt[GATE_CH:2 * GATE_CH] + bf_ref[...]))
        t0, t1 = total(log_f[0], jnp.add), total(log_f[1], jnp.add)
        b0 = scan(log_f[0], jnp.add, 0.0) + jnp.where(is_fw, 0.0, t1)
        b1 = scan(log_f[1], jnp.add, 0.0) + jnp.where(is_fw, t0, 0.0)
        r0, r1 = log_i[0] - b0, log_i[1] - b1
        a0, a1 = total(r0, jnp.maximum), total(r1, jnp.maximum)
        e0, e1 = scan(r0, jnp.maximum, -jnp.inf), scan(r1, jnp.maximum, -jnp.inf)
        cm0 = jnp.where(is_fw, e0, jnp.maximum(e0, a1))
        cm1 = jnp.where(is_fw, jnp.maximum(e1, a0), e1)
        g, rmax = t0 + t1, jnp.maximum(a0, a1)
        for off, v0, v1 in ((ROW_B, b0, b1), (ROW_R, r0, r1), (ROW_CM, cm0, cm1),
                            (ROW_G, g, g), (ROW_RMAX, rmax, rmax)):
            row_ref[c, off:off + GATE_CH, 0:LANES] = v0
            row_ref[c, off:off + GATE_CH, LANES:MCHUNK] = v1
        for half, r in ((0, r0), (1, r1)):
            rs = slice(c * MCHUNK + half * LANES, c * MCHUNK + (half + 1) * LANES)
            col_ref[rs, :] = jnp.concatenate([r, zpad], axis=0).T


def _gates(small, b_i, b_f):
    t = small.shape[0]
    nc = t // MCHUNK
    cpb = min(GATE_CHUNKS, nc)
    return pl.pallas_call(
        _gate_kernel,
        grid=(nc // cpb,),
        in_specs=[pl.BlockSpec((cpb * MCHUNK, LANES), lambda i: (i, SMALL_COLS // LANES - 1)),
                  pl.BlockSpec((GATE_CH, 1), lambda i: (0, 0)),
                  pl.BlockSpec((GATE_CH, 1), lambda i: (0, 0))],
        out_specs=[pl.BlockSpec((cpb * MCHUNK, LANES), lambda i: (i, 0)),
                   pl.BlockSpec((cpb, ROW_PACK, MCHUNK), lambda i: (i, 0, 0))],
        out_shape=[jax.ShapeDtypeStruct((t, LANES), F32),
                   jax.ShapeDtypeStruct((nc, ROW_PACK, MCHUNK), F32)],
        compiler_params=_params(("parallel",)),
        name="mlstm_gates",
    )(small, b_i, b_f)


V_EXT = M_V + 16


def _mlstm_kernel(qt_ref, k_ref, vt_ref, row_ref, col_ref, h_ref, c_scr, m_scr, *, reverse):
    @pl.when(pl.program_id(1) == 0)
    def _():
        c_scr[...] = jnp.zeros(c_scr.shape, F32)
        m_scr[...] = jnp.zeros(m_scr.shape, F32)

    d = 1 if reverse else 0
    ll = lax.broadcasted_iota(jnp.int32, (MCHUNK, MCHUNK), 0)
    jj = lax.broadcasted_iota(jnp.int32, (MCHUNK, MCHUNK), 1)
    mask = (ll >= jj) if reverse else (ll <= jj)
    ones_blk = (lax.broadcasted_iota(jnp.int32, (V_EXT - M_V, MCHUNK), 0) == 0).astype(F32)
    inv_scale = float(M_QK) ** 0.5
    col = col_ref[...]

    def row(off, ch):
        return row_ref[off + ch:off + ch + 1, :]

    for h in range(M_HEADS):
        ch = d * M_HEADS + h
        hs = slice(h * M_QK, (h + 1) * M_QK)
        m_prev = m_scr[h:h + 1, :]
        mm = jnp.maximum(m_prev, row(ROW_CM, ch))
        qt, k, vt = qt_ref[hs, :], k_ref[:, hs], vt_ref[hs, :]
        st = jnp.dot(k, qt, preferred_element_type=F32)
        r_col = col[:, ch:ch + 1]
        pt = (jnp.exp(jnp.where(mask, r_col - mm, -jnp.inf)) * st).astype(BF16)
        vt_ext = jnp.concatenate([vt, ones_blk.astype(BF16)], axis=0)
        c_prev = c_scr[h]
        tot = (jnp.dot(vt_ext, pt, preferred_element_type=F32)
               + jnp.exp(m_prev - mm) * jnp.dot(c_prev.astype(BF16), qt,
                                                preferred_element_type=F32))
        floor = jnp.exp(-row(ROW_B, ch) - mm) * inv_scale
        ht = tot[:M_V] / jnp.maximum(jnp.abs(tot[M_V:M_V + 1]), floor)
        h_ref[:, hs] = ht.T
        mx = jnp.maximum(m_prev, row(ROW_RMAX, ch))
        w_row = jnp.exp(row(ROW_R, ch) - mx)
        lhs = jnp.concatenate([(vt.astype(F32) * w_row).astype(BF16),
                               (ones_blk * w_row).astype(BF16)], axis=0)
        c_scr[h] = (jnp.exp(m_prev - mx)[:, :M_QK] * c_prev
                    + jnp.dot(lhs, k, preferred_element_type=F32))
        m_scr[h:h + 1, :] = row(ROW_G, ch) + mx


def _mlstm(qvt, big, row, col, b, s, reverse):
    nc = s // MCHUNK
    if reverse:
        blk = lambda bi, i: bi * nc + (nc - 1 - i)
    else:
        blk = lambda bi, i: bi * nc + i
    return pl.pallas_call(
        functools.partial(_mlstm_kernel, reverse=reverse),
        grid=(b, nc),
        in_specs=[pl.BlockSpec((M_WIDTH, MCHUNK), lambda bi, i: (0, blk(bi, i))),
                  pl.BlockSpec((MCHUNK, M_WIDTH), lambda bi, i: (blk(bi, i), 1)),
                  pl.BlockSpec((M_WIDTH, MCHUNK), lambda bi, i: (1, blk(bi, i))),
                  pl.BlockSpec((None, ROW_PACK, MCHUNK), lambda bi, i: (blk(bi, i), 0, 0)),
                  pl.BlockSpec((MCHUNK, LANES), lambda bi, i: (blk(bi, i), 0))],
        out_specs=pl.BlockSpec((MCHUNK, M_WIDTH), lambda bi, i: (blk(bi, i), 0)),
        out_shape=jax.ShapeDtypeStruct((b * s, M_WIDTH), F32),
        scratch_shapes=[pltpu.VMEM((M_HEADS, V_EXT, M_QK), F32),
                        pltpu.VMEM((M_HEADS, MCHUNK), F32)],
        compiler_params=_params(("parallel", "arbitrary")),
        name="mlstm_bwd" if reverse else "mlstm_fwd",
    )(qvt, big, qvt, row, col)


def _merge_kernel(x_ref, ya_ref, hf_ref, hb_ref, om_ref, zm_ref, ga_ref, gb_ref, hn_ref,
                  woa_ref, wob_ref, wout_ref, nf_ref, o_ref, *, final_norm):
    hm = hf_ref[...] + hb_ref[...]
    parts = []
    for h in range(M_HEADS):
        hh = hm[:, h * M_V:(h + 1) * M_V]
        ms = jnp.mean(hh * hh, axis=-1, keepdims=True)
        parts.append(hh * lax.rsqrt(ms + EPS))
    hn = jnp.concatenate(parts, axis=1) * hn_ref[...]
    zm = zm_ref[...].astype(F32)
    yb = hn * jax.nn.sigmoid(om_ref[...].astype(F32)) * (zm * jax.nn.sigmoid(zm))
    pa = jnp.dot(ya_ref[...], woa_ref[...], preferred_element_type=F32)
    pb = jnp.dot(yb.astype(BF16), wob_ref[...], preferred_element_type=F32)
    merged = (jax.nn.sigmoid(ga_ref[...].astype(F32)) * pa
              + jax.nn.sigmoid(gb_ref[...].astype(F32)) * pb)
    out = x_ref[...] + jnp.dot(merged.astype(BF16), wout_ref[...], preferred_element_type=F32)
    if final_norm:
        out = _rms(out, nf_ref[...])
    o_ref[...] = out


def _merge(x2, ya, hf, hb, big, hn, woa, wob, wout, nf, final_norm):
    t = x2.shape[0]
    tm = min(256, t)
    row = lambda c: (lambda i: (i, c))
    const = lambda shape: pl.BlockSpec(shape, lambda i: (0, 0), pipeline_mode=pl.Buffered(1))
    return pl.pallas_call(
        functools.partial(_merge_kernel, final_norm=final_norm),
        grid=(t // tm,),
        in_specs=[pl.BlockSpec((tm, D_MODEL), row(0)),
                  pl.BlockSpec((tm, A_WIDTH), row(0)),
                  pl.BlockSpec((tm, M_WIDTH), row(0)),
                  pl.BlockSpec((tm, M_WIDTH), row(0)),
                  pl.BlockSpec((tm, M_WIDTH), row(2)),
                  pl.BlockSpec((tm, M_WIDTH), row(3)),
                  pl.BlockSpec((tm, D_MODEL), row(2)),
                  pl.BlockSpec((tm, D_MODEL), row(3)),
                  const((1, M_WIDTH)),
                  const((A_WIDTH, D_MODEL)),
                  const((M_WIDTH, D_MODEL)),
                  const((D_MODEL, D_MODEL)),
                  const((1, D_MODEL))],
        out_specs=pl.BlockSpec((tm, D_MODEL), row(0)),
        out_shape=jax.ShapeDtypeStruct((t, D_MODEL), F32),
        compiler_params=_params(("parallel",)),
        name="merge_out",
    )(x2, ya, hf, hb, big, big, big, big, hn, woa, wob, wout, nf)


def _rot_cols(w):
    half = QK_ROPE // 2
    return jnp.concatenate([-w[:, half:], w[:, :half]], axis=1)


def _pack_layer(w_in, w_uq, w_ukv):
    o = 0
    seg = {}
    for name, width in (("c_q", Q_LORA), ("c_kv", KV_LORA), ("k_rope", QK_ROPE), ("z_a", A_WIDTH),
                        ("q_m", M_WIDTH), ("k_m", M_WIDTH), ("v_m", M_WIDTH), ("o_m", M_WIDTH),
                        ("z_m", M_WIDTH), ("gates", 4 * M_HEADS), ("g_a", D_MODEL), ("g_b", D_MODEL)):
        seg[name] = w_in[:, o:o + width]
        o += width
    pad = jnp.zeros((D_MODEL, LANES - 4 * M_HEADS), w_in.dtype)
    w_small = jnp.concatenate([seg["c_q"], seg["c_kv"], seg["k_rope"], _rot_cols(seg["k_rope"]),
                               seg["gates"], pad], axis=1).astype(BF16)
    w_big = jnp.concatenate([seg[n] for n in ("z_a", "k_m", "o_m", "z_m", "g_a", "g_b")],
                            axis=1).astype(BF16)
    w_qvt = jnp.concatenate([seg["q_m"], seg["v_m"]], axis=1).T.astype(BF16)
    wq = w_uq.reshape(Q_LORA, A_HEADS, QK_NOPE + QK_ROPE)
    rope = wq[:, :, QK_NOPE:]
    rot = jnp.concatenate([-rope[:, :, QK_ROPE // 2:], rope[:, :, :QK_ROPE // 2]], axis=2)
    wqt = jnp.concatenate([wq, rot], axis=2).reshape(Q_LORA, A_HEADS * QK_PAD).T.astype(BF16)
    wkv = w_ukv.reshape(KV_LORA, A_HEADS, QK_NOPE + V_HEAD)
    wk = wkv[:, :, :QK_NOPE].reshape(KV_LORA, A_HEADS * QK_NOPE).astype(BF16)
    wvt = wkv[:, :, QK_NOPE:].reshape(KV_LORA, A_HEADS * V_HEAD).T.astype(BF16)
    return w_small, w_big, w_qvt, wqt, wk, wvt


def _rope_table(s):
    inv = ROPE_THETA ** (-jnp.arange(0, QK_ROPE, 2, dtype=F32) / QK_ROPE)
    ang = jnp.arange(s, dtype=F32)[:, None] * inv[None, :]
    cos, sin = jnp.cos(ang), jnp.sin(ang)
    return jnp.concatenate([cos, cos, sin, sin], axis=1)


def _trunk(x, layers, norm_f):
    b, s, _ = x.shape
    x2 = x.reshape(b * s, D_MODEL)
    cs = _rope_table(s)
    cst = cs.T
    q_scale = float((QK_NOPE + QK_ROPE) ** -0.5 * 1.4426950408889634)
    for li, ly in enumerate(layers):
        small = _norm_matmul(x2, ly["norm_in"], ly["w_small"], F32, 512, SMALL_COLS)
        big = _norm_matmul(x2, ly["norm_in"], ly["w_big"], BF16, 1024, 1024)
        qt, kc, vt = _mla_prep(small, cs, cst, ly["q_a_norm"], ly["kv_a_norm"], ly["wqt"], ly["wk"],
                               ly["wvt"], b, s, q_scale)
        ya = _flash(qt, kc, vt, big, b, s)
        qvt = _norm_matmul(x2, ly["norm_in"], ly["w_qvt"], BF16, 1024, 1024, feature_major=True)
        col, row = _gates(small, ly["b_i"], ly["b_f"])
        hf = _mlstm(qvt, big, row, col, b, s, False)
        hb = _mlstm(qvt, big, row, col, b, s, True)
        x2 = _merge(x2, ya, hf, hb, big, ly["m_head_norm"], ly["w_oa"], ly["w_ob"], ly["w_out"],
                    norm_f, li == len(layers) - 1)
    return x2.reshape(b, s, D_MODEL)


def kernel(x_prompt, x_sample, norm_in, w_in, b_gates, q_a_norm, w_uq, kv_a_norm, w_ukv, w_oa,
           m_head_norm, w_ob, w_out, norm_f):
    layers = []
    for l in range(w_in.shape[0]):
        w_small, w_big, w_qvt, wqt, wk, wvt = _pack_layer(w_in[l], w_uq[l], w_ukv[l])
        layers.append(dict(
            norm_in=norm_in[l].reshape(1, D_MODEL), w_small=w_small, w_big=w_big, w_qvt=w_qvt,
            wqt=wqt, wk=wk, wvt=wvt,
            q_a_norm=q_a_norm[l].reshape(1, Q_LORA), kv_a_norm=kv_a_norm[l].reshape(1, KV_LORA),
            b_i=b_gates[l, :GATE_CH].reshape(GATE_CH, 1).astype(F32),
            b_f=b_gates[l, GATE_CH:].reshape(GATE_CH, 1).astype(F32),
            m_head_norm=m_head_norm[l].reshape(1, M_WIDTH),
            w_oa=w_oa[l].astype(BF16), w_ob=w_ob[l].astype(BF16), w_out=w_out[l].astype(BF16)))
    nf = norm_f.reshape(1, D_MODEL)
    return (_trunk(x_prompt, layers, nf), _trunk(x_sample, layers, nf))
```

```python
import functools

import jax
import jax.numpy as jnp
from jax import lax
from jax.experimental import pallas as pl
from jax.experimental.pallas import tpu as pltpu

D_MODEL = 2048
A_HEADS = 8
Q_LORA = 512
KV_LORA = 512
QK_NOPE = 128
QK_ROPE = 64
V_HEAD = 128
ROPE_THETA = 10000.0
A_WIDTH = A_HEADS * V_HEAD
M_HEADS = 8
M_QK = 128
M_V = 128
CHUNK = 64
M_WIDTH = M_HEADS * M_V
EPS = 1e-6

LANES = 128
QK_PAD = 256
KV_CHUNK = 512
FLASH_UNROLL = 6
FLASH_TQ = 512
AV_EXT = V_HEAD + 16
VMEM_LIMIT = 56 * 1024 * 1024

SMALL_COLS = Q_LORA + KV_LORA + 2 * LANES
BIG_COLS = 4 * 1024 + 2 * D_MODEL

F32 = jnp.float32
BF16 = jnp.bfloat16


def _rms(xf, g):
    ms = jnp.mean(xf * xf, axis=-1, keepdims=True)
    return xf * lax.rsqrt(ms + EPS) * g


def _params(sem, flags=None):
    return pltpu.CompilerParams(dimension_semantics=sem, vmem_limit_bytes=VMEM_LIMIT, flags=flags)


_NT = (((1,), (1,)), ((), ()))


def _norm_matmul_kernel(x_ref, g_ref, w_ref, o_ref, h_scr, *, rows, feature_major):
    @pl.when(pl.program_id(1) == 0)
    def _():
        def body(r, c):
            sl = pl.ds(pl.multiple_of(r * rows, rows), rows)
            h_scr[sl, :] = _rms(x_ref[sl, :], g_ref[...]).astype(h_scr.dtype)
            return c
        lax.fori_loop(0, x_ref.shape[0] // rows, body, 0)

    if feature_major:
        out = lax.dot_general(w_ref[...], h_scr[...], _NT, preferred_element_type=F32)
    else:
        out = jnp.dot(h_scr[...], w_ref[...], preferred_element_type=F32)
    o_ref[...] = out.astype(o_ref.dtype)


def _norm_matmul(x2, gain, w, out_dtype, tm, tn, feature_major=False):
    t, k = x2.shape
    n = w.shape[0] if feature_major else w.shape[1]
    tm = min(tm, t)
    rows = min(256, tm)
    if feature_major:
        w_spec = pl.BlockSpec((tn, k), lambda i, j: (j, 0))
        o_spec = pl.BlockSpec((tn, tm), lambda i, j: (j, i))
        o_shape = (n, t)
    else:
        w_spec = pl.BlockSpec((k, tn), lambda i, j: (0, j))
        o_spec = pl.BlockSpec((tm, tn), lambda i, j: (i, j))
        o_shape = (t, n)
    return pl.pallas_call(
        functools.partial(_norm_matmul_kernel, rows=rows, feature_major=feature_major),
        grid=(t // tm, n // tn),
        in_specs=[pl.BlockSpec((tm, k), lambda i, j: (i, 0)),
                  pl.BlockSpec((1, k), lambda i, j: (0, 0)),
                  w_spec],
        out_specs=o_spec,
        out_shape=jax.ShapeDtypeStruct(o_shape, out_dtype),
        scratch_shapes=[pltpu.VMEM((tm, k), BF16)],
        compiler_params=_params(("parallel", "arbitrary")),
        name="in_proj_t" if feature_major else "in_proj",
    )(x2, gain, w)


def _mla_prep_kernel(cq_ref, ckv_ref, rest_ref, cs_ref, cst_ref, gq_ref, gkv_ref,
                     wqt_ref, wk_ref, wvt_ref, qt_ref, k_ref, vt_ref, *, q_scale):
    hq = _rms(cq_ref[...], gq_ref[...]).astype(BF16)
    hkv = _rms(ckv_ref[...], gkv_ref[...]).astype(BF16)
    qt = lax.dot_general(wqt_ref[...], hq, _NT, preferred_element_type=F32)
    vt = lax.dot_general(wvt_ref[...], hkv, _NT, preferred_element_type=F32)
    kn = jnp.dot(hkv, wk_ref[...], preferred_element_type=F32)
    t = rest_ref[...] * cs_ref[...]
    lane = lax.broadcasted_iota(jnp.int32, t.shape, 1)
    k_r = jnp.where(lane < QK_ROPE, t + pltpu.roll(t, QK_ROPE, 1), 0.0)
    cst = cst_ref[...]
    pad = jnp.zeros((QK_PAD - QK_NOPE - QK_ROPE, qt.shape[1]), F32)
    ones_blk = (lax.broadcasted_iota(jnp.int32, (AV_EXT - V_HEAD, qt.shape[1]), 0) == 0).astype(BF16)
    for h in range(A_HEADS):
        qh = qt[h * QK_PAD:(h + 1) * QK_PAD]
        tq = qh[QK_NOPE:] * cst
        qt_ref[h] = jnp.concatenate(
            [qh[:QK_NOPE] * q_scale, (tq[:QK_ROPE] + tq[QK_ROPE:]) * q_scale, pad], axis=0).astype(BF16)
        k_ref[h] = jnp.concatenate([kn[:, h * QK_NOPE:(h + 1) * QK_NOPE], k_r], axis=1).astype(BF16)
        vt_ref[h] = jnp.concatenate([vt[h * V_HEAD:(h + 1) * V_HEAD].astype(BF16), ones_blk], axis=0)


def _mla_prep(small, cs, cst, gq, gkv, wqt, wk, wvt, b, s, q_scale):
    tm = min(KV_CHUNK, s // 2)
    nb = s // tm
    full = lambda bi, i: (0, 0)
    return pl.pallas_call(
        functools.partial(_mla_prep_kernel, q_scale=q_scale),
        grid=(b, nb),
        in_specs=[pl.BlockSpec((tm, Q_LORA), lambda bi, i: (bi * nb + i, 0)),
                  pl.BlockSpec((tm, KV_LORA), lambda bi, i: (bi * nb + i, 1)),
                  pl.BlockSpec((tm, LANES), lambda bi, i: (bi * nb + i, (Q_LORA + KV_LORA) // LANES)),
                  pl.BlockSpec((tm, LANES), lambda bi, i: (i, 0)),
                  pl.BlockSpec((LANES, tm), lambda bi, i: (0, i)),
                  pl.BlockSpec((1, Q_LORA), full),
                  pl.BlockSpec((1, KV_LORA), full),
                  pl.BlockSpec((A_HEADS * QK_PAD, Q_LORA), full),
                  pl.BlockSpec((KV_LORA, A_HEADS * QK_NOPE), full),
                  pl.BlockSpec((A_HEADS * V_HEAD, KV_LORA), full)],
        out_specs=[pl.BlockSpec((None, A_HEADS, QK_PAD, tm), lambda bi, i: (bi, 0, 0, i)),
                   pl.BlockSpec((None, A_HEADS, tm, QK_PAD), lambda bi, i: (bi, 0, i, 0)),
                   pl.BlockSpec((None, A_HEADS, None, AV_EXT, tm), lambda bi, i: (bi, 0, i, 0, 0))],
        out_shape=[jax.ShapeDtypeStruct((b, A_HEADS, QK_PAD, s), BF16),
                   jax.ShapeDtypeStruct((b, A_HEADS, s, QK_PAD), BF16),
                   jax.ShapeDtypeStruct((b, A_HEADS, nb, AV_EXT, tm), BF16)],
        compiler_params=_params(("parallel", "parallel")),
        name="mla_prep",
    )(small, small, small, cs, cst, gq, gkv, wqt, wk, wvt)


def _flash_kernel(qt_ref, k_ref, vt_ref, z_ref, o_ref, s_scr, acc_scr):
    qt = qt_ref[...]
    tq = qt.shape[1]
    nkb = vt_ref.shape[0]
    tk = k_ref.shape[0] // nkb
    acc_scr[...] = jnp.zeros(acc_scr.shape, F32)

    def scores(kb, slot):
        start = kb * tk if isinstance(kb, int) else pl.multiple_of(kb * tk, tk)
        st = jnp.dot(k_ref[pl.ds(start, tk), :], qt, preferred_element_type=F32)
        s_scr[slot] = st
        return jnp.max(st, axis=0, keepdims=True)

    def softmax_values(kb, slot, cmax, m_prev):
        m_new = jnp.maximum(m_prev, cmax)
        pt = jnp.exp2((s_scr[slot] - m_new).astype(BF16))
        acc_scr[...] = (jnp.exp2(m_prev - m_new) * acc_scr[...]
                        + jnp.dot(vt_ref[kb], pt, preferred_element_type=F32))
        return m_new

    m = jnp.full((1, tq), -jnp.inf, F32)
    cmax = scores(0, 0)
    unroll = FLASH_UNROLL if (nkb - 2) % FLASH_UNROLL == 0 else 2

    def trip(j, carry):
        m, cmax = carry
        for u in range(unroll):
            kb = unroll * j + u
            cmax_next = scores(kb + 1, (u + 1) % 2)
            m = softmax_values(kb, u % 2, cmax, m)
            cmax = cmax_next
        return m, cmax

    m, cmax = lax.fori_loop(0, (nkb - 2) // unroll, trip, (m, cmax))
    cmax_last = scores(nkb - 1, 1)
    m = softmax_values(nkb - 2, 0, cmax, m)
    m = softmax_values(nkb - 1, 1, cmax_last, m)
    z = z_ref[...].astype(F32)
    acc = acc_scr[...]
    o = (acc[:V_HEAD] / acc[V_HEAD:V_HEAD + 1]).T
    o_ref[...] = (o * (z * jax.nn.sigmoid(z))).astype(o_ref.dtype)


def _flash(qt, kc, vt, big, b, s):
    tq = min(FLASH_TQ, s)
    nq = s // tq
    nkb, tk = vt.shape[2], vt.shape[4]
    return pl.pallas_call(
        _flash_kernel,
        grid=(b, A_HEADS, nq),
        in_specs=[pl.BlockSpec((None, None, QK_PAD, tq), lambda bi, h, i: (bi, h, 0, i)),
                  pl.BlockSpec((None, None, s, QK_PAD), lambda bi, h, i: (bi, h, 0, 0)),
                  pl.BlockSpec((None, None, nkb, AV_EXT, tk), lambda bi, h, i: (bi, h, 0, 0, 0)),
                  pl.BlockSpec((tq, V_HEAD), lambda bi, h, i: (bi * nq + i, h))],
        out_specs=pl.BlockSpec((tq, V_HEAD), lambda bi, h, i: (bi * nq + i, h)),
        out_shape=jax.ShapeDtypeStruct((b * s, A_WIDTH), BF16),
        scratch_shapes=[pltpu.VMEM((2, tk, tq), F32),
                        pltpu.VMEM((AV_EXT, tq), F32)],
        compiler_params=_params(("parallel", "parallel", "arbitrary")),
        name="mla_flash",
    )(qt, kc, vt, big)


GATE_CH = 2 * M_HEADS
MCHUNK = 2 * LANES
GATE_CHUNKS = 4
ROW_B, ROW_R, ROW_CM, ROW_G, ROW_RMAX = (i * GATE_CH for i in range(5))
ROW_PACK = 5 * GATE_CH


def _gate_kernel(x_ref, bi_ref, bf_ref, col_ref, row_ref):
    shape = (GATE_CH, LANES)
    pos = lax.broadcasted_iota(jnp.int32, shape, 1)
    is_fw = lax.broadcasted_iota(jnp.int32, shape, 0) < M_HEADS
    shifts = (1, 2, 4, 8, 16, 32, 64)

    def scan(y, op, ident):
        for sft in shifts:
            y_f = jnp.where(pos >= sft, pltpu.roll(y, sft, 1), ident)
            y_b = jnp.where(pos < LANES - sft, pltpu.roll(y, LANES - sft, 1), ident)
            y = op(y, jnp.where(is_fw, y_f, y_b))
        return y

    def total(y, op):
        for sft in shifts:
            y = op(y, pltpu.roll(y, sft, 1))
        return y

    zpad = jnp.zeros((LANES - GATE_CH, LANES), F32)
    for c in range(row_ref.shape[0]):
        log_i, log_f = [], []
        for half in range(2):
            rs = slice(c * MCHUNK + half * LANES, c * MCHUNK + (half + 1) * LANES)
            xt = x_ref[rs, :].T
            log_i.append(xt[0:GATE_CH] + bi_ref[...])
            log_f.append(jax.nn.log_sigmoid(xt[GATE_CH:2 * GATE_CH] + bf_ref[...]))
        t0, t1 = total(log_f[0], jnp.add), total(log_f[1], jnp.add)
        b0 = scan(log_f[0], jnp.add, 0.0) + jnp.where(is_fw, 0.0, t1)
        b1 = scan(log_f[1], jnp.add, 0.0) + jnp.where(is_fw, t0, 0.0)
        r0, r1 = log_i[0] - b0, log_i[1] - b1
        a0, a1 = total(r0, jnp.maximum), total(r1, jnp.maximum)
        e0, e1 = scan(r0, jnp.maximum, -jnp.inf), scan(r1, jnp.maximum, -jnp.inf)
        cm0 = jnp.where(is_fw, e0, jnp.maximum(e0, a1))
        cm1 = jnp.where(is_fw, jnp.maximum(e1, a0), e1)
        g, rmax = t0 + t1, jnp.maximum(a0, a1)
        for off, v0, v1 in ((ROW_B, b0, b1), (ROW_R, r0, r1), (ROW_CM, cm0, cm1),
                            (ROW_G, g, g), (ROW_RMAX, rmax, rmax)):
            row_ref[c, off:off + GATE_CH, 0:LANES] = v0
            row_ref[c, off:off + GATE_CH, LANES:MCHUNK] = v1
        for half, r in ((0, r0), (1, r1)):
            rs = slice(c * MCHUNK + half * LANES, c * MCHUNK + (half + 1) * LANES)
            col_ref[rs, :] = jnp.concatenate([r, zpad], axis=0).T


def _gates(small, b_i, b_f):
    t = small.shape[0]
    nc = t // MCHUNK
    cpb = min(GATE_CHUNKS, nc)
    return pl.pallas_call(
        _gate_kernel,
        grid=(nc // cpb,),
        in_specs=[pl.BlockSpec((cpb * MCHUNK, LANES), lambda i: (i, SMALL_COLS // LANES - 1)),
                  pl.BlockSpec((GATE_CH, 1), lambda i: (0, 0)),
                  pl.BlockSpec((GATE_CH, 1), lambda i: (0, 0))],
        out_specs=[pl.BlockSpec((cpb * MCHUNK, LANES), lambda i: (i, 0)),
                   pl.BlockSpec((cpb, ROW_PACK, MCHUNK), lambda i: (i, 0, 0))],
        out_shape=[jax.ShapeDtypeStruct((t, LANES), F32),
                   jax.ShapeDtypeStruct((nc, ROW_PACK, MCHUNK), F32)],
        compiler_params=_params(("parallel",)),
        name="mlstm_gates",
    )(small, b_i, b_f)


V_EXT = M_V + 16


def _mlstm_kernel(qt_ref, k_ref, vt_ref, row_ref, col_ref, h_ref, c_scr, m_scr, *, reverse):
    @pl.when(pl.program_id(1) == 0)
    def _():
        c_scr[...] = jnp.zeros(c_scr.shape, F32)
        m_scr[...] = jnp.zeros(m_scr.shape, F32)

    d = 1 if reverse else 0
    ll = lax.broadcasted_iota(jnp.int32, (MCHUNK, MCHUNK), 0)
    jj = lax.broadcasted_iota(jnp.int32, (MCHUNK, MCHUNK), 1)
    mask = (ll >= jj) if reverse else (ll <= jj)
    ones_blk = (lax.broadcasted_iota(jnp.int32, (V_EXT - M_V, MCHUNK), 0) == 0).astype(F32)
    inv_scale = float(M_QK) ** 0.5
    col = col_ref[...]

    def row(off, ch):
        return row_ref[off + ch:off + ch + 1, :]

    for h in range(M_HEADS):
        ch = d * M_HEADS + h
        hs = slice(h * M_QK, (h + 1) * M_QK)
        m_prev = m_scr[h:h + 1, :]
        mm = jnp.maximum(m_prev, row(ROW_CM, ch))
        qt, k, vt = qt_ref[hs, :], k_ref[:, hs], vt_ref[hs, :]
        st = jnp.dot(k, qt, preferred_element_type=F32)
        r_col = col[:, ch:ch + 1]
        pt = (jnp.exp(jnp.where(mask, r_col - mm, -jnp.inf)) * st).astype(BF16)
        vt_ext = jnp.concatenate([vt, ones_blk.astype(BF16)], axis=0)
        c_prev = c_scr[h]
        tot = (jnp.dot(vt_ext, pt, preferred_element_type=F32)
               + jnp.exp(m_prev - mm) * jnp.dot(c_prev.astype(BF16), qt,
                                                preferred_element_type=F32))
        floor = jnp.exp(-row(ROW_B, ch) - mm) * inv_scale
        ht = tot[:M_V] / jnp.maximum(jnp.abs(tot[M_V:M_V + 1]), floor)
        h_ref[:, hs] = ht.T
        mx = jnp.maximum(m_prev, row(ROW_RMAX, ch))
        w_row = jnp.exp(row(ROW_R, ch) - mx)
        lhs = jnp.concatenate([(vt.astype(F32) * w_row).astype(BF16),
                               (ones_blk * w_row).astype(BF16)], axis=0)
        c_scr[h] = (jnp.exp(m_prev - mx)[:, :M_QK] * c_prev
                    + jnp.dot(lhs, k, preferred_element_type=F32))
        m_scr[h:h + 1, :] = row(ROW_G, ch) + mx


def _mlstm(qvt, big, row, col, b, s, reverse):
    nc = s // MCHUNK
    if reverse:
        blk = lambda bi, i: bi * nc + (nc - 1 - i)
    else:
        blk = lambda bi, i: bi * nc + i
    return pl.pallas_call(
        functools.partial(_mlstm_kernel, reverse=reverse),
        grid=(b, nc),
        in_specs=[pl.BlockSpec((M_WIDTH, MCHUNK), lambda bi, i: (0, blk(bi, i))),
                  pl.BlockSpec((MCHUNK, M_WIDTH), lambda bi, i: (blk(bi, i), 1)),
                  pl.BlockSpec((M_WIDTH, MCHUNK), lambda bi, i: (1, blk(bi, i))),
                  pl.BlockSpec((None, ROW_PACK, MCHUNK), lambda bi, i: (blk(bi, i), 0, 0)),
                  pl.BlockSpec((MCHUNK, LANES), lambda bi, i: (blk(bi, i), 0))],
        out_specs=pl.BlockSpec((MCHUNK, M_WIDTH), lambda bi, i: (blk(bi, i), 0)),
        out_shape=jax.ShapeDtypeStruct((b * s, M_WIDTH), F32),
        scratch_shapes=[pltpu.VMEM((M_HEADS, V_EXT, M_QK), F32),
                        pltpu.VMEM((M_HEADS, MCHUNK), F32)],
        compiler_params=_params(("parallel", "arbitrary")),
        name="mlstm_bwd" if reverse else "mlstm_fwd",
    )(qvt, big, qvt, row, col)


def _merge_kernel(x_ref, ya_ref, hf_ref, hb_ref, om_ref, zm_ref, ga_ref, gb_ref, hn_ref,
                  woa_ref, wob_ref, wout_ref, nf_ref, o_ref, *, final_norm):
    hm = hf_ref[...] + hb_ref[...]
    parts = []
    for h in range(M_HEADS):
        hh = hm[:, h * M_V:(h + 1) * M_V]
        ms = jnp.mean(hh * hh, axis=-1, keepdims=True)
        parts.append(hh * lax.rsqrt(ms + EPS))
    hn = jnp.concatenate(parts, axis=1) * hn_ref[...]
    zm = zm_ref[...].astype(F32)
    yb = hn * jax.nn.sigmoid(om_ref[...].astype(F32)) * (zm * jax.nn.sigmoid(zm))
    pa = jnp.dot(ya_ref[...], woa_ref[...], preferred_element_type=F32)
    pb = jnp.dot(yb.astype(BF16), wob_ref[...], preferred_element_type=F32)
    merged = (jax.nn.sigmoid(ga_ref[...].astype(F32)) * pa
              + jax.nn.sigmoid(gb_ref[...].astype(F32)) * pb)
    out = x_ref[...] + jnp.dot(merged.astype(BF16), wout_ref[...], preferred_element_type=F32)
    if final_norm:
        out = _rms(out, nf_ref[...])
    o_ref[...] = out


def _merge(x2, ya, hf, hb, big, hn, woa, wob, wout, nf, final_norm):
    t = x2.shape[0]
    tm = min(256, t)
    row = lambda c: (lambda i: (i, c))
    const = lambda shape: pl.BlockSpec(shape, lambda i: (0, 0), pipeline_mode=pl.Buffered(1))
    return pl.pallas_call(
        functools.partial(_merge_kernel, final_norm=final_norm),
        grid=(t // tm,),
        in_specs=[pl.BlockSpec((tm, D_MODEL), row(0)),
                  pl.BlockSpec((tm, A_WIDTH), row(0)),
                  pl.BlockSpec((tm, M_WIDTH), row(0)),
                  pl.BlockSpec((tm, M_WIDTH), row(0)),
                  pl.BlockSpec((tm, M_WIDTH), row(2)),
                  pl.BlockSpec((tm, M_WIDTH), row(3)),
                  pl.BlockSpec((tm, D_MODEL), row(2)),
                  pl.BlockSpec((tm, D_MODEL), row(3)),
                  const((1, M_WIDTH)),
                  const((A_WIDTH, D_MODEL)),
                  const((M_WIDTH, D_MODEL)),
                  const((D_MODEL, D_MODEL)),
                  const((1, D_MODEL))],
        out_specs=pl.BlockSpec((tm, D_MODEL), row(0)),
        out_shape=jax.ShapeDtypeStruct((t, D_MODEL), F32),
        compiler_params=_params(("parallel",)),
        name="merge_out",
    )(x2, ya, hf, hb, big, big, big, big, hn, woa, wob, wout, nf)


def _rot_cols(w):
    half = QK_ROPE // 2
    return jnp.concatenate([-w[:, half:], w[:, :half]], axis=1)


def _pack_layer(w_in, w_uq, w_ukv):
    o = 0
    seg = {}
    for name, width in (("c_q", Q_LORA), ("c_kv", KV_LORA), ("k_rope", QK_ROPE), ("z_a", A_WIDTH),
                        ("q_m", M_WIDTH), ("k_m", M_WIDTH), ("v_m", M_WIDTH), ("o_m", M_WIDTH),
                        ("z_m", M_WIDTH), ("gates", 4 * M_HEADS), ("g_a", D_MODEL), ("g_b", D_MODEL)):
        seg[name] = w_in[:, o:o + width]
        o += width
    pad = jnp.zeros((D_MODEL, LANES - 4 * M_HEADS), w_in.dtype)
    w_small = jnp.concatenate([seg["c_q"], seg["c_kv"], seg["k_rope"], _rot_cols(seg["k_rope"]),
                               seg["gates"], pad], axis=1).astype(BF16)
    w_big = jnp.concatenate([seg[n] for n in ("z_a", "k_m", "o_m", "z_m", "g_a", "g_b")],
                            axis=1).astype(BF16)
    w_qvt = jnp.concatenate([seg["q_m"], seg["v_m"]], axis=1).T.astype(BF16)
    wq = w_uq.reshape(Q_LORA, A_HEADS, QK_NOPE + QK_ROPE)
    rope = wq[:, :, QK_NOPE:]
    rot = jnp.concatenate([-rope[:, :, QK_ROPE // 2:], rope[:, :, :QK_ROPE // 2]], axis=2)
    wqt = jnp.concatenate([wq, rot], axis=2).reshape(Q_LORA, A_HEADS * QK_PAD).T.astype(BF16)
    wkv = w_ukv.reshape(KV_LORA, A_HEADS, QK_NOPE + V_HEAD)
    wk = wkv[:, :, :QK_NOPE].reshape(KV_LORA, A_HEADS * QK_NOPE).astype(BF16)
    wvt = wkv[:, :, QK_NOPE:].reshape(KV_LORA, A_HEADS * V_HEAD).T.astype(BF16)
    return w_small, w_big, w_qvt, wqt, wk, wvt


def _rope_table(s):
    inv = ROPE_THETA ** (-jnp.arange(0, QK_ROPE, 2, dtype=F32) / QK_ROPE)
    ang = jnp.arange(s, dtype=F32)[:, None] * inv[None, :]
    cos, sin = jnp.cos(ang), jnp.sin(ang)
    return jnp.concatenate([cos, cos, sin, sin], axis=1)


def _trunk(x, layers, norm_f):
    b, s, _ = x.shape
    x2 = x.reshape(b * s, D_MODEL)
    cs = _rope_table(s)
    cst = cs.T
    q_scale = float((QK_NOPE + QK_ROPE) ** -0.5 * 1.4426950408889634)
    for li, ly in enumerate(layers):
        small = _norm_matmul(x2, ly["norm_in"], ly["w_small"], F32, 512, SMALL_COLS)
        big = _norm_matmul(x2, ly["norm_in"], ly["w_big"], BF16, 1024, 1024)
        qt, kc, vt = _mla_prep(small, cs, cst, ly["q_a_norm"], ly["kv_a_norm"], ly["wqt"], ly["wk"],
                               ly["wvt"], b, s, q_scale)
        ya = _flash(qt, kc, vt, big, b, s)
        qvt = _norm_matmul(x2, ly["norm_in"], ly["w_qvt"], BF16, 1024, 1024, feature_major=True)
        col, row = _gates(small, ly["b_i"], ly["b_f"])
        hf = _mlstm(qvt, big, row, col, b, s, False)
        hb = _mlstm(qvt, big, row, col, b, s, True)
        x2 = _merge(x2, ya, hf, hb, big, ly["m_head_norm"], ly["w_oa"], ly["w_ob"], ly["w_out"],
                    norm_f, li == len(layers) - 1)
    return x2.reshape(b, s, D_MODEL)


def kernel(x_prompt, x_sample, norm_in, w_in, b_gates, q_a_norm, w_uq, kv_a_norm, w_ukv, w_oa,
           m_head_norm, w_ob, w_out, norm_f):
    layers = []
    for l in range(w_in.shape[0]):
        w_small, w_big, w_qvt, wqt, wk, wvt = _pack_layer(w_in[l], w_uq[l], w_ukv[l])
        layers.append(dict(
            norm_in=norm_in[l].reshape(1, D_MODEL), w_small=w_small, w_big=w_big, w_qvt=w_qvt,
            wqt=wqt, wk=wk, wvt=wvt,
            q_a_norm=q_a_norm[l].reshape(1, Q_LORA), kv_a_norm=kv_a_norm[l].reshape(1, KV_LORA),
            b_i=b_gates[l, :GATE_CH].reshape(GATE_CH, 1).astype(F32),
            b_f=b_gates[l, GATE_CH:].reshape(GATE_CH, 1).astype(F32),
            m_head_norm=m_head_norm[l].reshape(1, M_WIDTH),
            w_oa=w_oa[l].astype(BF16), w_ob=w_ob[l].astype(BF16), w_out=w_out[l].astype(BF16)))
    nf = norm_f.reshape(1, D_MODEL)
    return (_trunk(x_prompt, layers, nf), _trunk(x_sample, layers, nf))
```

```python
import functools

import jax
import jax.numpy as jnp
from jax import lax
from jax.experimental import pallas as pl
from jax.experimental.pallas import tpu as pltpu

D_MODEL = 2048
A_HEADS = 8
Q_LORA = 512
KV_LORA = 512
QK_NOPE = 128
QK_ROPE = 64
V_HEAD = 128
ROPE_THETA = 10000.0
A_WIDTH = A_HEADS * V_HEAD
M_HEADS = 8
M_QK = 128
M_V = 128
CHUNK = 64
M_WIDTH = M_HEADS * M_V
EPS = 1e-6

LANES = 128
QK_PAD = 256
KV_CHUNK = 512
FLASH_UNROLL = 8
FLASH_TQ = 512
AV_EXT = V_HEAD + 16
VMEM_LIMIT = 56 * 1024 * 1024

SMALL_COLS = Q_LORA + KV_LORA + 2 * LANES
BIG_COLS = 4 * 1024 + 2 * D_MODEL

F32 = jnp.float32
BF16 = jnp.bfloat16


def _rms(xf, g):
    ms = jnp.mean(xf * xf, axis=-1, keepdims=True)
    return xf * lax.rsqrt(ms + EPS) * g


def _params(sem, flags=None):
    return pltpu.CompilerParams(dimension_semantics=sem, vmem_limit_bytes=VMEM_LIMIT, flags=flags)


_NT = (((1,), (1,)), ((), ()))


def _norm_matmul_kernel(x_ref, g_ref, w_ref, o_ref, h_scr, *, rows, feature_major):
    @pl.when(pl.program_id(1) == 0)
    def _():
        def body(r, c):
            sl = pl.ds(pl.multiple_of(r * rows, rows), rows)
            h_scr[sl, :] = _rms(x_ref[sl, :], g_ref[...]).astype(h_scr.dtype)
            return c
        lax.fori_loop(0, x_ref.shape[0] // rows, body, 0)

    if feature_major:
        out = lax.dot_general(w_ref[...], h_scr[...], _NT, preferred_element_type=F32)
    else:
        out = jnp.dot(h_scr[...], w_ref[...], preferred_element_type=F32)
    o_ref[...] = out.astype(o_ref.dtype)


def _norm_matmul(x2, gain, w, out_dtype, tm, tn, feature_major=False):
    t, k = x2.shape
    n = w.shape[0] if feature_major else w.shape[1]
    tm = min(tm, t)
    rows = min(256, tm)
    if feature_major:
        w_spec = pl.BlockSpec((tn, k), lambda i, j: (j, 0))
        o_spec = pl.BlockSpec((tn, tm), lambda i, j: (j, i))
        o_shape = (n, t)
    else:
        w_spec = pl.BlockSpec((k, tn), lambda i, j: (0, j))
        o_spec = pl.BlockSpec((tm, tn), lambda i, j: (i, j))
        o_shape = (t, n)
    return pl.pallas_call(
        functools.partial(_norm_matmul_kernel, rows=rows, feature_major=feature_major),
        grid=(t // tm, n // tn),
        in_specs=[pl.BlockSpec((tm, k), lambda i, j: (i, 0)),
                  pl.BlockSpec((1, k), lambda i, j: (0, 0)),
                  w_spec],
        out_specs=o_spec,
        out_shape=jax.ShapeDtypeStruct(o_shape, out_dtype),
        scratch_shapes=[pltpu.VMEM((tm, k), BF16)],
        compiler_params=_params(("parallel", "arbitrary")),
        name="in_proj_t" if feature_major else "in_proj",
    )(x2, gain, w)


def _mla_prep_kernel(cq_ref, ckv_ref, rest_ref, cs_ref, cst_ref, gq_ref, gkv_ref,
                     wqt_ref, wk_ref, wvt_ref, qt_ref, k_ref, vt_ref, *, q_scale):
    hq = _rms(cq_ref[...], gq_ref[...]).astype(BF16)
    hkv = _rms(ckv_ref[...], gkv_ref[...]).astype(BF16)
    qt = lax.dot_general(wqt_ref[...], hq, _NT, preferred_element_type=F32)
    vt = lax.dot_general(wvt_ref[...], hkv, _NT, preferred_element_type=F32)
    kn = jnp.dot(hkv, wk_ref[...], preferred_element_type=F32)
    t = rest_ref[...] * cs_ref[...]
    lane = lax.broadcasted_iota(jnp.int32, t.shape, 1)
    k_r = jnp.where(lane < QK_ROPE, t + pltpu.roll(t, QK_ROPE, 1), 0.0)
    cst = cst_ref[...]
    pad = jnp.zeros((QK_PAD - QK_NOPE - QK_ROPE, qt.shape[1]), F32)
    ones_blk = (lax.broadcasted_iota(jnp.int32, (AV_EXT - V_HEAD, qt.shape[1]), 0) == 0).astype(BF16)
    for h in range(A_HEADS):
        qh = qt[h * QK_PAD:(h + 1) * QK_PAD]
        tq = qh[QK_NOPE:] * cst
        qt_ref[h] = jnp.concatenate(
            [qh[:QK_NOPE] * q_scale, (tq[:QK_ROPE] + tq[QK_ROPE:]) * q_scale, pad], axis=0).astype(BF16)
        k_ref[h] = jnp.concatenate([kn[:, h * QK_NOPE:(h + 1) * QK_NOPE], k_r], axis=1).astype(BF16)
        vt_ref[h] = jnp.concatenate([vt[h * V_HEAD:(h + 1) * V_HEAD].astype(BF16), ones_blk], axis=0)


def _mla_prep(small, cs, cst, gq, gkv, wqt, wk, wvt, b, s, q_scale):
    tm = min(KV_CHUNK, s // 2)
    nb = s // tm
    full = lambda bi, i: (0, 0)
    return pl.pallas_call(
        functools.partial(_mla_prep_kernel, q_scale=q_scale),
        grid=(b, nb),
        in_specs=[pl.BlockSpec((tm, Q_LORA), lambda bi, i: (bi * nb + i, 0)),
                  pl.BlockSpec((tm, KV_LORA), lambda bi, i: (bi * nb + i, 1)),
                  pl.BlockSpec((tm, LANES), lambda bi, i: (bi * nb + i, (Q_LORA + KV_LORA) // LANES)),
                  pl.BlockSpec((tm, LANES), lambda bi, i: (i, 0)),
                  pl.BlockSpec((LANES, tm), lambda bi, i: (0, i)),
                  pl.BlockSpec((1, Q_LORA), full),
                  pl.BlockSpec((1, KV_LORA), full),
                  pl.BlockSpec((A_HEADS * QK_PAD, Q_LORA), full),
                  pl.BlockSpec((KV_LORA, A_HEADS * QK_NOPE), full),
                  pl.BlockSpec((A_HEADS * V_HEAD, KV_LORA), full)],
        out_specs=[pl.BlockSpec((None, A_HEADS, QK_PAD, tm), lambda bi, i: (bi, 0, 0, i)),
                   pl.BlockSpec((None, A_HEADS, tm, QK_PAD), lambda bi, i: (bi, 0, i, 0)),
                   pl.BlockSpec((None, A_HEADS, None, AV_EXT, tm), lambda bi, i: (bi, 0, i, 0, 0))],
        out_shape=[jax.ShapeDtypeStruct((b, A_HEADS, QK_PAD, s), BF16),
                   jax.ShapeDtypeStruct((b, A_HEADS, s, QK_PAD), BF16),
                   jax.ShapeDtypeStruct((b, A_HEADS, nb, AV_EXT, tm), BF16)],
        compiler_params=_params(("parallel", "parallel")),
        name="mla_prep",
    )(small, small, small, cs, cst, gq, gkv, wqt, wk, wvt)


def _flash_kernel(qt_ref, qtn_ref, k_ref, vt_ref, z_ref, o_ref, q2_scr, s_scr, cmax_scr, acc_scr):
    tq = qt_ref.shape[1]
    nkb = vt_ref.shape[0]
    tk = k_ref.shape[0] // nkb
    unroll = FLASH_UNROLL if nkb % FLASH_UNROLL == 0 else 2
    acc_scr[...] = jnp.zeros(acc_scr.shape, F32)
    q2_scr[0] = qt_ref[...]
    q2_scr[1] = qtn_ref[...]

    def scores(kb, qsel, slot):
        start = kb * tk if isinstance(kb, int) else pl.multiple_of(kb * tk, tk)
        st = jnp.dot(k_ref[pl.ds(start, tk), :], q2_scr[qsel], preferred_element_type=F32)
        s_scr[slot] = st
        return jnp.max(st, axis=0, keepdims=True)

    @pl.when(pl.program_id(2) == 0)
    def _():
        cmax_scr[...] = scores(0, 0, 0)

    def softmax_values(kb, slot, cmax, m_prev):
        m_new = jnp.maximum(m_prev, cmax)
        pt = jnp.exp2((s_scr[slot] - m_new).astype(BF16))
        acc_scr[...] = (jnp.exp2(m_prev - m_new) * acc_scr[...]
                        + jnp.dot(vt_ref[kb], pt, preferred_element_type=F32))
        return m_new

    def trip(j, carry):
        m, cmax = carry
        for u in range(unroll):
            kb = unroll * j + u
            if u == unroll - 1:
                wrap = kb + 1 == nkb
                cmax_next = scores(jnp.where(wrap, 0, kb + 1), jnp.where(wrap, 1, 0), 0)
            else:
                cmax_next = scores(kb + 1, 0, (u + 1) % 2)
            m = softmax_values(kb, u % 2, cmax, m)
            cmax = cmax_next
        return m, cmax

    init = (jnp.full((1, tq), -jnp.inf, F32), cmax_scr[...])
    _, cmax = lax.fori_loop(0, nkb // unroll, trip, init)
    cmax_scr[...] = cmax
    z = z_ref[...].astype(F32)
    acc = acc_scr[...]
    o = (acc[:V_HEAD] / acc[V_HEAD:V_HEAD + 1]).T
    o_ref[...] = (o * (z * jax.nn.sigmoid(z))).astype(o_ref.dtype)


def _flash(qt, kc, vt, big, b, s):
    tq = min(FLASH_TQ, s)
    nq = s // tq
    nkb, tk = vt.shape[2], vt.shape[4]
    return pl.pallas_call(
        _flash_kernel,
        grid=(b, A_HEADS, nq),
        in_specs=[pl.BlockSpec((None, None, QK_PAD, tq), lambda bi, h, i: (bi, h, 0, i)),
                  pl.BlockSpec((None, None, QK_PAD, tq),
                               lambda bi, h, i: (bi, h, 0, jnp.minimum(i + 1, nq - 1))),
                  pl.BlockSpec((None, None, s, QK_PAD), lambda bi, h, i: (bi, h, 0, 0)),
                  pl.BlockSpec((None, None, nkb, AV_EXT, tk), lambda bi, h, i: (bi, h, 0, 0, 0)),
                  pl.BlockSpec((tq, V_HEAD), lambda bi, h, i: (bi * nq + i, h))],
        out_specs=pl.BlockSpec((tq, V_HEAD), lambda bi, h, i: (bi * nq + i, h)),
        out_shape=jax.ShapeDtypeStruct((b * s, A_WIDTH), BF16),
        scratch_shapes=[pltpu.VMEM((2, QK_PAD, tq), BF16),
                        pltpu.VMEM((2, tk, tq), F32),
                        pltpu.VMEM((1, tq), F32),
                        pltpu.VMEM((AV_EXT, tq), F32)],
        compiler_params=_params(("parallel", "parallel", "arbitrary")),
        name="mla_flash",
    )(qt, qt, kc, vt, big)


GATE_CH = 2 * M_HEADS
MCHUNK = 2 * LANES
GATE_CHUNKS = 4
ROW_B, ROW_R, ROW_CM, ROW_G, ROW_RMAX = (i * GATE_CH for i in range(5))
ROW_PACK = 5 * GATE_CH


def _gate_kernel(x_ref, bi_ref, bf_ref, col_ref, row_ref):
    shape = (GATE_CH, LANES)
    pos = lax.broadcasted_iota(jnp.int32, shape, 1)
    is_fw = lax.broadcasted_iota(jnp.int32, shape, 0) < M_HEADS
    shifts = (1, 2, 4, 8, 16, 32, 64)

    def scan(y, op, ident):
        for sft in shifts:
            y_f = jnp.where(pos >= sft, pltpu.roll(y, sft, 1), ident)
            y_b = jnp.where(pos < LANES - sft, pltpu.roll(y, LANES - sft, 1), ident)
            y = op(y, jnp.where(is_fw, y_f, y_b))
        return y

    def total(y, op):
        for sft in shifts:
            y = op(y, pltpu.roll(y, sft, 1))
        return y

    zpad = jnp.zeros((LANES - GATE_CH, LANES), F32)
    for c in range(row_ref.shape[0]):
        log_i, log_f = [], []
        for half in range(2):
            rs = slice(c * MCHUNK + half * LANES, c * MCHUNK + (half + 1) * LANES)
            xt = x_ref[rs, :].T
            log_i.append(xt[0:GATE_CH] + bi_ref[...])
            log_f.append(jax.nn.log_sigmoid(xt[GATE_CH:2 * GATE_CH] + bf_ref[...]))
        t0, t1 = total(log_f[0], jnp.add), total(log_f[1], jnp.add)
        b0 = scan(log_f[0], jnp.add, 0.0) + jnp.where(is_fw, 0.0, t1)
        b1 = scan(log_f[1], jnp.add, 0.0) + jnp.where(is_fw, t0, 0.0)
        r0, r1 = log_i[0] - b0, log_i[1] - b1
        a0, a1 = total(r0, jnp.maximum), total(r1, jnp.maximum)
        e0, e1 = scan(r0, jnp.maximum, -jnp.inf), scan(r1, jnp.maximum, -jnp.inf)
        cm0 = jnp.where(is_fw, e0, jnp.maximum(e0, a1))
        cm1 = jnp.where(is_fw, jnp.maximum(e1, a0), e1)
        g, rmax = t0 + t1, jnp.maximum(a0, a1)
        for off, v0, v1 in ((ROW_B, b0, b1), (ROW_R, r0, r1), (ROW_CM, cm0, cm1),
                            (ROW_G, g, g), (ROW_RMAX, rmax, rmax)):
            row_ref[c, off:off + GATE_CH, 0:LANES] = v0
            row_ref[c, off:off + GATE_CH, LANES:MCHUNK] = v1
        for half, r in ((0, r0), (1, r1)):
            rs = slice(c * MCHUNK + half * LANES, c * MCHUNK + (half + 1) * LANES)
            col_ref[rs, :] = jnp.concatenate([r, zpad], axis=0).T


def _gates(small, b_i, b_f):
    t = small.shape[0]
    nc = t // MCHUNK
    cpb = min(GATE_CHUNKS, nc)
    return pl.pallas_call(
        _gate_kernel,
        grid=(nc // cpb,),
        in_specs=[pl.BlockSpec((cpb * MCHUNK, LANES), lambda i: (i, SMALL_COLS // LANES - 1)),
                  pl.BlockSpec((GATE_CH, 1), lambda i: (0, 0)),
                  pl.BlockSpec((GATE_CH, 1), lambda i: (0, 0))],
        out_specs=[pl.BlockSpec((cpb * MCHUNK, LANES), lambda i: (i, 0)),
                   pl.BlockSpec((cpb, ROW_PACK, MCHUNK), lambda i: (i, 0, 0))],
        out_shape=[jax.ShapeDtypeStruct((t, LANES), F32),
                   jax.ShapeDtypeStruct((nc, ROW_PACK, MCHUNK), F32)],
        compiler_params=_params(("parallel",)),
        name="mlstm_gates",
    )(small, b_i, b_f)


V_EXT = M_V + 16


def _mlstm_kernel(qt_ref, k_ref, vt_ref, row_ref, col_ref, h_ref, c_scr, m_scr, *, reverse):
    @pl.when(pl.program_id(1) == 0)
    def _():
        c_scr[...] = jnp.zeros(c_scr.shape, F32)
        m_scr[...] = jnp.zeros(m_scr.shape, F32)

    d = 1 if reverse else 0
    ll = lax.broadcasted_iota(jnp.int32, (MCHUNK, MCHUNK), 0)
    jj = lax.broadcasted_iota(jnp.int32, (MCHUNK, MCHUNK), 1)
    mask = (ll >= jj) if reverse else (ll <= jj)
    ones_blk = (lax.broadcasted_iota(jnp.int32, (V_EXT - M_V, MCHUNK), 0) == 0).astype(F32)
    inv_scale = float(M_QK) ** 0.5
    col = col_ref[...]

    def row(off, ch):
        return row_ref[off + ch:off + ch + 1, :]

    for h in range(M_HEADS):
        ch = d * M_HEADS + h
        hs = slice(h * M_QK, (h + 1) * M_QK)
        m_prev = m_scr[h:h + 1, :]
        mm = jnp.maximum(m_prev, row(ROW_CM, ch))
        qt, k, vt = qt_ref[hs, :], k_ref[:, hs], vt_ref[hs, :]
        st = jnp.dot(k, qt, preferred_element_type=F32)
        r_col = col[:, ch:ch + 1]
        pt = (jnp.exp(jnp.where(mask, r_col - mm, -jnp.inf)) * st).astype(BF16)
        vt_ext = jnp.concatenate([vt, ones_blk.astype(BF16)], axis=0)
        c_prev = c_scr[h]
        tot = (jnp.dot(vt_ext, pt, preferred_element_type=F32)
               + jnp.exp(m_prev - mm) * jnp.dot(c_prev.astype(BF16), qt,
                                                preferred_element_type=F32))
        floor = jnp.exp(-row(ROW_B, ch) - mm) * inv_scale
        ht = tot[:M_V] / jnp.maximum(jnp.abs(tot[M_V:M_V + 1]), floor)
        h_ref[:, hs] = ht.T
        mx = jnp.maximum(m_prev, row(ROW_RMAX, ch))
        w_row = jnp.exp(row(ROW_R, ch) - mx)
        lhs = jnp.concatenate([(vt.astype(F32) * w_row).astype(BF16),
                               (ones_blk * w_row).astype(BF16)], axis=0)
        c_scr[h] = (jnp.exp(m_prev - mx)[:, :M_QK] * c_prev
                    + jnp.dot(lhs, k, preferred_element_type=F32))
        m_scr[h:h + 1, :] = row(ROW_G, ch) + mx


def _mlstm(qvt, big, row, col, b, s, reverse):
    nc = s // MCHUNK
    if reverse:
        blk = lambda bi, i: bi * nc + (nc - 1 - i)
    else:
        blk = lambda bi, i: bi * nc + i
    return pl.pallas_call(
        functools.partial(_mlstm_kernel, reverse=reverse),
        grid=(b, nc),
        in_specs=[pl.BlockSpec((M_WIDTH, MCHUNK), lambda bi, i: (0, blk(bi, i))),
                  pl.BlockSpec((MCHUNK, M_WIDTH), lambda bi, i: (blk(bi, i), 1)),
                  pl.BlockSpec((M_WIDTH, MCHUNK), lambda bi, i: (1, blk(bi, i))),
                  pl.BlockSpec((None, ROW_PACK, MCHUNK), lambda bi, i: (blk(bi, i), 0, 0)),
                  pl.BlockSpec((MCHUNK, LANES), lambda bi, i: (blk(bi, i), 0))],
        out_specs=pl.BlockSpec((MCHUNK, M_WIDTH), lambda bi, i: (blk(bi, i), 0)),
        out_shape=jax.ShapeDtypeStruct((b * s, M_WIDTH), F32),
        scratch_shapes=[pltpu.VMEM((M_HEADS, V_EXT, M_QK), F32),
                        pltpu.VMEM((M_HEADS, MCHUNK), F32)],
        compiler_params=_params(("parallel", "arbitrary")),
        name="mlstm_bwd" if reverse else "mlstm_fwd",
    )(qvt, big, qvt, row, col)


def _merge_kernel(x_ref, ya_ref, hf_ref, hb_ref, om_ref, zm_ref, ga_ref, gb_ref, hn_ref,
                  woa_ref, wob_ref, wout_ref, nf_ref, o_ref, *, final_norm):
    hm = hf_ref[...] + hb_ref[...]
    parts = []
    for h in range(M_HEADS):
        hh = hm[:, h * M_V:(h + 1) * M_V]
        ms = jnp.mean(hh * hh, axis=-1, keepdims=True)
        parts.append(hh * lax.rsqrt(ms + EPS))
    hn = jnp.concatenate(parts, axis=1) * hn_ref[...]
    zm = zm_ref[...].astype(F32)
    yb = hn * jax.nn.sigmoid(om_ref[...].astype(F32)) * (zm * jax.nn.sigmoid(zm))
    pa = jnp.dot(ya_ref[...], woa_ref[...], preferred_element_type=F32)
    pb = jnp.dot(yb.astype(BF16), wob_ref[...], preferred_element_type=F32)
    merged = (jax.nn.sigmoid(ga_ref[...].astype(F32)) * pa
              + jax.nn.sigmoid(gb_ref[...].astype(F32)) * pb)
    out = x_ref[...] + jnp.dot(merged.astype(BF16), wout_ref[...], preferred_element_type=F32)
    if final_norm:
        out = _rms(out, nf_ref[...])
    o_ref[...] = out


def _merge(x2, ya, hf, hb, big, hn, woa, wob, wout, nf, final_norm):
    t = x2.shape[0]
    tm = min(256, t)
    row = lambda c: (lambda i: (i, c))
    const = lambda shape: pl.BlockSpec(shape, lambda i: (0, 0), pipeline_mode=pl.Buffered(1))
    return pl.pallas_call(
        functools.partial(_merge_kernel, final_norm=final_norm),
        grid=(t // tm,),
        in_specs=[pl.BlockSpec((tm, D_MODEL), row(0)),
                  pl.BlockSpec((tm, A_WIDTH), row(0)),
                  pl.BlockSpec((tm, M_WIDTH), row(0)),
                  pl.BlockSpec((tm, M_WIDTH), row(0)),
                  pl.BlockSpec((tm, M_WIDTH), row(2)),
                  pl.BlockSpec((tm, M_WIDTH), row(3)),
                  pl.BlockSpec((tm, D_MODEL), row(2)),
                  pl.BlockSpec((tm, D_MODEL), row(3)),
                  const((1, M_WIDTH)),
                  const((A_WIDTH, D_MODEL)),
                  const((M_WIDTH, D_MODEL)),
                  const((D_MODEL, D_MODEL)),
                  const((1, D_MODEL))],
        out_specs=pl.BlockSpec((tm, D_MODEL), row(0)),
        out_shape=jax.ShapeDtypeStruct((t, D_MODEL), F32),
        compiler_params=_params(("parallel",)),
        name="merge_out",
    )(x2, ya, hf, hb, big, big, big, big, hn, woa, wob, wout, nf)


def _rot_cols(w):
    half = QK_ROPE // 2
    return jnp.concatenate([-w[:, half:], w[:, :half]], axis=1)


def _pack_layer(w_in, w_uq, w_ukv):
    o = 0
    seg = {}
    for name, width in (("c_q", Q_LORA), ("c_kv", KV_LORA), ("k_rope", QK_ROPE), ("z_a", A_WIDTH),
                        ("q_m", M_WIDTH), ("k_m", M_WIDTH), ("v_m", M_WIDTH), ("o_m", M_WIDTH),
                        ("z_m", M_WIDTH), ("gates", 4 * M_HEADS), ("g_a", D_MODEL), ("g_b", D_MODEL)):
        seg[name] = w_in[:, o:o + width]
        o += width
    pad = jnp.zeros((D_MODEL, LANES - 4 * M_HEADS), w_in.dtype)
    w_small = jnp.concatenate([seg["c_q"], seg["c_kv"], seg["k_rope"], _rot_cols(seg["k_rope"]),
                               seg["gates"], pad], axis=1).astype(BF16)
    w_big = jnp.concatenate([seg[n] for n in ("z_a", "k_m", "o_m", "z_m", "g_a", "g_b")],
                            axis=1).astype(BF16)
    w_qvt = jnp.concatenate([seg["q_m"], seg["v_m"]], axis=1).T.astype(BF16)
    wq = w_uq.reshape(Q_LORA, A_HEADS, QK_NOPE + QK_ROPE)
    rope = wq[:, :, QK_NOPE:]
    rot = jnp.concatenate([-rope[:, :, QK_ROPE // 2:], rope[:, :, :QK_ROPE // 2]], axis=2)
    wqt = jnp.concatenate([wq, rot], axis=2).reshape(Q_LORA, A_HEADS * QK_PAD).T.astype(BF16)
    wkv = w_ukv.reshape(KV_LORA, A_HEADS, QK_NOPE + V_HEAD)
    wk = wkv[:, :, :QK_NOPE].reshape(KV_LORA, A_HEADS * QK_NOPE).astype(BF16)
    wvt = wkv[:, :, QK_NOPE:].reshape(KV_LORA, A_HEADS * V_HEAD).T.astype(BF16)
    return w_small, w_big, w_qvt, wqt, wk, wvt


def _rope_table(s):
    inv = ROPE_THETA ** (-jnp.arange(0, QK_ROPE, 2, dtype=F32) / QK_ROPE)
    ang = jnp.arange(s, dtype=F32)[:, None] * inv[None, :]
    cos, sin = jnp.cos(ang), jnp.sin(ang)
    return jnp.concatenate([cos, cos, sin, sin], axis=1)


def _trunk(x, layers, norm_f):
    b, s, _ = x.shape
    x2 = x.reshape(b * s, D_MODEL)
    cs = _rope_table(s)
    cst = cs.T
    q_scale = float((QK_NOPE + QK_ROPE) ** -0.5 * 1.4426950408889634)
    for li, ly in enumerate(layers):
        small = _norm_matmul(x2, ly["norm_in"], ly["w_small"], F32, 512, SMALL_COLS)
        big = _norm_matmul(x2, ly["norm_in"], ly["w_big"], BF16, 1024, 1024)
        qt, kc, vt = _mla_prep(small, cs, cst, ly["q_a_norm"], ly["kv_a_norm"], ly["wqt"], ly["wk"],
                               ly["wvt"], b, s, q_scale)
        ya = _flash(qt, kc, vt, big, b, s)
        qvt = _norm_matmul(x2, ly["norm_in"], ly["w_qvt"], BF16, 1024, 1024, feature_major=True)
        col, row = _gates(small, ly["b_i"], ly["b_f"])
        hf = _mlstm(qvt, big, row, col, b, s, False)
        hb = _mlstm(qvt, big, row, col, b, s, True)
        x2 = _merge(x2, ya, hf, hb, big, ly["m_head_norm"], ly["w_oa"], ly["w_ob"], ly["w_out"],
                    norm_f, li == len(layers) - 1)
    return x2.reshape(b, s, D_MODEL)


def kernel(x_prompt, x_sample, norm_in, w_in, b_gates, q_a_norm, w_uq, kv_a_norm, w_ukv, w_oa,
           m_head_norm, w_ob, w_out, norm_f):
    layers = []
    for l in range(w_in.shape[0]):
        w_small, w_big, w_qvt, wqt, wk, wvt = _pack_layer(w_in[l], w_uq[l], w_ukv[l])
        layers.append(dict(
            norm_in=norm_in[l].reshape(1, D_MODEL), w_small=w_small, w_big=w_big, w_qvt=w_qvt,
            wqt=wqt, wk=wk, wvt=wvt,
            q_a_norm=q_a_norm[l].reshape(1, Q_LORA), kv_a_norm=kv_a_norm[l].reshape(1, KV_LORA),
            b_i=b_gates[l, :GATE_CH].reshape(GATE_CH, 1).astype(F32),
            b_f=b_gates[l, GATE_CH:].reshape(GATE_CH, 1).astype(F32),
            m_head_norm=m_head_norm[l].reshape(1, M_WIDTH),
            w_oa=w_oa[l].astype(BF16), w_ob=w_ob[l].astype(BF16), w_out=w_out[l].astype(BF16)))
    nf = norm_f.reshape(1, D_MODEL)
    return (_trunk(x_prompt, layers, nf), _trunk(x_sample, layers, nf))
```

```python
import functools

import jax
import jax.numpy as jnp
from jax import lax
from jax.experimental import pallas as pl
from jax.experimental.pallas import tpu as pltpu

D_MODEL = 2048
A_HEADS = 8
Q_LORA = 512
KV_LORA = 512
QK_NOPE = 128
QK_ROPE = 64
V_HEAD = 128
ROPE_THETA = 10000.0
A_WIDTH = A_HEADS * V_HEAD
M_HEADS = 8
M_QK = 128
M_V = 128
CHUNK = 64
M_WIDTH = M_HEADS * M_V
EPS = 1e-6

LANES = 128
QK_PAD = 256
KV_CHUNK = 512
FLASH_UNROLL = 16
FLASH_TQ = 512
AV_EXT = V_HEAD + 16
VMEM_LIMIT = 56 * 1024 * 1024

SMALL_COLS = Q_LORA + KV_LORA + 2 * LANES
BIG_COLS = 4 * 1024 + 2 * D_MODEL

F32 = jnp.float32
BF16 = jnp.bfloat16


def _rms(xf, g):
    ms = jnp.mean(xf * xf, axis=-1, keepdims=True)
    return xf * lax.rsqrt(ms + EPS) * g


def _params(sem, flags=None):
    return pltpu.CompilerParams(dimension_semantics=sem, vmem_limit_bytes=VMEM_LIMIT, flags=flags)


_NT = (((1,), (1,)), ((), ()))


def _norm_matmul_kernel(x_ref, g_ref, w_ref, o_ref, h_scr, *, rows, feature_major):
    @pl.when(pl.program_id(1) == 0)
    def _():
        def body(r, c):
            sl = pl.ds(pl.multiple_of(r * rows, rows), rows)
            h_scr[sl, :] = _rms(x_ref[sl, :], g_ref[...]).astype(h_scr.dtype)
            return c
        lax.fori_loop(0, x_ref.shape[0] // rows, body, 0)

    if feature_major:
        out = lax.dot_general(w_ref[...], h_scr[...], _NT, preferred_element_type=F32)
    else:
        out = jnp.dot(h_scr[...], w_ref[...], preferred_element_type=F32)
    o_ref[...] = out.astype(o_ref.dtype)


def _norm_matmul(x2, gain, w, out_dtype, tm, tn, feature_major=False):
    t, k = x2.shape
    n = w.shape[0] if feature_major else w.shape[1]
    tm = min(tm, t)
    rows = min(256, tm)
    if feature_major:
        w_spec = pl.BlockSpec((tn, k), lambda i, j: (j, 0))
        o_spec = pl.BlockSpec((tn, tm), lambda i, j: (j, i))
        o_shape = (n, t)
    else:
        w_spec = pl.BlockSpec((k, tn), lambda i, j: (0, j))
        o_spec = pl.BlockSpec((tm, tn), lambda i, j: (i, j))
        o_shape = (t, n)
    return pl.pallas_call(
        functools.partial(_norm_matmul_kernel, rows=rows, feature_major=feature_major),
        grid=(t // tm, n // tn),
        in_specs=[pl.BlockSpec((tm, k), lambda i, j: (i, 0)),
                  pl.BlockSpec((1, k), lambda i, j: (0, 0)),
                  w_spec],
        out_specs=o_spec,
        out_shape=jax.ShapeDtypeStruct(o_shape, out_dtype),
        scratch_shapes=[pltpu.VMEM((tm, k), BF16)],
        compiler_params=_params(("parallel", "arbitrary")),
        name="in_proj_t" if feature_major else "in_proj",
    )(x2, gain, w)


def _mla_prep_kernel(cq_ref, ckv_ref, rest_ref, cs_ref, cst_ref, gq_ref, gkv_ref,
                     wqt_ref, wk_ref, wvt_ref, qt_ref, k_ref, vt_ref, *, q_scale):
    hq = _rms(cq_ref[...], gq_ref[...]).astype(BF16)
    hkv = _rms(ckv_ref[...], gkv_ref[...]).astype(BF16)
    qt = lax.dot_general(wqt_ref[...], hq, _NT, preferred_element_type=F32)
    vt = lax.dot_general(wvt_ref[...], hkv, _NT, preferred_element_type=F32)
    kn = jnp.dot(hkv, wk_ref[...], preferred_element_type=F32)
    t = rest_ref[...] * cs_ref[...]
    lane = lax.broadcasted_iota(jnp.int32, t.shape, 1)
    k_r = jnp.where(lane < QK_ROPE, t + pltpu.roll(t, QK_ROPE, 1), 0.0)
    cst = cst_ref[...]
    pad = jnp.zeros((QK_PAD - QK_NOPE - QK_ROPE, qt.shape[1]), F32)
    ones_blk = (lax.broadcasted_iota(jnp.int32, (AV_EXT - V_HEAD, qt.shape[1]), 0) == 0).astype(BF16)
    for h in range(A_HEADS):
        qh = qt[h * QK_PAD:(h + 1) * QK_PAD]
        tq = qh[QK_NOPE:] * cst
        qt_ref[h] = jnp.concatenate(
            [qh[:QK_NOPE] * q_scale, (tq[:QK_ROPE] + tq[QK_ROPE:]) * q_scale, pad], axis=0).astype(BF16)
        k_ref[h] = jnp.concatenate([kn[:, h * QK_NOPE:(h + 1) * QK_NOPE], k_r], axis=1).astype(BF16)
        vt_ref[h] = jnp.concatenate([vt[h * V_HEAD:(h + 1) * V_HEAD].astype(BF16), ones_blk], axis=0)


def _mla_prep(small, cs, cst, gq, gkv, wqt, wk, wvt, b, s, q_scale):
    tm = min(KV_CHUNK, s // 2)
    nb = s // tm
    full = lambda bi, i: (0, 0)
    return pl.pallas_call(
        functools.partial(_mla_prep_kernel, q_scale=q_scale),
        grid=(b, nb),
        in_specs=[pl.BlockSpec((tm, Q_LORA), lambda bi, i: (bi * nb + i, 0)),
                  pl.BlockSpec((tm, KV_LORA), lambda bi, i: (bi * nb + i, 1)),
                  pl.BlockSpec((tm, LANES), lambda bi, i: (bi * nb + i, (Q_LORA + KV_LORA) // LANES)),
                  pl.BlockSpec((tm, LANES), lambda bi, i: (i, 0)),
                  pl.BlockSpec((LANES, tm), lambda bi, i: (0, i)),
                  pl.BlockSpec((1, Q_LORA), full),
                  pl.BlockSpec((1, KV_LORA), full),
                  pl.BlockSpec((A_HEADS * QK_PAD, Q_LORA), full),
                  pl.BlockSpec((KV_LORA, A_HEADS * QK_NOPE), full),
                  pl.BlockSpec((A_HEADS * V_HEAD, KV_LORA), full)],
        out_specs=[pl.BlockSpec((None, A_HEADS, QK_PAD, tm), lambda bi, i: (bi, 0, 0, i)),
                   pl.BlockSpec((None, A_HEADS, tm, QK_PAD), lambda bi, i: (bi, 0, i, 0)),
                   pl.BlockSpec((None, A_HEADS, None, AV_EXT, tm), lambda bi, i: (bi, 0, i, 0, 0))],
        out_shape=[jax.ShapeDtypeStruct((b, A_HEADS, QK_PAD, s), BF16),
                   jax.ShapeDtypeStruct((b, A_HEADS, s, QK_PAD), BF16),
                   jax.ShapeDtypeStruct((b, A_HEADS, nb, AV_EXT, tm), BF16)],
        compiler_params=_params(("parallel", "parallel")),
        name="mla_prep",
    )(small, small, small, cs, cst, gq, gkv, wqt, wk, wvt)


def _flash_kernel(qt_ref, qtn_ref, k_ref, vt_ref, z_ref, o_ref, q2_scr, s_scr, cmax_scr, acc_scr):
    tq = qt_ref.shape[1]
    nkb = vt_ref.shape[0]
    tk = k_ref.shape[0] // nkb
    unroll = next(u for u in (FLASH_UNROLL, 8, 2) if nkb % u == 0)
    acc_scr[...] = jnp.zeros(acc_scr.shape, F32)
    q2_scr[0] = qt_ref[...]
    q2_scr[1] = qtn_ref[...]

    def scores(kb, qsel, slot):
        start = kb * tk if isinstance(kb, int) else pl.multiple_of(kb * tk, tk)
        st = jnp.dot(k_ref[pl.ds(start, tk), :], q2_scr[qsel], preferred_element_type=F32)
        s_scr[slot] = st
        return jnp.max(st, axis=0, keepdims=True)

    @pl.when(pl.program_id(2) == 0)
    def _():
        cmax_scr[...] = scores(0, 0, 0)

    def softmax_values(kb, slot, cmax, m_prev):
        m_new = jnp.maximum(m_prev, cmax)
        pt = jnp.exp2((s_scr[slot] - m_new).astype(BF16))
        acc_scr[...] = (jnp.exp2(m_prev - m_new) * acc_scr[...]
                        + jnp.dot(vt_ref[kb], pt, preferred_element_type=F32))
        return m_new

    def trip(j, carry):
        m, cmax = carry
        for u in range(unroll):
            kb = unroll * j + u
            if u == unroll - 1:
                wrap = kb + 1 == nkb
                cmax_next = scores(jnp.where(wrap, 0, kb + 1), jnp.where(wrap, 1, 0), 0)
            else:
                cmax_next = scores(kb + 1, 0, (u + 1) % 2)
            m = softmax_values(kb, u % 2, cmax, m)
            cmax = cmax_next
        return m, cmax

    init = (jnp.full((1, tq), -jnp.inf, F32), cmax_scr[...])
    _, cmax = lax.fori_loop(0, nkb // unroll, trip, init)
    cmax_scr[...] = cmax
    z = z_ref[...].astype(F32)
    acc = acc_scr[...]
    o = (acc[:V_HEAD] / acc[V_HEAD:V_HEAD + 1]).T
    o_ref[...] = (o * (z * jax.nn.sigmoid(z))).astype(o_ref.dtype)


def _flash(qt, kc, vt, big, b, s):
    tq = min(FLASH_TQ, s)
    nq = s // tq
    nkb, tk = vt.shape[2], vt.shape[4]
    return pl.pallas_call(
        _flash_kernel,
        grid=(b, A_HEADS, nq),
        in_specs=[pl.BlockSpec((None, None, QK_PAD, tq), lambda bi, h, i: (bi, h, 0, i)),
                  pl.BlockSpec((None, None, QK_PAD, tq),
                               lambda bi, h, i: (bi, h, 0, jnp.minimum(i + 1, nq - 1))),
                  pl.BlockSpec((None, None, s, QK_PAD), lambda bi, h, i: (bi, h, 0, 0)),
                  pl.BlockSpec((None, None, nkb, AV_EXT, tk), lambda bi, h, i: (bi, h, 0, 0, 0)),
                  pl.BlockSpec((tq, V_HEAD), lambda bi, h, i: (bi * nq + i, h))],
        out_specs=pl.BlockSpec((tq, V_HEAD), lambda bi, h, i: (bi * nq + i, h)),
        out_shape=jax.ShapeDtypeStruct((b * s, A_WIDTH), BF16),
        scratch_shapes=[pltpu.VMEM((2, QK_PAD, tq), BF16),
                        pltpu.VMEM((2, tk, tq), F32),
                        pltpu.VMEM((1, tq), F32),
                        pltpu.VMEM((AV_EXT, tq), F32)],
        compiler_params=_params(("parallel", "parallel", "arbitrary")),
        name="mla_flash",
    )(qt, qt, kc, vt, big)


GATE_CH = 2 * M_HEADS
MCHUNK = 2 * LANES
GATE_CHUNKS = 4
ROW_B, ROW_R, ROW_CM, ROW_G, ROW_RMAX = (i * GATE_CH for i in range(5))
ROW_PACK = 5 * GATE_CH


def _gate_kernel(x_ref, bi_ref, bf_ref, col_ref, row_ref):
    shape = (GATE_CH, LANES)
    pos = lax.broadcasted_iota(jnp.int32, shape, 1)
    is_fw = lax.broadcasted_iota(jnp.int32, shape, 0) < M_HEADS
    shifts = (1, 2, 4, 8, 16, 32, 64)

    def scan(y, op, ident):
        for sft in shifts:
            y_f = jnp.where(pos >= sft, pltpu.roll(y, sft, 1), ident)
            y_b = jnp.where(pos < LANES - sft, pltpu.roll(y, LANES - sft, 1), ident)
            y = op(y, jnp.where(is_fw, y_f, y_b))
        return y

    def total(y, op):
        for sft in shifts:
            y = op(y, pltpu.roll(y, sft, 1))
        return y

    zpad = jnp.zeros((LANES - GATE_CH, LANES), F32)
    for c in range(row_ref.shape[0]):
        log_i, log_f = [], []
        for half in range(2):
            rs = slice(c * MCHUNK + half * LANES, c * MCHUNK + (half + 1) * LANES)
            xt = x_ref[rs, :].T
            log_i.append(xt[0:GATE_CH] + bi_ref[...])
            log_f.append(jax.nn.log_sigmoid(xt[GATE_CH:2 * GATE_CH] + bf_ref[...]))
        t0, t1 = total(log_f[0], jnp.add), total(log_f[1], jnp.add)
        b0 = scan(log_f[0], jnp.add, 0.0) + jnp.where(is_fw, 0.0, t1)
        b1 = scan(log_f[1], jnp.add, 0.0) + jnp.where(is_fw, t0, 0.0)
        r0, r1 = log_i[0] - b0, log_i[1] - b1
        a0, a1 = total(r0, jnp.maximum), total(r1, jnp.maximum)
        e0, e1 = scan(r0, jnp.maximum, -jnp.inf), scan(r1, jnp.maximum, -jnp.inf)
        cm0 = jnp.where(is_fw, e0, jnp.maximum(e0, a1))
        cm1 = jnp.where(is_fw, jnp.maximum(e1, a0), e1)
        g, rmax = t0 + t1, jnp.maximum(a0, a1)
        for off, v0, v1 in ((ROW_B, b0, b1), (ROW_R, r0, r1), (ROW_CM, cm0, cm1),
                            (ROW_G, g, g), (ROW_RMAX, rmax, rmax)):
            row_ref[c, off:off + GATE_CH, 0:LANES] = v0
            row_ref[c, off:off + GATE_CH, LANES:MCHUNK] = v1
        for half, r in ((0, r0), (1, r1)):
            rs = slice(c * MCHUNK + half * LANES, c * MCHUNK + (half + 1) * LANES)
            col_ref[rs, :] = jnp.concatenate([r, zpad], axis=0).T


def _gates(small, b_i, b_f):
    t = small.shape[0]
    nc = t // MCHUNK
    cpb = min(GATE_CHUNKS, nc)
    return pl.pallas_call(
        _gate_kernel,
        grid=(nc // cpb,),
        in_specs=[pl.BlockSpec((cpb * MCHUNK, LANES), lambda i: (i, SMALL_COLS // LANES - 1)),
                  pl.BlockSpec((GATE_CH, 1), lambda i: (0, 0)),
                  pl.BlockSpec((GATE_CH, 1), lambda i: (0, 0))],
        out_specs=[pl.BlockSpec((cpb * MCHUNK, LANES), lambda i: (i, 0)),
                   pl.BlockSpec((cpb, ROW_PACK, MCHUNK), lambda i: (i, 0, 0))],
        out_shape=[jax.ShapeDtypeStruct((t, LANES), F32),
                   jax.ShapeDtypeStruct((nc, ROW_PACK, MCHUNK), F32)],
        compiler_params=_params(("parallel",)),
        name="mlstm_gates",
    )(small, b_i, b_f)


V_EXT = M_V + 16


def _mlstm_kernel(qt_ref, k_ref, vt_ref, row_ref, col_ref, h_ref, c_scr, m_scr, *, reverse):
    @pl.when(pl.program_id(1) == 0)
    def _():
        c_scr[...] = jnp.zeros(c_scr.shape, F32)
        m_scr[...] = jnp.zeros(m_scr.shape, F32)

    d = 1 if reverse else 0
    ll = lax.broadcasted_iota(jnp.int32, (MCHUNK, MCHUNK), 0)
    jj = lax.broadcasted_iota(jnp.int32, (MCHUNK, MCHUNK), 1)
    mask = (ll >= jj) if reverse else (ll <= jj)
    ones_blk = (lax.broadcasted_iota(jnp.int32, (V_EXT - M_V, MCHUNK), 0) == 0).astype(F32)
    inv_scale = float(M_QK) ** 0.5
    col = col_ref[...]
    m_all = m_scr[...]
    m_next = []

    def row(off, ch):
        return row_ref[off + ch:off + ch + 1, :]

    st, cq = [], []
    for h in range(M_HEADS):
        hs = slice(h * M_QK, (h + 1) * M_QK)
        qt = qt_ref[hs, :]
        st.append(jnp.dot(k_ref[:, hs], qt, preferred_element_type=F32))
        cq.append(jnp.dot(c_scr[h].astype(BF16), qt, preferred_element_type=F32))

    for h in range(M_HEADS):
        ch = d * M_HEADS + h
        hs = slice(h * M_QK, (h + 1) * M_QK)
        m_prev = m_all[h:h + 1, :]
        mm = jnp.maximum(m_prev, row(ROW_CM, ch))
        r_col = col[:, ch:ch + 1]
        pt = (jnp.exp(jnp.where(mask, r_col - mm, -jnp.inf)) * st[h]).astype(BF16)
        vt_ext = jnp.concatenate([vt_ref[hs, :], ones_blk.astype(BF16)], axis=0)
        tot = (jnp.dot(vt_ext, pt, preferred_element_type=F32)
               + jnp.exp(m_prev - mm) * cq[h])
        floor = jnp.exp(-row(ROW_B, ch) - mm) * inv_scale
        ht = tot[:M_V] / jnp.maximum(jnp.abs(tot[M_V:M_V + 1]), floor)
        h_ref[:, hs] = ht.T

    for h in range(M_HEADS):
        ch = d * M_HEADS + h
        hs = slice(h * M_QK, (h + 1) * M_QK)
        m_prev = m_all[h:h + 1, :]
        mx = jnp.maximum(m_prev, row(ROW_RMAX, ch))
        w_row = jnp.exp(row(ROW_R, ch) - mx)
        lhs = jnp.concatenate([(vt_ref[hs, :].astype(F32) * w_row).astype(BF16),
                               (ones_blk * w_row).astype(BF16)], axis=0)
        c_scr[h] = (jnp.exp(m_prev - mx)[:, :M_QK] * c_scr[h]
                    + jnp.dot(lhs, k_ref[:, hs], preferred_element_type=F32))
        m_next.append(row(ROW_G, ch) + mx)
    m_scr[...] = jnp.concatenate(m_next, axis=0)


def _mlstm(qvt, big, row, col, b, s, reverse):
    nc = s // MCHUNK
    if reverse:
        blk = lambda bi, i: bi * nc + (nc - 1 - i)
    else:
        blk = lambda bi, i: bi * nc + i
    return pl.pallas_call(
        functools.partial(_mlstm_kernel, reverse=reverse),
        grid=(b, nc),
        in_specs=[pl.BlockSpec((M_WIDTH, MCHUNK), lambda bi, i: (0, blk(bi, i))),
                  pl.BlockSpec((MCHUNK, M_WIDTH), lambda bi, i: (blk(bi, i), 1)),
                  pl.BlockSpec((M_WIDTH, MCHUNK), lambda bi, i: (1, blk(bi, i))),
                  pl.BlockSpec((None, ROW_PACK, MCHUNK), lambda bi, i: (blk(bi, i), 0, 0)),
                  pl.BlockSpec((MCHUNK, LANES), lambda bi, i: (blk(bi, i), 0))],
        out_specs=pl.BlockSpec((MCHUNK, M_WIDTH), lambda bi, i: (blk(bi, i), 0)),
        out_shape=jax.ShapeDtypeStruct((b * s, M_WIDTH), F32),
        scratch_shapes=[pltpu.VMEM((M_HEADS, V_EXT, M_QK), F32),
                        pltpu.VMEM((M_HEADS, MCHUNK), F32)],
        compiler_params=_params(("parallel", "arbitrary")),
        name="mlstm_bwd" if reverse else "mlstm_fwd",
    )(qvt, big, qvt, row, col)


def _merge_kernel(x_ref, ya_ref, hf_ref, hb_ref, om_ref, zm_ref, ga_ref, gb_ref, hn_ref,
                  woa_ref, wob_ref, wout_ref, nf_ref, o_ref, *, final_norm):
    hm = hf_ref[...] + hb_ref[...]
    parts = []
    for h in range(M_HEADS):
        hh = hm[:, h * M_V:(h + 1) * M_V]
        ms = jnp.mean(hh * hh, axis=-1, keepdims=True)
        parts.append(hh * lax.rsqrt(ms + EPS))
    hn = jnp.concatenate(parts, axis=1) * hn_ref[...]
    zm = zm_ref[...].astype(F32)
    yb = hn * jax.nn.sigmoid(om_ref[...].astype(F32)) * (zm * jax.nn.sigmoid(zm))
    pa = jnp.dot(ya_ref[...], woa_ref[...], preferred_element_type=F32)
    pb = jnp.dot(yb.astype(BF16), wob_ref[...], preferred_element_type=F32)
    merged = (jax.nn.sigmoid(ga_ref[...].astype(F32)) * pa
              + jax.nn.sigmoid(gb_ref[...].astype(F32)) * pb)
    out = x_ref[...] + jnp.dot(merged.astype(BF16), wout_ref[...], preferred_element_type=F32)
    if final_norm:
        out = _rms(out, nf_ref[...])
    o_ref[...] = out


def _merge(x2, ya, hf, hb, big, hn, woa, wob, wout, nf, final_norm):
    t = x2.shape[0]
    tm = min(256, t)
    row = lambda c: (lambda i: (i, c))
    const = lambda shape: pl.BlockSpec(shape, lambda i: (0, 0), pipeline_mode=pl.Buffered(1))
    return pl.pallas_call(
        functools.partial(_merge_kernel, final_norm=final_norm),
        grid=(t // tm,),
        in_specs=[pl.BlockSpec((tm, D_MODEL), row(0)),
                  pl.BlockSpec((tm, A_WIDTH), row(0)),
                  pl.BlockSpec((tm, M_WIDTH), row(0)),
                  pl.BlockSpec((tm, M_WIDTH), row(0)),
                  pl.BlockSpec((tm, M_WIDTH), row(2)),
                  pl.BlockSpec((tm, M_WIDTH), row(3)),
                  pl.BlockSpec((tm, D_MODEL), row(2)),
                  pl.BlockSpec((tm, D_MODEL), row(3)),
                  const((1, M_WIDTH)),
                  const((A_WIDTH, D_MODEL)),
                  const((M_WIDTH, D_MODEL)),
                  const((D_MODEL, D_MODEL)),
                  const((1, D_MODEL))],
        out_specs=pl.BlockSpec((tm, D_MODEL), row(0)),
        out_shape=jax.ShapeDtypeStruct((t, D_MODEL), F32),
        compiler_params=_params(("parallel",)),
        name="merge_out",
    )(x2, ya, hf, hb, big, big, big, big, hn, woa, wob, wout, nf)


def _rot_cols(w):
    half = QK_ROPE // 2
    return jnp.concatenate([-w[:, half:], w[:, :half]], axis=1)


def _pack_layer(w_in, w_uq, w_ukv):
    o = 0
    seg = {}
    for name, width in (("c_q", Q_LORA), ("c_kv", KV_LORA), ("k_rope", QK_ROPE), ("z_a", A_WIDTH),
                        ("q_m", M_WIDTH), ("k_m", M_WIDTH), ("v_m", M_WIDTH), ("o_m", M_WIDTH),
                        ("z_m", M_WIDTH), ("gates", 4 * M_HEADS), ("g_a", D_MODEL), ("g_b", D_MODEL)):
        seg[name] = w_in[:, o:o + width]
        o += width
    pad = jnp.zeros((D_MODEL, LANES - 4 * M_HEADS), w_in.dtype)
    w_small = jnp.concatenate([seg["c_q"], seg["c_kv"], seg["k_rope"], _rot_cols(seg["k_rope"]),
                               seg["gates"], pad], axis=1).astype(BF16)
    w_big = jnp.concatenate([seg[n] for n in ("z_a", "k_m", "o_m", "z_m", "g_a", "g_b")],
                            axis=1).astype(BF16)
    w_qvt = jnp.concatenate([seg["q_m"], seg["v_m"]], axis=1).T.astype(BF16)
    wq = w_uq.reshape(Q_LORA, A_HEADS, QK_NOPE + QK_ROPE)
    rope = wq[:, :, QK_NOPE:]
    rot = jnp.concatenate([-rope[:, :, QK_ROPE // 2:], rope[:, :, :QK_ROPE // 2]], axis=2)
    wqt = jnp.concatenate([wq, rot], axis=2).reshape(Q_LORA, A_HEADS * QK_PAD).T.astype(BF16)
    wkv = w_ukv.reshape(KV_LORA, A_HEADS, QK_NOPE + V_HEAD)
    wk = wkv[:, :, :QK_NOPE].reshape(KV_LORA, A_HEADS * QK_NOPE).astype(BF16)
    wvt = wkv[:, :, QK_NOPE:].reshape(KV_LORA, A_HEADS * V_HEAD).T.astype(BF16)
    return w_small, w_big, w_qvt, wqt, wk, wvt


def _rope_table(s):
    inv = ROPE_THETA ** (-jnp.arange(0, QK_ROPE, 2, dtype=F32) / QK_ROPE)
    ang = jnp.arange(s, dtype=F32)[:, None] * inv[None, :]
    cos, sin = jnp.cos(ang), jnp.sin(ang)
    return jnp.concatenate([cos, cos, sin, sin], axis=1)


def _trunk(x, layers, norm_f):
    b, s, _ = x.shape
    x2 = x.reshape(b * s, D_MODEL)
    cs = _rope_table(s)
    cst = cs.T
    q_scale = float((QK_NOPE + QK_ROPE) ** -0.5 * 1.4426950408889634)
    for li, ly in enumerate(layers):
        small = _norm_matmul(x2, ly["norm_in"], ly["w_small"], F32, 512, SMALL_COLS)
        big = _norm_matmul(x2, ly["norm_in"], ly["w_big"], BF16, 1024, 1024)
        qt, kc, vt = _mla_prep(small, cs, cst, ly["q_a_norm"], ly["kv_a_norm"], ly["wqt"], ly["wk"],
                               ly["wvt"], b, s, q_scale)
        ya = _flash(qt, kc, vt, big, b, s)
        qvt = _norm_matmul(x2, ly["norm_in"], ly["w_qvt"], BF16, 1024, 1024, feature_major=True)
        col, row = _gates(small, ly["b_i"], ly["b_f"])
        hf = _mlstm(qvt, big, row, col, b, s, False)
        hb = _mlstm(qvt, big, row, col, b, s, True)
        x2 = _merge(x2, ya, hf, hb, big, ly["m_head_norm"], ly["w_oa"], ly["w_ob"], ly["w_out"],
                    norm_f, li == len(layers) - 1)
    return x2.reshape(b, s, D_MODEL)


def kernel(x_prompt, x_sample, norm_in, w_in, b_gates, q_a_norm, w_uq, kv_a_norm, w_ukv, w_oa,
           m_head_norm, w_ob, w_out, norm_f):
    layers = []
    for l in range(w_in.shape[0]):
        w_small, w_big, w_qvt, wqt, wk, wvt = _pack_layer(w_in[l], w_uq[l], w_ukv[l])
        layers.append(dict(
            norm_in=norm_in[l].reshape(1, D_MODEL), w_small=w_small, w_big=w_big, w_qvt=w_qvt,
            wqt=wqt, wk=wk, wvt=wvt,
            q_a_norm=q_a_norm[l].reshape(1, Q_LORA), kv_a_norm=kv_a_norm[l].reshape(1, KV_LORA),
            b_i=b_gates[l, :GATE_CH].reshape(GATE_CH, 1).astype(F32),
            b_f=b_gates[l, GATE_CH:].reshape(GATE_CH, 1).astype(F32),
            m_head_norm=m_head_norm[l].reshape(1, M_WIDTH),
            w_oa=w_oa[l].astype(BF16), w_ob=w_ob[l].astype(BF16), w_out=w_out[l].astype(BF16)))
    nf = norm_f.reshape(1, D_MODEL)
    return (_trunk(x_prompt, layers, nf), _trunk(x_sample, layers, nf))
```

```python
import functools

import jax
import jax.numpy as jnp
from jax import lax
from jax.experimental import pallas as pl
from jax.experimental.pallas import tpu as pltpu

D_MODEL = 2048
A_HEADS = 8
Q_LORA = 512
KV_LORA = 512
QK_NOPE = 128
QK_ROPE = 64
V_HEAD = 128
ROPE_THETA = 10000.0
A_WIDTH = A_HEADS * V_HEAD
M_HEADS = 8
M_QK = 128
M_V = 128
CHUNK = 64
M_WIDTH = M_HEADS * M_V
EPS = 1e-6

LANES = 128
QK_PAD = 256
KV_CHUNK = 512
FLASH_UNROLL = 16
FLASH_TQ = 512
FLASH_AHEAD = 2
AV_EXT = V_HEAD + 16
VMEM_LIMIT = 56 * 1024 * 1024

SMALL_COLS = Q_LORA + KV_LORA + 2 * LANES
BIG_COLS = 4 * 1024 + 2 * D_MODEL

F32 = jnp.float32
BF16 = jnp.bfloat16


def _rms(xf, g):
    ms = jnp.mean(xf * xf, axis=-1, keepdims=True)
    return xf * lax.rsqrt(ms + EPS) * g


def _params(sem, flags=None):
    return pltpu.CompilerParams(dimension_semantics=sem, vmem_limit_bytes=VMEM_LIMIT, flags=flags)


_NT = (((1,), (1,)), ((), ()))


def _norm_matmul_kernel(x_ref, g_ref, w_ref, o_ref, h_scr, *, rows, feature_major):
    @pl.when(pl.program_id(1) == 0)
    def _():
        def body(r, c):
            sl = pl.ds(pl.multiple_of(r * rows, rows), rows)
            h_scr[sl, :] = _rms(x_ref[sl, :], g_ref[...]).astype(h_scr.dtype)
            return c
        lax.fori_loop(0, x_ref.shape[0] // rows, body, 0)

    if feature_major:
        out = lax.dot_general(w_ref[...], h_scr[...], _NT, preferred_element_type=F32)
    else:
        out = jnp.dot(h_scr[...], w_ref[...], preferred_element_type=F32)
    o_ref[...] = out.astype(o_ref.dtype)


def _norm_matmul(x2, gain, w, out_dtype, tm, tn, feature_major=False):
    t, k = x2.shape
    n = w.shape[0] if feature_major else w.shape[1]
    tm = min(tm, t)
    rows = min(256, tm)
    if feature_major:
        w_spec = pl.BlockSpec((tn, k), lambda i, j: (j, 0))
        o_spec = pl.BlockSpec((tn, tm), lambda i, j: (j, i))
        o_shape = (n, t)
    else:
        w_spec = pl.BlockSpec((k, tn), lambda i, j: (0, j))
        o_spec = pl.BlockSpec((tm, tn), lambda i, j: (i, j))
        o_shape = (t, n)
    return pl.pallas_call(
        functools.partial(_norm_matmul_kernel, rows=rows, feature_major=feature_major),
        grid=(t // tm, n // tn),
        in_specs=[pl.BlockSpec((tm, k), lambda i, j: (i, 0)),
                  pl.BlockSpec((1, k), lambda i, j: (0, 0)),
                  w_spec],
        out_specs=o_spec,
        out_shape=jax.ShapeDtypeStruct(o_shape, out_dtype),
        scratch_shapes=[pltpu.VMEM((tm, k), BF16)],
        compiler_params=_params(("parallel", "arbitrary")),
        name="in_proj_t" if feature_major else "in_proj",
    )(x2, gain, w)


def _mla_prep_kernel(cq_ref, ckv_ref, rest_ref, cs_ref, cst_ref, gq_ref, gkv_ref,
                     wqt_ref, wk_ref, wvt_ref, qt_ref, k_ref, vt_ref, *, q_scale):
    hq = _rms(cq_ref[...], gq_ref[...]).astype(BF16)
    hkv = _rms(ckv_ref[...], gkv_ref[...]).astype(BF16)
    qt = lax.dot_general(wqt_ref[...], hq, _NT, preferred_element_type=F32)
    vt = lax.dot_general(wvt_ref[...], hkv, _NT, preferred_element_type=F32)
    kn = jnp.dot(hkv, wk_ref[...], preferred_element_type=F32)
    t = rest_ref[...] * cs_ref[...]
    lane = lax.broadcasted_iota(jnp.int32, t.shape, 1)
    k_r = jnp.where(lane < QK_ROPE, t + pltpu.roll(t, QK_ROPE, 1), 0.0)
    cst = cst_ref[...]
    pad = jnp.zeros((QK_PAD - QK_NOPE - QK_ROPE, qt.shape[1]), F32)
    ones_blk = (lax.broadcasted_iota(jnp.int32, (AV_EXT - V_HEAD, qt.shape[1]), 0) == 0).astype(BF16)
    for h in range(A_HEADS):
        qh = qt[h * QK_PAD:(h + 1) * QK_PAD]
        tq = qh[QK_NOPE:] * cst
        qt_ref[h] = jnp.concatenate(
            [qh[:QK_NOPE] * q_scale, (tq[:QK_ROPE] + tq[QK_ROPE:]) * q_scale, pad], axis=0).astype(BF16)
        k_ref[h] = jnp.concatenate([kn[:, h * QK_NOPE:(h + 1) * QK_NOPE], k_r], axis=1).astype(BF16)
        vt_ref[h] = jnp.concatenate([vt[h * V_HEAD:(h + 1) * V_HEAD].astype(BF16), ones_blk], axis=0)


def _mla_prep(small, cs, cst, gq, gkv, wqt, wk, wvt, b, s, q_scale):
    tm = min(KV_CHUNK, s // 2)
    nb = s // tm
    full = lambda bi, i: (0, 0)
    return pl.pallas_call(
        functools.partial(_mla_prep_kernel, q_scale=q_scale),
        grid=(b, nb),
        in_specs=[pl.BlockSpec((tm, Q_LORA), lambda bi, i: (bi * nb + i, 0)),
                  pl.BlockSpec((tm, KV_LORA), lambda bi, i: (bi * nb + i, 1)),
                  pl.BlockSpec((tm, LANES), lambda bi, i: (bi * nb + i, (Q_LORA + KV_LORA) // LANES)),
                  pl.BlockSpec((tm, LANES), lambda bi, i: (i, 0)),
                  pl.BlockSpec((LANES, tm), lambda bi, i: (0, i)),
                  pl.BlockSpec((1, Q_LORA), full),
                  pl.BlockSpec((1, KV_LORA), full),
                  pl.BlockSpec((A_HEADS * QK_PAD, Q_LORA), full),
                  pl.BlockSpec((KV_LORA, A_HEADS * QK_NOPE), full),
                  pl.BlockSpec((A_HEADS * V_HEAD, KV_LORA), full)],
        out_specs=[pl.BlockSpec((None, A_HEADS, QK_PAD, tm), lambda bi, i: (bi, 0, 0, i)),
                   pl.BlockSpec((None, A_HEADS, tm, QK_PAD), lambda bi, i: (bi, 0, i, 0)),
                   pl.BlockSpec((None, A_HEADS, None, AV_EXT, tm), lambda bi, i: (bi, 0, i, 0, 0))],
        out_shape=[jax.ShapeDtypeStruct((b, A_HEADS, QK_PAD, s), BF16),
                   jax.ShapeDtypeStruct((b, A_HEADS, s, QK_PAD), BF16),
                   jax.ShapeDtypeStruct((b, A_HEADS, nb, AV_EXT, tm), BF16)],
        compiler_params=_params(("parallel", "parallel")),
        name="mla_prep",
    )(small, small, small, cs, cst, gq, gkv, wqt, wk, wvt)


def _flash_plan(nkb):
    unroll = next(u for u in (FLASH_UNROLL, 8, 2) if nkb % u == 0)
    ahead = FLASH_AHEAD if (nkb % (2 * FLASH_AHEAD) == 0 and unroll % (2 * FLASH_AHEAD) == 0) else 1
    return unroll, ahead


def _flash_kernel(qt_ref, qtn_ref, k_ref, vt_ref, z_ref, o_ref, q2_scr, s_scr, cmax_scr, acc_scr):
    tq = qt_ref.shape[1]
    nkb = vt_ref.shape[0]
    tk = k_ref.shape[0] // nkb
    unroll, ahead = _flash_plan(nkb)
    slots = s_scr.shape[0]
    acc_scr[...] = jnp.zeros(acc_scr.shape, F32)
    q2_scr[0] = qt_ref[...]
    q2_scr[1] = qtn_ref[...]

    def scores(kb, qsel, slot):
        start = kb * tk if isinstance(kb, int) else pl.multiple_of(kb * tk, tk)
        st = jnp.dot(k_ref[pl.ds(start, tk), :], q2_scr[qsel], preferred_element_type=F32)
        s_scr[slot] = st
        return jnp.max(st, axis=0, keepdims=True)

    @pl.when(pl.program_id(2) == 0)
    def _():
        for a in range(ahead):
            cmax_scr[a] = scores(a, 0, a)

    def softmax_values(kb, slot, cmax, m_prev):
        m_new = jnp.maximum(m_prev, cmax)
        pt = jnp.exp2((s_scr[slot] - m_new).astype(BF16))
        acc_scr[...] = (jnp.exp2(m_prev - m_new) * acc_scr[...]
                        + jnp.dot(vt_ref[kb], pt, preferred_element_type=F32))
        return m_new

    def trip(j, carry):
        m, cmax = carry[0], list(carry[1:])
        for u in range(unroll):
            kb = unroll * j + u
            nxt = kb + ahead
            if u >= unroll - ahead:
                wrap = nxt >= nkb
                c_new = scores(jnp.where(wrap, nxt - nkb, nxt), jnp.where(wrap, 1, 0),
                               (u + ahead) % slots)
            else:
                c_new = scores(nxt, 0, (u + ahead) % slots)
            m = softmax_values(kb, u % slots, cmax[0], m)
            cmax = cmax[1:] + [c_new]
        return (m, *cmax)

    init = (jnp.full((1, tq), -jnp.inf, F32), *[cmax_scr[a] for a in range(ahead)])
    out = lax.fori_loop(0, nkb // unroll, trip, init)
    for a in range(ahead):
        cmax_scr[a] = out[1 + a]
    z = z_ref[...].astype(F32)
    acc = acc_scr[...]
    o = (acc[:V_HEAD] / acc[V_HEAD:V_HEAD + 1]).T
    o_ref[...] = (o * (z * jax.nn.sigmoid(z))).astype(o_ref.dtype)


def _flash(qt, kc, vt, big, b, s):
    tq = min(FLASH_TQ, s)
    nq = s // tq
    nkb, tk = vt.shape[2], vt.shape[4]
    _, ahead = _flash_plan(nkb)
    return pl.pallas_call(
        _flash_kernel,
        grid=(b, A_HEADS, nq),
        in_specs=[pl.BlockSpec((None, None, QK_PAD, tq), lambda bi, h, i: (bi, h, 0, i)),
                  pl.BlockSpec((None, None, QK_PAD, tq),
                               lambda bi, h, i: (bi, h, 0, jnp.minimum(i + 1, nq - 1))),
                  pl.BlockSpec((None, None, s, QK_PAD), lambda bi, h, i: (bi, h, 0, 0)),
                  pl.BlockSpec((None, None, nkb, AV_EXT, tk), lambda bi, h, i: (bi, h, 0, 0, 0)),
                  pl.BlockSpec((tq, V_HEAD), lambda bi, h, i: (bi * nq + i, h))],
        out_specs=pl.BlockSpec((tq, V_HEAD), lambda bi, h, i: (bi * nq + i, h)),
        out_shape=jax.ShapeDtypeStruct((b * s, A_WIDTH), BF16),
        scratch_shapes=[pltpu.VMEM((2, QK_PAD, tq), BF16),
                        pltpu.VMEM((2 * ahead, tk, tq), F32),
                        pltpu.VMEM((ahead, 1, tq), F32),
                        pltpu.VMEM((AV_EXT, tq), F32)],
        compiler_params=_params(("parallel", "parallel", "arbitrary")),
        name="mla_flash",
    )(qt, qt, kc, vt, big)


GATE_CH = 2 * M_HEADS
MCHUNK = 2 * LANES
GATE_CHUNKS = 4
ROW_B, ROW_R, ROW_CM, ROW_G, ROW_RMAX = (i * GATE_CH for i in range(5))
ROW_PACK = 5 * GATE_CH


def _gate_kernel(x_ref, bi_ref, bf_ref, col_ref, row_ref):
    shape = (GATE_CH, LANES)
    pos = lax.broadcasted_iota(jnp.int32, shape, 1)
    is_fw = lax.broadcasted_iota(jnp.int32, shape, 0) < M_HEADS
    shifts = (1, 2, 4, 8, 16, 32, 64)

    def scan(y, op, ident):
        for sft in shifts:
            y_f = jnp.where(pos >= sft, pltpu.roll(y, sft, 1), ident)
            y_b = jnp.where(pos < LANES - sft, pltpu.roll(y, LANES - sft, 1), ident)
            y = op(y, jnp.where(is_fw, y_f, y_b))
        return y

    def total(y, op):
        for sft in shifts:
            y = op(y, pltpu.roll(y, sft, 1))
        return y

    zpad = jnp.zeros((LANES - GATE_CH, LANES), F32)
    for c in range(row_ref.shape[0]):
        log_i, log_f = [], []
        for half in range(2):
            rs = slice(c * MCHUNK + half * LANES, c * MCHUNK + (half + 1) * LANES)
            xt = x_ref[rs, :].T
            log_i.append(xt[0:GATE_CH] + bi_ref[...])
            log_f.append(jax.nn.log_sigmoid(xt[GATE_CH:2 * GATE_CH] + bf_ref[...]))
        t0, t1 = total(log_f[0], jnp.add), total(log_f[1], jnp.add)
        b0 = scan(log_f[0], jnp.add, 0.0) + jnp.where(is_fw, 0.0, t1)
        b1 = scan(log_f[1], jnp.add, 0.0) + jnp.where(is_fw, t0, 0.0)
        r0, r1 = log_i[0] - b0, log_i[1] - b1
        a0, a1 = total(r0, jnp.maximum), total(r1, jnp.maximum)
        e0, e1 = scan(r0, jnp.maximum, -jnp.inf), scan(r1, jnp.maximum, -jnp.inf)
        cm0 = jnp.where(is_fw, e0, jnp.maximum(e0, a1))
        cm1 = jnp.where(is_fw, jnp.maximum(e1, a0), e1)
        g, rmax = t0 + t1, jnp.maximum(a0, a1)
        for off, v0, v1 in ((ROW_B, b0, b1), (ROW_R, r0, r1), (ROW_CM, cm0, cm1),
                            (ROW_G, g, g), (ROW_RMAX, rmax, rmax)):
            row_ref[c, off:off + GATE_CH, 0:LANES] = v0
            row_ref[c, off:off + GATE_CH, LANES:MCHUNK] = v1
        for half, r in ((0, r0), (1, r1)):
            rs = slice(c * MCHUNK + half * LANES, c * MCHUNK + (half + 1) * LANES)
            col_ref[rs, :] = jnp.concatenate([r, zpad], axis=0).T


def _gates(small, b_i, b_f):
    t = small.shape[0]
    nc = t // MCHUNK
    cpb = min(GATE_CHUNKS, nc)
    return pl.pallas_call(
        _gate_kernel,
        grid=(nc // cpb,),
        in_specs=[pl.BlockSpec((cpb * MCHUNK, LANES), lambda i: (i, SMALL_COLS // LANES - 1)),
                  pl.BlockSpec((GATE_CH, 1), lambda i: (0, 0)),
                  pl.BlockSpec((GATE_CH, 1), lambda i: (0, 0))],
        out_specs=[pl.BlockSpec((cpb * MCHUNK, LANES), lambda i: (i, 0)),
                   pl.BlockSpec((cpb, ROW_PACK, MCHUNK), lambda i: (i, 0, 0))],
        out_shape=[jax.ShapeDtypeStruct((t, LANES), F32),
                   jax.ShapeDtypeStruct((nc, ROW_PACK, MCHUNK), F32)],
        compiler_params=_params(("parallel",)),
        name="mlstm_gates",
    )(small, b_i, b_f)


V_EXT = M_V + 16


def _mlstm_kernel(qt_ref, k_ref, vt_ref, row_ref, col_ref, h_ref, c_scr, m_scr, *, reverse):
    @pl.when(pl.program_id(1) == 0)
    def _():
        c_scr[...] = jnp.zeros(c_scr.shape, F32)
        m_scr[...] = jnp.zeros(m_scr.shape, F32)

    d = 1 if reverse else 0
    ll = lax.broadcasted_iota(jnp.int32, (MCHUNK, MCHUNK), 0)
    jj = lax.broadcasted_iota(jnp.int32, (MCHUNK, MCHUNK), 1)
    mask = (ll >= jj) if reverse else (ll <= jj)
    ones_blk = (lax.broadcasted_iota(jnp.int32, (V_EXT - M_V, MCHUNK), 0) == 0).astype(F32)
    inv_scale = float(M_QK) ** 0.5
    col = col_ref[...]
    m_all = m_scr[...]
    m_next = []

    def row(off, ch):
        return row_ref[off + ch:off + ch + 1, :]

    st, cq = [], []
    for h in range(M_HEADS):
        hs = slice(h * M_QK, (h + 1) * M_QK)
        qt = qt_ref[hs, :]
        st.append(jnp.dot(k_ref[:, hs], qt, preferred_element_type=F32))
        cq.append(jnp.dot(c_scr[h].astype(BF16), qt, preferred_element_type=F32))

    for h in range(M_HEADS):
        ch = d * M_HEADS + h
        hs = slice(h * M_QK, (h + 1) * M_QK)
        m_prev = m_all[h:h + 1, :]
        mm = jnp.maximum(m_prev, row(ROW_CM, ch))
        r_col = col[:, ch:ch + 1]
        pt = (jnp.exp(jnp.where(mask, r_col - mm, -jnp.inf)) * st[h]).astype(BF16)
        vt_ext = jnp.concatenate([vt_ref[hs, :], ones_blk.astype(BF16)], axis=0)
        tot = (jnp.dot(vt_ext, pt, preferred_element_type=F32)
               + jnp.exp(m_prev - mm) * cq[h])
        floor = jnp.exp(-row(ROW_B, ch) - mm) * inv_scale
        ht = tot[:M_V] / jnp.maximum(jnp.abs(tot[M_V:M_V + 1]), floor)
        h_ref[:, hs] = ht.T

    for h in range(M_HEADS):
        ch = d * M_HEADS + h
        hs = slice(h * M_QK, (h + 1) * M_QK)
        m_prev = m_all[h:h + 1, :]
        mx = jnp.maximum(m_prev, row(ROW_RMAX, ch))
        w_row = jnp.exp(row(ROW_R, ch) - mx)
        lhs = jnp.concatenate([(vt_ref[hs, :].astype(F32) * w_row).astype(BF16),
                               (ones_blk * w_row).astype(BF16)], axis=0)
        c_scr[h] = (jnp.exp(m_prev - mx)[:, :M_QK] * c_scr[h]
                    + jnp.dot(lhs, k_ref[:, hs], preferred_element_type=F32))
        m_next.append(row(ROW_G, ch) + mx)
    m_scr[...] = jnp.concatenate(m_next, axis=0)


def _mlstm(qvt, big, row, col, b, s, reverse):
    nc = s // MCHUNK
    if reverse:
        blk = lambda bi, i: bi * nc + (nc - 1 - i)
    else:
        blk = lambda bi, i: bi * nc + i
    return pl.pallas_call(
        functools.partial(_mlstm_kernel, reverse=reverse),
        grid=(b, nc),
        in_specs=[pl.BlockSpec((M_WIDTH, MCHUNK), lambda bi, i: (0, blk(bi, i))),
                  pl.BlockSpec((MCHUNK, M_WIDTH), lambda bi, i: (blk(bi, i), 1)),
                  pl.BlockSpec((M_WIDTH, MCHUNK), lambda bi, i: (1, blk(bi, i))),
                  pl.BlockSpec((None, ROW_PACK, MCHUNK), lambda bi, i: (blk(bi, i), 0, 0)),
                  pl.BlockSpec((MCHUNK, LANES), lambda bi, i: (blk(bi, i), 0))],
        out_specs=pl.BlockSpec((MCHUNK, M_WIDTH), lambda bi, i: (blk(bi, i), 0)),
        out_shape=jax.ShapeDtypeStruct((b * s, M_WIDTH), F32),
        scratch_shapes=[pltpu.VMEM((M_HEADS, V_EXT, M_QK), F32),
                        pltpu.VMEM((M_HEADS, MCHUNK), F32)],
        compiler_params=_params(("parallel", "arbitrary")),
        name="mlstm_bwd" if reverse else "mlstm_fwd",
    )(qvt, big, qvt, row, col)


def _merge_kernel(x_ref, ya_ref, hf_ref, hb_ref, om_ref, zm_ref, ga_ref, gb_ref, hn_ref,
                  woa_ref, wob_ref, wout_ref, nf_ref, o_ref, *, final_norm):
    hm = hf_ref[...] + hb_ref[...]
    parts = []
    for h in range(M_HEADS):
        hh = hm[:, h * M_V:(h + 1) * M_V]
        ms = jnp.mean(hh * hh, axis=-1, keepdims=True)
        parts.append(hh * lax.rsqrt(ms + EPS))
    hn = jnp.concatenate(parts, axis=1) * hn_ref[...]
    zm = zm_ref[...].astype(F32)
    yb = hn * jax.nn.sigmoid(om_ref[...].astype(F32)) * (zm * jax.nn.sigmoid(zm))
    pa = jnp.dot(ya_ref[...], woa_ref[...], preferred_element_type=F32)
    pb = jnp.dot(yb.astype(BF16), wob_ref[...], preferred_element_type=F32)
    merged = (jax.nn.sigmoid(ga_ref[...].astype(F32)) * pa
              + jax.nn.sigmoid(gb_ref[...].astype(F32)) * pb)
    out = x_ref[...] + jnp.dot(merged.astype(BF16), wout_ref[...], preferred_element_type=F32)
    if final_norm:
        out = _rms(out, nf_ref[...])
    o_ref[...] = out


def _merge(x2, ya, hf, hb, big, hn, woa, wob, wout, nf, final_norm):
    t = x2.shape[0]
    tm = min(256, t)
    row = lambda c: (lambda i: (i, c))
    const = lambda shape: pl.BlockSpec(shape, lambda i: (0, 0), pipeline_mode=pl.Buffered(1))
    return pl.pallas_call(
        functools.partial(_merge_kernel, final_norm=final_norm),
        grid=(t // tm,),
        in_specs=[pl.BlockSpec((tm, D_MODEL), row(0)),
                  pl.BlockSpec((tm, A_WIDTH), row(0)),
                  pl.BlockSpec((tm, M_WIDTH), row(0)),
                  pl.BlockSpec((tm, M_WIDTH), row(0)),
                  pl.BlockSpec((tm, M_WIDTH), row(2)),
                  pl.BlockSpec((tm, M_WIDTH), row(3)),
                  pl.BlockSpec((tm, D_MODEL), row(2)),
                  pl.BlockSpec((tm, D_MODEL), row(3)),
                  const((1, M_WIDTH)),
                  const((A_WIDTH, D_MODEL)),
                  const((M_WIDTH, D_MODEL)),
                  const((D_MODEL, D_MODEL)),
                  const((1, D_MODEL))],
        out_specs=pl.BlockSpec((tm, D_MODEL), row(0)),
        out_shape=jax.ShapeDtypeStruct((t, D_MODEL), F32),
        compiler_params=_params(("parallel",)),
        name="merge_out",
    )(x2, ya, hf, hb, big, big, big, big, hn, woa, wob, wout, nf)


def _rot_cols(w):
    half = QK_ROPE // 2
    return jnp.concatenate([-w[:, half:], w[:, :half]], axis=1)


def _pack_layer(w_in, w_uq, w_ukv):
    o = 0
    seg = {}
    for name, width in (("c_q", Q_LORA), ("c_kv", KV_LORA), ("k_rope", QK_ROPE), ("z_a", A_WIDTH),
                        ("q_m", M_WIDTH), ("k_m", M_WIDTH), ("v_m", M_WIDTH), ("o_m", M_WIDTH),
                        ("z_m", M_WIDTH), ("gates", 4 * M_HEADS), ("g_a", D_MODEL), ("g_b", D_MODEL)):
        seg[name] = w_in[:, o:o + width]
        o += width
    pad = jnp.zeros((D_MODEL, LANES - 4 * M_HEADS), w_in.dtype)
    w_small = jnp.concatenate([seg["c_q"], seg["c_kv"], seg["k_rope"], _rot_cols(seg["k_rope"]),
                               seg["gates"], pad], axis=1).astype(BF16)
    w_big = jnp.concatenate([seg[n] for n in ("z_a", "k_m", "o_m", "z_m", "g_a", "g_b")],
                            axis=1).astype(BF16)
    w_qvt = jnp.concatenate([seg["q_m"], seg["v_m"]], axis=1).T.astype(BF16)
    wq = w_uq.reshape(Q_LORA, A_HEADS, QK_NOPE + QK_ROPE)
    rope = wq[:, :, QK_NOPE:]
    rot = jnp.concatenate([-rope[:, :, QK_ROPE // 2:], rope[:, :, :QK_ROPE // 2]], axis=2)
    wqt = jnp.concatenate([wq, rot], axis=2).reshape(Q_LORA, A_HEADS * QK_PAD).T.astype(BF16)
    wkv = w_ukv.reshape(KV_LORA, A_HEADS, QK_NOPE + V_HEAD)
    wk = wkv[:, :, :QK_NOPE].reshape(KV_LORA, A_HEADS * QK_NOPE).astype(BF16)
    wvt = wkv[:, :, QK_NOPE:].reshape(KV_LORA, A_HEADS * V_HEAD).T.astype(BF16)
    return w_small, w_big, w_qvt, wqt, wk, wvt


def _rope_table(s):
    inv = ROPE_THETA ** (-jnp.arange(0, QK_ROPE, 2, dtype=F32) / QK_ROPE)
    ang = jnp.arange(s, dtype=F32)[:, None] * inv[None, :]
    cos, sin = jnp.cos(ang), jnp.sin(ang)
    return jnp.concatenate([cos, cos, sin, sin], axis=1)


def _trunk(x, layers, norm_f):
    b, s, _ = x.shape
    x2 = x.reshape(b * s, D_MODEL)
    cs = _rope_table(s)
    cst = cs.T
    q_scale = float((QK_NOPE + QK_ROPE) ** -0.5 * 1.4426950408889634)
    for li, ly in enumerate(layers):
        small = _norm_matmul(x2, ly["norm_in"], ly["w_small"], F32, 512, SMALL_COLS)
        big = _norm_matmul(x2, ly["norm_in"], ly["w_big"], BF16, 1024, 1024)
        qt, kc, vt = _mla_prep(small, cs, cst, ly["q_a_norm"], ly["kv_a_norm"], ly["wqt"], ly["wk"],
                               ly["wvt"], b, s, q_scale)
        ya = _flash(qt, kc, vt, big, b, s)
        qvt = _norm_matmul(x2, ly["norm_in"], ly["w_qvt"], BF16, 1024, 1024, feature_major=True)
        col, row = _gates(small, ly["b_i"], ly["b_f"])
        hf = _mlstm(qvt, big, row, col, b, s, False)
        hb = _mlstm(qvt, big, row, col, b, s, True)
        x2 = _merge(x2, ya, hf, hb, big, ly["m_head_norm"], ly["w_oa"], ly["w_ob"], ly["w_out"],
                    norm_f, li == len(layers) - 1)
    return x2.reshape(b, s, D_MODEL)


def kernel(x_prompt, x_sample, norm_in, w_in, b_gates, q_a_norm, w_uq, kv_a_norm, w_ukv, w_oa,
           m_head_norm, w_ob, w_out, norm_f):
    layers = []
    for l in range(w_in.shape[0]):
        w_small, w_big, w_qvt, wqt, wk, wvt = _pack_layer(w_in[l], w_uq[l], w_ukv[l])
        layers.append(dict(
            norm_in=norm_in[l].reshape(1, D_MODEL), w_small=w_small, w_big=w_big, w_qvt=w_qvt,
            wqt=wqt, wk=wk, wvt=wvt,
            q_a_norm=q_a_norm[l].reshape(1, Q_LORA), kv_a_norm=kv_a_norm[l].reshape(1, KV_LORA),
            b_i=b_gates[l, :GATE_CH].reshape(GATE_CH, 1).astype(F32),
            b_f=b_gates[l, GATE_CH:].reshape(GATE_CH, 1).astype(F32),
            m_head_norm=m_head_norm[l].reshape(1, M_WIDTH),
            w_oa=w_oa[l].astype(BF16), w_ob=w_ob[l].astype(BF16), w_out=w_out[l].astype(BF16)))
    nf = norm_f.reshape(1, D_MODEL)
    return (_trunk(x_prompt, layers, nf), _trunk(x_sample, layers, nf))
```

```python
import functools

import jax
import jax.numpy as jnp
from jax import lax
from jax.experimental import pallas as pl
from jax.experimental.pallas import tpu as pltpu

D_MODEL = 2048
A_HEADS = 8
Q_LORA = 512
KV_LORA = 512
QK_NOPE = 128
QK_ROPE = 64
V_HEAD = 128
ROPE_THETA = 10000.0
A_WIDTH = A_HEADS * V_HEAD
M_HEADS = 8
M_QK = 128
M_V = 128
CHUNK = 64
M_WIDTH = M_HEADS * M_V
EPS = 1e-6

LANES = 128
QK_PAD = 256
KV_CHUNK = 512
FLASH_UNROLL = 16
FLASH_TQ = 512
FLASH_AHEAD = 2
AV_EXT = V_HEAD + 16
VMEM_LIMIT = 56 * 1024 * 1024

SMALL_COLS = Q_LORA + KV_LORA + 2 * LANES
BIG_COLS = 4 * 1024 + 2 * D_MODEL

F32 = jnp.float32
BF16 = jnp.bfloat16


def _rms(xf, g):
    ms = jnp.mean(xf * xf, axis=-1, keepdims=True)
    return xf * lax.rsqrt(ms + EPS) * g


def _params(sem, flags=None):
    return pltpu.CompilerParams(dimension_semantics=sem, vmem_limit_bytes=VMEM_LIMIT, flags=flags)


_NT = (((1,), (1,)), ((), ()))


def _norm_matmul_kernel(x_ref, g_ref, w_ref, o_ref, h_scr, *, rows, feature_major):
    @pl.when(pl.program_id(1) == 0)
    def _():
        def body(r, c):
            sl = pl.ds(pl.multiple_of(r * rows, rows), rows)
            h_scr[sl, :] = _rms(x_ref[sl, :], g_ref[...]).astype(h_scr.dtype)
            return c
        lax.fori_loop(0, x_ref.shape[0] // rows, body, 0)

    if feature_major:
        out = lax.dot_general(w_ref[...], h_scr[...], _NT, preferred_element_type=F32)
        for c in range(o_ref.shape[0]):
            o_ref[c] = out[:, c * o_ref.shape[2]:(c + 1) * o_ref.shape[2]].astype(o_ref.dtype)
    else:
        out = jnp.dot(h_scr[...], w_ref[...], preferred_element_type=F32)
        o_ref[...] = out.astype(o_ref.dtype)


def _norm_matmul(x2, gain, w, out_dtype, tm, tn, feature_major=False):
    t, k = x2.shape
    n = w.shape[0] if feature_major else w.shape[1]
    tm = min(tm, t)
    rows = min(256, tm)
    if feature_major:
        w_spec = pl.BlockSpec((tn, k), lambda i, j: (j, 0))
        o_spec = pl.BlockSpec((tm // MCHUNK, tn, MCHUNK), lambda i, j: (i, j, 0))
        o_shape = (t // MCHUNK, n, MCHUNK)
    else:
        w_spec = pl.BlockSpec((k, tn), lambda i, j: (0, j))
        o_spec = pl.BlockSpec((tm, tn), lambda i, j: (i, j))
        o_shape = (t, n)
    return pl.pallas_call(
        functools.partial(_norm_matmul_kernel, rows=rows, feature_major=feature_major),
        grid=(t // tm, n // tn),
        in_specs=[pl.BlockSpec((tm, k), lambda i, j: (i, 0)),
                  pl.BlockSpec((1, k), lambda i, j: (0, 0)),
                  w_spec],
        out_specs=o_spec,
        out_shape=jax.ShapeDtypeStruct(o_shape, out_dtype),
        scratch_shapes=[pltpu.VMEM((tm, k), BF16)],
        compiler_params=_params(("parallel", "arbitrary")),
        name="in_proj_t" if feature_major else "in_proj",
    )(x2, gain, w)


def _mla_prep_kernel(cq_ref, ckv_ref, rest_ref, cs_ref, cst_ref, gq_ref, gkv_ref,
                     wqt_ref, wk_ref, wvt_ref, qt_ref, k_ref, vt_ref, *, q_scale):
    hq = _rms(cq_ref[...], gq_ref[...]).astype(BF16)
    hkv = _rms(ckv_ref[...], gkv_ref[...]).astype(BF16)
    qt = lax.dot_general(wqt_ref[...], hq, _NT, preferred_element_type=F32)
    vt = lax.dot_general(wvt_ref[...], hkv, _NT, preferred_element_type=F32)
    kn = jnp.dot(hkv, wk_ref[...], preferred_element_type=F32)
    t = rest_ref[...] * cs_ref[...]
    lane = lax.broadcasted_iota(jnp.int32, t.shape, 1)
    k_r = jnp.where(lane < QK_ROPE, t + pltpu.roll(t, QK_ROPE, 1), 0.0)
    cst = cst_ref[...]
    pad = jnp.zeros((QK_PAD - QK_NOPE - QK_ROPE, qt.shape[1]), F32)
    ones_blk = (lax.broadcasted_iota(jnp.int32, (AV_EXT - V_HEAD, qt.shape[1]), 0) == 0).astype(BF16)
    for h in range(A_HEADS):
        qh = qt[h * QK_PAD:(h + 1) * QK_PAD]
        tq = qh[QK_NOPE:] * cst
        qt_ref[h] = jnp.concatenate(
            [qh[:QK_NOPE] * q_scale, (tq[:QK_ROPE] + tq[QK_ROPE:]) * q_scale, pad], axis=0).astype(BF16)
        k_ref[h] = jnp.concatenate([kn[:, h * QK_NOPE:(h + 1) * QK_NOPE], k_r], axis=1).astype(BF16)
        vt_ref[h] = jnp.concatenate([vt[h * V_HEAD:(h + 1) * V_HEAD].astype(BF16), ones_blk], axis=0)


def _mla_prep(small, cs, cst, gq, gkv, wqt, wk, wvt, b, s, q_scale):
    tm = min(KV_CHUNK, s // 2)
    nb = s // tm
    full = lambda bi, i: (0, 0)
    return pl.pallas_call(
        functools.partial(_mla_prep_kernel, q_scale=q_scale),
        grid=(b, nb),
        in_specs=[pl.BlockSpec((tm, Q_LORA), lambda bi, i: (bi * nb + i, 0)),
                  pl.BlockSpec((tm, KV_LORA), lambda bi, i: (bi * nb + i, 1)),
                  pl.BlockSpec((tm, LANES), lambda bi, i: (bi * nb + i, (Q_LORA + KV_LORA) // LANES)),
                  pl.BlockSpec((tm, LANES), lambda bi, i: (i, 0)),
                  pl.BlockSpec((LANES, tm), lambda bi, i: (0, i)),
                  pl.BlockSpec((1, Q_LORA), full),
                  pl.BlockSpec((1, KV_LORA), full),
                  pl.BlockSpec((A_HEADS * QK_PAD, Q_LORA), full),
                  pl.BlockSpec((KV_LORA, A_HEADS * QK_NOPE), full),
                  pl.BlockSpec((A_HEADS * V_HEAD, KV_LORA), full)],
        out_specs=[pl.BlockSpec((None, A_HEADS, QK_PAD, tm), lambda bi, i: (bi, 0, 0, i)),
                   pl.BlockSpec((None, A_HEADS, tm, QK_PAD), lambda bi, i: (bi, 0, i, 0)),
                   pl.BlockSpec((None, A_HEADS, None, AV_EXT, tm), lambda bi, i: (bi, 0, i, 0, 0))],
        out_shape=[jax.ShapeDtypeStruct((b, A_HEADS, QK_PAD, s), BF16),
                   jax.ShapeDtypeStruct((b, A_HEADS, s, QK_PAD), BF16),
                   jax.ShapeDtypeStruct((b, A_HEADS, nb, AV_EXT, tm), BF16)],
        compiler_params=_params(("parallel", "parallel")),
        name="mla_prep",
    )(small, small, small, cs, cst, gq, gkv, wqt, wk, wvt)


def _flash_plan(nkb):
    unroll = next(u for u in (FLASH_UNROLL, 8, 2) if nkb % u == 0)
    ahead = FLASH_AHEAD if (nkb % (2 * FLASH_AHEAD) == 0 and unroll % (2 * FLASH_AHEAD) == 0) else 1
    return unroll, ahead


def _flash_kernel(qt_ref, qtn_ref, k_ref, vt_ref, z_ref, o_ref, q2_scr, s_scr, cmax_scr, acc_scr):
    tq = qt_ref.shape[1]
    nkb = vt_ref.shape[0]
    tk = k_ref.shape[0] // nkb
    unroll, ahead = _flash_plan(nkb)
    slots = s_scr.shape[0]
    acc_scr[...] = jnp.zeros(acc_scr.shape, F32)
    q2_scr[0] = qt_ref[...]
    q2_scr[1] = qtn_ref[...]

    def scores(kb, qsel, slot):
        start = kb * tk if isinstance(kb, int) else pl.multiple_of(kb * tk, tk)
        st = jnp.dot(k_ref[pl.ds(start, tk), :], q2_scr[qsel], preferred_element_type=F32)
        s_scr[slot] = st
        return jnp.max(st, axis=0, keepdims=True)

    @pl.when(pl.program_id(2) == 0)
    def _():
        for a in range(ahead):
            cmax_scr[a] = scores(a, 0, a)

    def softmax_values(kb, slot, cmax, m_prev):
        m_new = jnp.maximum(m_prev, cmax)
        pt = jnp.exp2((s_scr[slot] - m_new).astype(BF16))
        acc_scr[...] = (jnp.exp2(m_prev - m_new) * acc_scr[...]
                        + jnp.dot(vt_ref[kb], pt, preferred_element_type=F32))
        return m_new

    def trip(j, carry):
        m, cmax = carry[0], list(carry[1:])
        for u in range(unroll):
            kb = unroll * j + u
            nxt = kb + ahead
            if u >= unroll - ahead:
                wrap = nxt >= nkb
                c_new = scores(jnp.where(wrap, nxt - nkb, nxt), jnp.where(wrap, 1, 0),
                               (u + ahead) % slots)
            else:
                c_new = scores(nxt, 0, (u + ahead) % slots)
            m = softmax_values(kb, u % slots, cmax[0], m)
            cmax = cmax[1:] + [c_new]
        return (m, *cmax)

    init = (jnp.full((1, tq), -jnp.inf, F32), *[cmax_scr[a] for a in range(ahead)])
    out = lax.fori_loop(0, nkb // unroll, trip, init)
    for a in range(ahead):
        cmax_scr[a] = out[1 + a]
    z = z_ref[...].astype(F32)
    acc = acc_scr[...]
    o = (acc[:V_HEAD] / acc[V_HEAD:V_HEAD + 1]).T
    o_ref[...] = (o * (z * jax.nn.sigmoid(z))).astype(o_ref.dtype)


def _flash(qt, kc, vt, big, b, s):
    tq = min(FLASH_TQ, s)
    nq = s // tq
    nkb, tk = vt.shape[2], vt.shape[4]
    _, ahead = _flash_plan(nkb)
    return pl.pallas_call(
        _flash_kernel,
        grid=(b, A_HEADS, nq),
        in_specs=[pl.BlockSpec((None, None, QK_PAD, tq), lambda bi, h, i: (bi, h, 0, i)),
                  pl.BlockSpec((None, None, QK_PAD, tq),
                               lambda bi, h, i: (bi, h, 0, jnp.minimum(i + 1, nq - 1))),
                  pl.BlockSpec((None, None, s, QK_PAD), lambda bi, h, i: (bi, h, 0, 0)),
                  pl.BlockSpec((None, None, nkb, AV_EXT, tk), lambda bi, h, i: (bi, h, 0, 0, 0)),
                  pl.BlockSpec((tq, V_HEAD), lambda bi, h, i: (bi * nq + i, h))],
        out_specs=pl.BlockSpec((tq, V_HEAD), lambda bi, h, i: (bi * nq + i, h)),
        out_shape=jax.ShapeDtypeStruct((b * s, A_WIDTH), BF16),
        scratch_shapes=[pltpu.VMEM((2, QK_PAD, tq), BF16),
                        pltpu.VMEM((2 * ahead, tk, tq), F32),
                        pltpu.VMEM((ahead, 1, tq), F32),
                        pltpu.VMEM((AV_EXT, tq), F32)],
        compiler_params=_params(("parallel", "parallel", "arbitrary")),
        name="mla_flash",
    )(qt, qt, kc, vt, big)


GATE_CH = 2 * M_HEADS
MCHUNK = 2 * LANES
GATE_CHUNKS = 4
ROW_B, ROW_R, ROW_CM, ROW_G, ROW_RMAX = (i * GATE_CH for i in range(5))
ROW_PACK = 5 * GATE_CH


def _gate_kernel(x_ref, bi_ref, bf_ref, col_ref, row_ref):
    shape = (GATE_CH, LANES)
    pos = lax.broadcasted_iota(jnp.int32, shape, 1)
    is_fw = lax.broadcasted_iota(jnp.int32, shape, 0) < M_HEADS
    shifts = (1, 2, 4, 8, 16, 32, 64)

    def scan(y, op, ident):
        for sft in shifts:
            y_f = jnp.where(pos >= sft, pltpu.roll(y, sft, 1), ident)
            y_b = jnp.where(pos < LANES - sft, pltpu.roll(y, LANES - sft, 1), ident)
            y = op(y, jnp.where(is_fw, y_f, y_b))
        return y

    def total(y, op):
        for sft in shifts:
            y = op(y, pltpu.roll(y, sft, 1))
        return y

    zpad = jnp.zeros((LANES - GATE_CH, LANES), F32)
    for c in range(row_ref.shape[0]):
        log_i, log_f = [], []
        for half in range(2):
            rs = slice(c * MCHUNK + half * LANES, c * MCHUNK + (half + 1) * LANES)
            xt = x_ref[rs, :].T
            log_i.append(xt[0:GATE_CH] + bi_ref[...])
            log_f.append(jax.nn.log_sigmoid(xt[GATE_CH:2 * GATE_CH] + bf_ref[...]))
        t0, t1 = total(log_f[0], jnp.add), total(log_f[1], jnp.add)
        b0 = scan(log_f[0], jnp.add, 0.0) + jnp.where(is_fw, 0.0, t1)
        b1 = scan(log_f[1], jnp.add, 0.0) + jnp.where(is_fw, t0, 0.0)
        r0, r1 = log_i[0] - b0, log_i[1] - b1
        a0, a1 = total(r0, jnp.maximum), total(r1, jnp.maximum)
        e0, e1 = scan(r0, jnp.maximum, -jnp.inf), scan(r1, jnp.maximum, -jnp.inf)
        cm0 = jnp.where(is_fw, e0, jnp.maximum(e0, a1))
        cm1 = jnp.where(is_fw, jnp.maximum(e1, a0), e1)
        g, rmax = t0 + t1, jnp.maximum(a0, a1)
        for off, v0, v1 in ((ROW_B, b0, b1), (ROW_R, r0, r1), (ROW_CM, cm0, cm1),
                            (ROW_G, g, g), (ROW_RMAX, rmax, rmax)):
            row_ref[c, off:off + GATE_CH, 0:LANES] = v0
            row_ref[c, off:off + GATE_CH, LANES:MCHUNK] = v1
        for half, r in ((0, r0), (1, r1)):
            rs = slice(c * MCHUNK + half * LANES, c * MCHUNK + (half + 1) * LANES)
            col_ref[rs, :] = jnp.concatenate([r, zpad], axis=0).T


def _gates(small, b_i, b_f):
    t = small.shape[0]
    nc = t // MCHUNK
    cpb = min(GATE_CHUNKS, nc)
    return pl.pallas_call(
        _gate_kernel,
        grid=(nc // cpb,),
        in_specs=[pl.BlockSpec((cpb * MCHUNK, LANES), lambda i: (i, SMALL_COLS // LANES - 1)),
                  pl.BlockSpec((GATE_CH, 1), lambda i: (0, 0)),
                  pl.BlockSpec((GATE_CH, 1), lambda i: (0, 0))],
        out_specs=[pl.BlockSpec((cpb * MCHUNK, LANES), lambda i: (i, 0)),
                   pl.BlockSpec((cpb, ROW_PACK, MCHUNK), lambda i: (i, 0, 0))],
        out_shape=[jax.ShapeDtypeStruct((t, LANES), F32),
                   jax.ShapeDtypeStruct((nc, ROW_PACK, MCHUNK), F32)],
        compiler_params=_params(("parallel",)),
        name="mlstm_gates",
    )(small, b_i, b_f)


V_EXT = M_V + 16


def _mlstm_kernel(qtf_ref, kf_ref, vtf_ref, rowf_ref, colf_ref,
                  qtb_ref, kb_ref, vtb_ref, rowb_ref, colb_ref,
                  hf_ref, hb_ref, c_scr, m_scr):
    @pl.when(pl.program_id(1) == 0)
    def _():
        c_scr[...] = jnp.zeros(c_scr.shape, F32)
        m_scr[...] = jnp.zeros(m_scr.shape, F32)

    ll = lax.broadcasted_iota(jnp.int32, (MCHUNK, MCHUNK), 0)
    jj = lax.broadcasted_iota(jnp.int32, (MCHUNK, MCHUNK), 1)
    ones_blk = (lax.broadcasted_iota(jnp.int32, (V_EXT - M_V, MCHUNK), 0) == 0).astype(F32)
    inv_scale = float(M_QK) ** 0.5
    m_all = m_scr[...]
    m_next = []
    dirs = ((qtf_ref, kf_ref, vtf_ref, rowf_ref, colf_ref[...], hf_ref, ll <= jj),
            (qtb_ref, kb_ref, vtb_ref, rowb_ref, colb_ref[...], hb_ref, ll >= jj))
    units = [(d, h) for d in range(2) for h in range(M_HEADS)]

    def row(d, off, h):
        r = off + d * M_HEADS + h
        return dirs[d][3][r:r + 1, :]

    st, cq = {}, {}
    for d, h in units:
        qt_ref, k_ref = dirs[d][0], dirs[d][1]
        hs = slice(h * M_QK, (h + 1) * M_QK)
        qt = qt_ref[hs, :]
        st[d, h] = jnp.dot(k_ref[:, hs], qt, preferred_element_type=F32)
        cq[d, h] = jnp.dot(c_scr[d, h].astype(BF16), qt, preferred_element_type=F32)

    for d, h in units:
        _, _, vt_ref, _, col, h_ref, mask = dirs[d]
        ch = d * M_HEADS + h
        hs = slice(h * M_QK, (h + 1) * M_QK)
        m_prev = m_all[ch:ch + 1, :]
        mm = jnp.maximum(m_prev, row(d, ROW_CM, h))
        r_col = col[:, ch:ch + 1]
        pt = (jnp.exp(jnp.where(mask, r_col - mm, -jnp.inf)) * st[d, h]).astype(BF16)
        vt_ext = jnp.concatenate([vt_ref[hs, :], ones_blk.astype(BF16)], axis=0)
        tot = (jnp.dot(vt_ext, pt, preferred_element_type=F32)
               + jnp.exp(m_prev - mm) * cq[d, h])
        floor = jnp.exp(-row(d, ROW_B, h) - mm) * inv_scale
        ht = tot[:M_V] / jnp.maximum(jnp.abs(tot[M_V:M_V + 1]), floor)
        h_ref[:, hs] = ht.T

    for d, h in units:
        k_ref, vt_ref = dirs[d][1], dirs[d][2]
        ch = d * M_HEADS + h
        hs = slice(h * M_QK, (h + 1) * M_QK)
        m_prev = m_all[ch:ch + 1, :]
        mx = jnp.maximum(m_prev, row(d, ROW_RMAX, h))
        w_row = jnp.exp(row(d, ROW_R, h) - mx)
        lhs = jnp.concatenate([(vt_ref[hs, :].astype(F32) * w_row).astype(BF16),
                               (ones_blk * w_row).astype(BF16)], axis=0)
        c_scr[d, h] = (jnp.exp(m_prev - mx)[:, :M_QK] * c_scr[d, h]
                       + jnp.dot(lhs, k_ref[:, hs], preferred_element_type=F32))
        m_next.append(row(d, ROW_G, h) + mx)
    m_scr[...] = jnp.concatenate(m_next, axis=0)


def _mlstm(qvt, big, row, col, b, s):
    nc = s // MCHUNK
    fw = lambda bi, i: bi * nc + i
    bw = lambda bi, i: bi * nc + (nc - 1 - i)

    def specs(blk):
        return [pl.BlockSpec((None, M_WIDTH, MCHUNK), lambda bi, i: (blk(bi, i), 0, 0)),
                pl.BlockSpec((MCHUNK, M_WIDTH), lambda bi, i: (blk(bi, i), 1)),
                pl.BlockSpec((None, M_WIDTH, MCHUNK), lambda bi, i: (blk(bi, i), 1, 0)),
                pl.BlockSpec((None, ROW_PACK, MCHUNK), lambda bi, i: (blk(bi, i), 0, 0)),
                pl.BlockSpec((MCHUNK, LANES), lambda bi, i: (blk(bi, i), 0))]

    out = jax.ShapeDtypeStruct((b * s, M_WIDTH), F32)
    return pl.pallas_call(
        _mlstm_kernel,
        grid=(b, nc),
        in_specs=specs(fw) + specs(bw),
        out_specs=[pl.BlockSpec((MCHUNK, M_WIDTH), lambda bi, i: (fw(bi, i), 0)),
                   pl.BlockSpec((MCHUNK, M_WIDTH), lambda bi, i: (bw(bi, i), 0))],
        out_shape=[out, out],
        scratch_shapes=[pltpu.VMEM((2, M_HEADS, V_EXT, M_QK), F32),
                        pltpu.VMEM((GATE_CH, MCHUNK), F32)],
        compiler_params=_params(("parallel", "arbitrary")),
        name="mlstm",
    )(qvt, big, qvt, row, col, qvt, big, qvt, row, col)


def _merge_kernel(x_ref, ya_ref, hf_ref, hb_ref, om_ref, zm_ref, ga_ref, gb_ref, hn_ref,
                  woa_ref, wob_ref, wout_ref, nf_ref, o_ref, *, final_norm):
    hm = hf_ref[...] + hb_ref[...]
    parts = []
    for h in range(M_HEADS):
        hh = hm[:, h * M_V:(h + 1) * M_V]
        ms = jnp.mean(hh * hh, axis=-1, keepdims=True)
        parts.append(hh * lax.rsqrt(ms + EPS))
    hn = jnp.concatenate(parts, axis=1) * hn_ref[...]
    zm = zm_ref[...].astype(F32)
    yb = hn * jax.nn.sigmoid(om_ref[...].astype(F32)) * (zm * jax.nn.sigmoid(zm))
    pa = jnp.dot(ya_ref[...], woa_ref[...], preferred_element_type=F32)
    pb = jnp.dot(yb.astype(BF16), wob_ref[...], preferred_element_type=F32)
    merged = (jax.nn.sigmoid(ga_ref[...].astype(F32)) * pa
              + jax.nn.sigmoid(gb_ref[...].astype(F32)) * pb)
    out = x_ref[...] + jnp.dot(merged.astype(BF16), wout_ref[...], preferred_element_type=F32)
    if final_norm:
        out = _rms(out, nf_ref[...])
    o_ref[...] = out


def _merge(x2, ya, hf, hb, big, hn, woa, wob, wout, nf, final_norm):
    t = x2.shape[0]
    tm = min(256, t)
    row = lambda c: (lambda i: (i, c))
    const = lambda shape: pl.BlockSpec(shape, lambda i: (0, 0), pipeline_mode=pl.Buffered(1))
    return pl.pallas_call(
        functools.partial(_merge_kernel, final_norm=final_norm),
        grid=(t // tm,),
        in_specs=[pl.BlockSpec((tm, D_MODEL), row(0)),
                  pl.BlockSpec((tm, A_WIDTH), row(0)),
                  pl.BlockSpec((tm, M_WIDTH), row(0)),
                  pl.BlockSpec((tm, M_WIDTH), row(0)),
                  pl.BlockSpec((tm, M_WIDTH), row(2)),
                  pl.BlockSpec((tm, M_WIDTH), row(3)),
                  pl.BlockSpec((tm, D_MODEL), row(2)),
                  pl.BlockSpec((tm, D_MODEL), row(3)),
                  const((1, M_WIDTH)),
                  const((A_WIDTH, D_MODEL)),
                  const((M_WIDTH, D_MODEL)),
                  const((D_MODEL, D_MODEL)),
                  const((1, D_MODEL))],
        out_specs=pl.BlockSpec((tm, D_MODEL), row(0)),
        out_shape=jax.ShapeDtypeStruct((t, D_MODEL), F32),
        compiler_params=_params(("parallel",)),
        name="merge_out",
    )(x2, ya, hf, hb, big, big, big, big, hn, woa, wob, wout, nf)


def _rot_cols(w):
    half = QK_ROPE // 2
    return jnp.concatenate([-w[:, half:], w[:, :half]], axis=1)


def _pack_layer(w_in, w_uq, w_ukv):
    o = 0
    seg = {}
    for name, width in (("c_q", Q_LORA), ("c_kv", KV_LORA), ("k_rope", QK_ROPE), ("z_a", A_WIDTH),
                        ("q_m", M_WIDTH), ("k_m", M_WIDTH), ("v_m", M_WIDTH), ("o_m", M_WIDTH),
                        ("z_m", M_WIDTH), ("gates", 4 * M_HEADS), ("g_a", D_MODEL), ("g_b", D_MODEL)):
        seg[name] = w_in[:, o:o + width]
        o += width
    pad = jnp.zeros((D_MODEL, LANES - 4 * M_HEADS), w_in.dtype)
    w_small = jnp.concatenate([seg["c_q"], seg["c_kv"], seg["k_rope"], _rot_cols(seg["k_rope"]),
                               seg["gates"], pad], axis=1).astype(BF16)
    w_big = jnp.concatenate([seg[n] for n in ("z_a", "k_m", "o_m", "z_m", "g_a", "g_b")],
                            axis=1).astype(BF16)
    w_qvt = jnp.concatenate([seg["q_m"], seg["v_m"]], axis=1).T.astype(BF16)
    wq = w_uq.reshape(Q_LORA, A_HEADS, QK_NOPE + QK_ROPE)
    rope = wq[:, :, QK_NOPE:]
    rot = jnp.concatenate([-rope[:, :, QK_ROPE // 2:], rope[:, :, :QK_ROPE // 2]], axis=2)
    wqt = jnp.concatenate([wq, rot], axis=2).reshape(Q_LORA, A_HEADS * QK_PAD).T.astype(BF16)
    wkv = w_ukv.reshape(KV_LORA, A_HEADS, QK_NOPE + V_HEAD)
    wk = wkv[:, :, :QK_NOPE].reshape(KV_LORA, A_HEADS * QK_NOPE).astype(BF16)
    wvt = wkv[:, :, QK_NOPE:].reshape(KV_LORA, A_HEADS * V_HEAD).T.astype(BF16)
    return w_small, w_big, w_qvt, wqt, wk, wvt


def _rope_table(s):
    inv = ROPE_THETA ** (-jnp.arange(0, QK_ROPE, 2, dtype=F32) / QK_ROPE)
    ang = jnp.arange(s, dtype=F32)[:, None] * inv[None, :]
    cos, sin = jnp.cos(ang), jnp.sin(ang)
    return jnp.concatenate([cos, cos, sin, sin], axis=1)


def _trunk(x, layers, norm_f):
    b, s, _ = x.shape
    x2 = x.reshape(b * s, D_MODEL)
    cs = _rope_table(s)
    cst = cs.T
    q_scale = float((QK_NOPE + QK_ROPE) ** -0.5 * 1.4426950408889634)
    for li, ly in enumerate(layers):
        small = _norm_matmul(x2, ly["norm_in"], ly["w_small"], F32, 512, SMALL_COLS)
        big = _norm_matmul(x2, ly["norm_in"], ly["w_big"], BF16, 1024, 1024)
        qt, kc, vt = _mla_prep(small, cs, cst, ly["q_a_norm"], ly["kv_a_norm"], ly["wqt"], ly["wk"],
                               ly["wvt"], b, s, q_scale)
        ya = _flash(qt, kc, vt, big, b, s)
        qvt = _norm_matmul(x2, ly["norm_in"], ly["w_qvt"], BF16, 1024, 1024, feature_major=True)
        col, row = _gates(small, ly["b_i"], ly["b_f"])
        hf, hb = _mlstm(qvt, big, row, col, b, s)
        x2 = _merge(x2, ya, hf, hb, big, ly["m_head_norm"], ly["w_oa"], ly["w_ob"], ly["w_out"],
                    norm_f, li == len(layers) - 1)
    return x2.reshape(b, s, D_MODEL)


def kernel(x_prompt, x_sample, norm_in, w_in, b_gates, q_a_norm, w_uq, kv_a_norm, w_ukv, w_oa,
           m_head_norm, w_ob, w_out, norm_f):
    layers = []
    for l in range(w_in.shape[0]):
        w_small, w_big, w_qvt, wqt, wk, wvt = _pack_layer(w_in[l], w_uq[l], w_ukv[l])
        layers.append(dict(
            norm_in=norm_in[l].reshape(1, D_MODEL), w_small=w_small, w_big=w_big, w_qvt=w_qvt,
            wqt=wqt, wk=wk, wvt=wvt,
            q_a_norm=q_a_norm[l].reshape(1, Q_LORA), kv_a_norm=kv_a_norm[l].reshape(1, KV_LORA),
            b_i=b_gates[l, :GATE_CH].reshape(GATE_CH, 1).astype(F32),
            b_f=b_gates[l, GATE_CH:].reshape(GATE_CH, 1).astype(F32),
            m_head_norm=m_head_norm[l].reshape(1, M_WIDTH),
            w_oa=w_oa[l].astype(BF16), w_ob=w_ob[l].astype(BF16), w_out=w_out[l].astype(BF16)))
    nf = norm_f.reshape(1, D_MODEL)
    return (_trunk(x_prompt, layers, nf), _trunk(x_sample, layers, nf))
```

```python
import functools

import jax
import jax.numpy as jnp
from jax import lax
from jax.experimental import pallas as pl
from jax.experimental.pallas import tpu as pltpu

D_MODEL = 2048
A_HEADS = 8
Q_LORA = 512
KV_LORA = 512
QK_NOPE = 128
QK_ROPE = 64
V_HEAD = 128
ROPE_THETA = 10000.0
A_WIDTH = A_HEADS * V_HEAD
M_HEADS = 8
M_QK = 128
M_V = 128
CHUNK = 64
M_WIDTH = M_HEADS * M_V
EPS = 1e-6

LANES = 128
QK_PAD = 256
KV_CHUNK = 512
FLASH_UNROLL = 16
FLASH_TQ = 512
FLASH_AHEAD = 2
AV_EXT = V_HEAD + 16
VMEM_LIMIT = 56 * 1024 * 1024

SMALL_COLS = Q_LORA + KV_LORA + 2 * LANES
BIG_COLS = 4 * 1024 + 2 * D_MODEL

F32 = jnp.float32
BF16 = jnp.bfloat16


def _rms(xf, g):
    ms = jnp.mean(xf * xf, axis=-1, keepdims=True)
    return xf * lax.rsqrt(ms + EPS) * g


def _params(sem, flags=None):
    return pltpu.CompilerParams(dimension_semantics=sem, vmem_limit_bytes=VMEM_LIMIT, flags=flags)


_NT = (((1,), (1,)), ((), ()))


def _norm_matmul_kernel(x_ref, g_ref, w_ref, o_ref, h_scr, *, rows, feature_major):
    @pl.when(pl.program_id(1) == 0)
    def _():
        def body(r, c):
            sl = pl.ds(pl.multiple_of(r * rows, rows), rows)
            h_scr[sl, :] = _rms(x_ref[sl, :], g_ref[...]).astype(h_scr.dtype)
            return c
        lax.fori_loop(0, x_ref.shape[0] // rows, body, 0)

    if feature_major:
        out = lax.dot_general(w_ref[...], h_scr[...], _NT, preferred_element_type=F32)
        for c in range(o_ref.shape[0]):
            o_ref[c] = out[:, c * o_ref.shape[2]:(c + 1) * o_ref.shape[2]].astype(o_ref.dtype)
    else:
        out = jnp.dot(h_scr[...], w_ref[...], preferred_element_type=F32)
        o_ref[...] = out.astype(o_ref.dtype)


def _norm_matmul(x2, gain, w, out_dtype, tm, tn, feature_major=False):
    t, k = x2.shape
    n = w.shape[0] if feature_major else w.shape[1]
    tm = min(tm, t)
    rows = min(256, tm)
    if feature_major:
        w_spec = pl.BlockSpec((tn, k), lambda i, j: (j, 0))
        o_spec = pl.BlockSpec((tm // MCHUNK, tn, MCHUNK), lambda i, j: (i, j, 0))
        o_shape = (t // MCHUNK, n, MCHUNK)
    else:
        w_spec = pl.BlockSpec((k, tn), lambda i, j: (0, j))
        o_spec = pl.BlockSpec((tm, tn), lambda i, j: (i, j))
        o_shape = (t, n)
    return pl.pallas_call(
        functools.partial(_norm_matmul_kernel, rows=rows, feature_major=feature_major),
        grid=(t // tm, n // tn),
        in_specs=[pl.BlockSpec((tm, k), lambda i, j: (i, 0)),
                  pl.BlockSpec((1, k), lambda i, j: (0, 0)),
                  w_spec],
        out_specs=o_spec,
        out_shape=jax.ShapeDtypeStruct(o_shape, out_dtype),
        scratch_shapes=[pltpu.VMEM((tm, k), BF16)],
        compiler_params=_params(("parallel", "arbitrary")),
        name="in_proj_t" if feature_major else "in_proj",
    )(x2, gain, w)


def _mla_prep_kernel(cq_ref, ckv_ref, rest_ref, cs_ref, cst_ref, gq_ref, gkv_ref,
                     wqt_ref, wk_ref, wvt_ref, qt_ref, k_ref, vt_ref, *, q_scale):
    hq = _rms(cq_ref[...], gq_ref[...]).astype(BF16)
    hkv = _rms(ckv_ref[...], gkv_ref[...]).astype(BF16)
    qt = lax.dot_general(wqt_ref[...], hq, _NT, preferred_element_type=F32)
    vt = lax.dot_general(wvt_ref[...], hkv, _NT, preferred_element_type=F32)
    kn = jnp.dot(hkv, wk_ref[...], preferred_element_type=F32)
    t = rest_ref[...] * cs_ref[...]
    lane = lax.broadcasted_iota(jnp.int32, t.shape, 1)
    k_r = jnp.where(lane < QK_ROPE, t + pltpu.roll(t, QK_ROPE, 1), 0.0)
    cst = cst_ref[...]
    pad = jnp.zeros((QK_PAD - QK_NOPE - QK_ROPE, qt.shape[1]), F32)
    ones_blk = (lax.broadcasted_iota(jnp.int32, (AV_EXT - V_HEAD, qt.shape[1]), 0) == 0).astype(BF16)
    for h in range(A_HEADS):
        qh = qt[h * QK_PAD:(h + 1) * QK_PAD]
        tq = qh[QK_NOPE:] * cst
        qt_ref[h] = jnp.concatenate(
            [qh[:QK_NOPE] * q_scale, (tq[:QK_ROPE] + tq[QK_ROPE:]) * q_scale, pad], axis=0).astype(BF16)
        k_ref[h] = jnp.concatenate([kn[:, h * QK_NOPE:(h + 1) * QK_NOPE], k_r], axis=1).astype(BF16)
        vt_ref[h] = jnp.concatenate([vt[h * V_HEAD:(h + 1) * V_HEAD].astype(BF16), ones_blk], axis=0)


def _mla_prep(small, cs, cst, gq, gkv, wqt, wk, wvt, b, s, q_scale):
    tm = min(KV_CHUNK, s // 2)
    nb = s // tm
    full = lambda bi, i: (0, 0)
    return pl.pallas_call(
        functools.partial(_mla_prep_kernel, q_scale=q_scale),
        grid=(b, nb),
        in_specs=[pl.BlockSpec((tm, Q_LORA), lambda bi, i: (bi * nb + i, 0)),
                  pl.BlockSpec((tm, KV_LORA), lambda bi, i: (bi * nb + i, 1)),
                  pl.BlockSpec((tm, LANES), lambda bi, i: (bi * nb + i, (Q_LORA + KV_LORA) // LANES)),
                  pl.BlockSpec((tm, LANES), lambda bi, i: (i, 0)),
                  pl.BlockSpec((LANES, tm), lambda bi, i: (0, i)),
                  pl.BlockSpec((1, Q_LORA), full),
                  pl.BlockSpec((1, KV_LORA), full),
                  pl.BlockSpec((A_HEADS * QK_PAD, Q_LORA), full),
                  pl.BlockSpec((KV_LORA, A_HEADS * QK_NOPE), full),
                  pl.BlockSpec((A_HEADS * V_HEAD, KV_LORA), full)],
        out_specs=[pl.BlockSpec((None, A_HEADS, QK_PAD, tm), lambda bi, i: (bi, 0, 0, i)),
                   pl.BlockSpec((None, A_HEADS, tm, QK_PAD), lambda bi, i: (bi, 0, i, 0)),
                   pl.BlockSpec((None, A_HEADS, None, AV_EXT, tm), lambda bi, i: (bi, 0, i, 0, 0))],
        out_shape=[jax.ShapeDtypeStruct((b, A_HEADS, QK_PAD, s), BF16),
                   jax.ShapeDtypeStruct((b, A_HEADS, s, QK_PAD), BF16),
                   jax.ShapeDtypeStruct((b, A_HEADS, nb, AV_EXT, tm), BF16)],
        compiler_params=_params(("parallel", "parallel")),
        name="mla_prep",
    )(small, small, small, cs, cst, gq, gkv, wqt, wk, wvt)


def _flash_plan(nkb):
    unroll = next(u for u in (FLASH_UNROLL, 8, 2) if nkb % u == 0)
    ahead = FLASH_AHEAD if (nkb % (2 * FLASH_AHEAD) == 0 and unroll % (2 * FLASH_AHEAD) == 0) else 1
    return unroll, ahead


def _flash_kernel(qt_ref, qtn_ref, k_ref, vt_ref, z_ref, o_ref, q2_scr, s_scr, cmax_scr, acc_scr):
    tq = qt_ref.shape[1]
    nkb = vt_ref.shape[0]
    tk = k_ref.shape[0] // nkb
    unroll, ahead = _flash_plan(nkb)
    slots = s_scr.shape[0]
    acc_scr[...] = jnp.zeros(acc_scr.shape, F32)
    q2_scr[0] = qt_ref[...]
    q2_scr[1] = qtn_ref[...]

    def scores(kb, qsel, slot):
        start = kb * tk if isinstance(kb, int) else pl.multiple_of(kb * tk, tk)
        st = jnp.dot(k_ref[pl.ds(start, tk), :], q2_scr[qsel], preferred_element_type=F32)
        s_scr[slot] = st
        return jnp.max(st, axis=0, keepdims=True)

    @pl.when(pl.program_id(2) == 0)
    def _():
        for a in range(ahead):
            cmax_scr[a] = scores(a, 0, a)

    def softmax_values(kb, slot, cmax, m_prev):
        m_new = jnp.maximum(m_prev, cmax)
        pt = jnp.exp2((s_scr[slot] - m_new).astype(BF16))
        acc_scr[...] = (jnp.exp2(m_prev - m_new) * acc_scr[...]
                        + jnp.dot(vt_ref[kb], pt, preferred_element_type=F32))
        return m_new

    def trip(j, carry):
        m, cmax = carry[0], list(carry[1:])
        for u in range(unroll):
            kb = unroll * j + u
            nxt = kb + ahead
            if u >= unroll - ahead:
                wrap = nxt >= nkb
                c_new = scores(jnp.where(wrap, nxt - nkb, nxt), jnp.where(wrap, 1, 0),
                               (u + ahead) % slots)
            else:
                c_new = scores(nxt, 0, (u + ahead) % slots)
            m = softmax_values(kb, u % slots, cmax[0], m)
            cmax = cmax[1:] + [c_new]
        return (m, *cmax)

    init = (jnp.full((1, tq), -jnp.inf, F32), *[cmax_scr[a] for a in range(ahead)])
    out = lax.fori_loop(0, nkb // unroll, trip, init)
    for a in range(ahead):
        cmax_scr[a] = out[1 + a]
    z = z_ref[...].astype(F32)
    acc = acc_scr[...]
    o = (acc[:V_HEAD] / acc[V_HEAD:V_HEAD + 1]).T
    o_ref[...] = (o * (z * jax.nn.sigmoid(z))).astype(o_ref.dtype)


def _flash(qt, kc, vt, big, b, s):
    tq = min(FLASH_TQ, s)
    nq = s // tq
    nkb, tk = vt.shape[2], vt.shape[4]
    _, ahead = _flash_plan(nkb)
    return pl.pallas_call(
        _flash_kernel,
        grid=(b, A_HEADS, nq),
        in_specs=[pl.BlockSpec((None, None, QK_PAD, tq), lambda bi, h, i: (bi, h, 0, i)),
                  pl.BlockSpec((None, None, QK_PAD, tq),
                               lambda bi, h, i: (bi, h, 0, jnp.minimum(i + 1, nq - 1))),
                  pl.BlockSpec((None, None, s, QK_PAD), lambda bi, h, i: (bi, h, 0, 0)),
                  pl.BlockSpec((None, None, nkb, AV_EXT, tk), lambda bi, h, i: (bi, h, 0, 0, 0)),
                  pl.BlockSpec((tq, V_HEAD), lambda bi, h, i: (bi * nq + i, h))],
        out_specs=pl.BlockSpec((tq, V_HEAD), lambda bi, h, i: (bi * nq + i, h)),
        out_shape=jax.ShapeDtypeStruct((b * s, A_WIDTH), BF16),
        scratch_shapes=[pltpu.VMEM((2, QK_PAD, tq), BF16),
                        pltpu.VMEM((2 * ahead, tk, tq), F32),
                        pltpu.VMEM((ahead, 1, tq), F32),
                        pltpu.VMEM((AV_EXT, tq), F32)],
        compiler_params=_params(("parallel", "parallel", "arbitrary")),
        name="mla_flash",
    )(qt, qt, kc, vt, big)


GATE_CH = 2 * M_HEADS
MCHUNK = 2 * LANES
GATE_CHUNKS = 16
ROW_B, ROW_R, ROW_CM, ROW_G, ROW_RMAX = (i * GATE_CH for i in range(5))
ROW_PACK = 5 * GATE_CH


def _gate_kernel(x_ref, bi_ref, bf_ref, col_ref, row_ref):
    nc = row_ref.shape[0]
    shape = (nc * GATE_CH, LANES)
    pos = lax.broadcasted_iota(jnp.int32, shape, 1)
    is_fw = lax.broadcasted_iota(jnp.int32, shape, 0) % GATE_CH < M_HEADS
    shifts = (1, 2, 4, 8, 16, 32, 64)

    def scan(y, op, ident):
        for sft in shifts:
            y_f = jnp.where(pos >= sft, pltpu.roll(y, sft, 1), ident)
            y_b = jnp.where(pos < LANES - sft, pltpu.roll(y, LANES - sft, 1), ident)
            y = op(y, jnp.where(is_fw, y_f, y_b))
        return y

    def total(y, op):
        for sft in shifts:
            y = op(y, pltpu.roll(y, sft, 1))
        return y

    zpad = jnp.zeros((LANES - GATE_CH, LANES), F32)
    log_i, log_f = ([], []), ([], [])
    for c in range(nc):
        for half in range(2):
            rs = slice(c * MCHUNK + half * LANES, c * MCHUNK + (half + 1) * LANES)
            xt = x_ref[rs, :].T
            log_i[half].append(xt[0:GATE_CH] + bi_ref[...])
            log_f[half].append(jax.nn.log_sigmoid(xt[GATE_CH:2 * GATE_CH] + bf_ref[...]))
    li0, li1 = (jnp.concatenate(v, axis=0) for v in log_i)
    lf0, lf1 = (jnp.concatenate(v, axis=0) for v in log_f)
    t0, t1 = total(lf0, jnp.add), total(lf1, jnp.add)
    b0 = scan(lf0, jnp.add, 0.0) + jnp.where(is_fw, 0.0, t1)
    b1 = scan(lf1, jnp.add, 0.0) + jnp.where(is_fw, t0, 0.0)
    r0, r1 = li0 - b0, li1 - b1
    a0, a1 = total(r0, jnp.maximum), total(r1, jnp.maximum)
    e0, e1 = scan(r0, jnp.maximum, -jnp.inf), scan(r1, jnp.maximum, -jnp.inf)
    cm0 = jnp.where(is_fw, e0, jnp.maximum(e0, a1))
    cm1 = jnp.where(is_fw, jnp.maximum(e1, a0), e1)
    g, rmax = t0 + t1, jnp.maximum(a0, a1)
    for c in range(nc):
        cs = slice(c * GATE_CH, (c + 1) * GATE_CH)
        for off, v0, v1 in ((ROW_B, b0, b1), (ROW_R, r0, r1), (ROW_CM, cm0, cm1),
                            (ROW_G, g, g), (ROW_RMAX, rmax, rmax)):
            row_ref[c, off:off + GATE_CH, 0:LANES] = v0[cs]
            row_ref[c, off:off + GATE_CH, LANES:MCHUNK] = v1[cs]
        for half, r in ((0, r0), (1, r1)):
            rs = slice(c * MCHUNK + half * LANES, c * MCHUNK + (half + 1) * LANES)
            col_ref[rs, :] = jnp.concatenate([r[cs], zpad], axis=0).T


def _gates(small, b_i, b_f):
    t = small.shape[0]
    nc = t // MCHUNK
    cpb = min(GATE_CHUNKS, nc)
    return pl.pallas_call(
        _gate_kernel,
        grid=(nc // cpb,),
        in_specs=[pl.BlockSpec((cpb * MCHUNK, LANES), lambda i: (i, SMALL_COLS // LANES - 1)),
                  pl.BlockSpec((GATE_CH, 1), lambda i: (0, 0)),
                  pl.BlockSpec((GATE_CH, 1), lambda i: (0, 0))],
        out_specs=[pl.BlockSpec((cpb * MCHUNK, LANES), lambda i: (i, 0)),
                   pl.BlockSpec((cpb, ROW_PACK, MCHUNK), lambda i: (i, 0, 0))],
        out_shape=[jax.ShapeDtypeStruct((t, LANES), F32),
                   jax.ShapeDtypeStruct((nc, ROW_PACK, MCHUNK), F32)],
        compiler_params=_params(("parallel",)),
        name="mlstm_gates",
    )(small, b_i, b_f)


V_EXT = M_V + 16


def _mlstm_kernel(qtf_ref, kf_ref, vtf_ref, rowf_ref, colf_ref,
                  qtb_ref, kb_ref, vtb_ref, rowb_ref, colb_ref,
                  hf_ref, hb_ref, c_scr, m_scr):
    @pl.when(pl.program_id(1) == 0)
    def _():
        c_scr[...] = jnp.zeros(c_scr.shape, F32)
        m_scr[...] = jnp.zeros(m_scr.shape, F32)

    ll = lax.broadcasted_iota(jnp.int32, (MCHUNK, MCHUNK), 0)
    jj = lax.broadcasted_iota(jnp.int32, (MCHUNK, MCHUNK), 1)
    ones_blk = (lax.broadcasted_iota(jnp.int32, (V_EXT - M_V, MCHUNK), 0) == 0).astype(F32)
    inv_scale = float(M_QK) ** 0.5
    m_all = m_scr[...]
    m_next = []
    dirs = ((qtf_ref, kf_ref, vtf_ref, rowf_ref, colf_ref[...], hf_ref, ll <= jj),
            (qtb_ref, kb_ref, vtb_ref, rowb_ref, colb_ref[...], hb_ref, ll >= jj))
    units = [(d, h) for d in range(2) for h in range(M_HEADS)]

    def row(d, off, h):
        r = off + d * M_HEADS + h
        return dirs[d][3][r:r + 1, :]

    st, cq = {}, {}
    for d, h in units:
        qt_ref, k_ref = dirs[d][0], dirs[d][1]
        hs = slice(h * M_QK, (h + 1) * M_QK)
        qt = qt_ref[hs, :]
        st[d, h] = jnp.dot(k_ref[:, hs], qt, preferred_element_type=F32)
        cq[d, h] = jnp.dot(c_scr[d, h].astype(BF16), qt, preferred_element_type=F32)

    for d, h in units:
        _, _, vt_ref, _, col, h_ref, mask = dirs[d]
        ch = d * M_HEADS + h
        hs = slice(h * M_QK, (h + 1) * M_QK)
        m_prev = m_all[ch:ch + 1, :]
        mm = jnp.maximum(m_prev, row(d, ROW_CM, h))
        r_col = col[:, ch:ch + 1]
        pt = (jnp.exp(jnp.where(mask, r_col - mm, -jnp.inf)) * st[d, h]).astype(BF16)
        vt_ext = jnp.concatenate([vt_ref[hs, :], ones_blk.astype(BF16)], axis=0)
        tot = (jnp.dot(vt_ext, pt, preferred_element_type=F32)
               + jnp.exp(m_prev - mm) * cq[d, h])
        floor = jnp.exp(-row(d, ROW_B, h) - mm) * inv_scale
        ht = tot[:M_V] / jnp.maximum(jnp.abs(tot[M_V:M_V + 1]), floor)
        h_ref[:, hs] = ht.T

    for d, h in units:
        k_ref, vt_ref = dirs[d][1], dirs[d][2]
        ch = d * M_HEADS + h
        hs = slice(h * M_QK, (h + 1) * M_QK)
        m_prev = m_all[ch:ch + 1, :]
        mx = jnp.maximum(m_prev, row(d, ROW_RMAX, h))
        w_row = jnp.exp(row(d, ROW_R, h) - mx)
        lhs = jnp.concatenate([(vt_ref[hs, :].astype(F32) * w_row).astype(BF16),
                               (ones_blk * w_row).astype(BF16)], axis=0)
        c_scr[d, h] = (jnp.exp(m_prev - mx)[:, :M_QK] * c_scr[d, h]
                       + jnp.dot(lhs, k_ref[:, hs], preferred_element_type=F32))
        m_next.append(row(d, ROW_G, h) + mx)
    m_scr[...] = jnp.concatenate(m_next, axis=0)


def _mlstm(qvt, big, row, col, b, s):
    nc = s // MCHUNK
    fw = lambda bi, i: bi * nc + i
    bw = lambda bi, i: bi * nc + (nc - 1 - i)

    def specs(blk):
        return [pl.BlockSpec((None, M_WIDTH, MCHUNK), lambda bi, i: (blk(bi, i), 0, 0)),
                pl.BlockSpec((MCHUNK, M_WIDTH), lambda bi, i: (blk(bi, i), 1)),
                pl.BlockSpec((None, M_WIDTH, MCHUNK), lambda bi, i: (blk(bi, i), 1, 0)),
                pl.BlockSpec((None, ROW_PACK, MCHUNK), lambda bi, i: (blk(bi, i), 0, 0)),
                pl.BlockSpec((MCHUNK, LANES), lambda bi, i: (blk(bi, i), 0))]

    out = jax.ShapeDtypeStruct((b * s, M_WIDTH), F32)
    return pl.pallas_call(
        _mlstm_kernel,
        grid=(b, nc),
        in_specs=specs(fw) + specs(bw),
        out_specs=[pl.BlockSpec((MCHUNK, M_WIDTH), lambda bi, i: (fw(bi, i), 0)),
                   pl.BlockSpec((MCHUNK, M_WIDTH), lambda bi, i: (bw(bi, i), 0))],
        out_shape=[out, out],
        scratch_shapes=[pltpu.VMEM((2, M_HEADS, V_EXT, M_QK), F32),
                        pltpu.VMEM((GATE_CH, MCHUNK), F32)],
        compiler_params=_params(("parallel", "arbitrary")),
        name="mlstm",
    )(qvt, big, qvt, row, col, qvt, big, qvt, row, col)


def _merge_kernel(x_ref, ya_ref, hf_ref, hb_ref, om_ref, zm_ref, ga_ref, gb_ref, hn_ref,
                  woa_ref, wob_ref, wout_ref, nf_ref, o_ref, *, final_norm):
    hm = hf_ref[...] + hb_ref[...]
    parts = []
    for h in range(M_HEADS):
        hh = hm[:, h * M_V:(h + 1) * M_V]
        ms = jnp.mean(hh * hh, axis=-1, keepdims=True)
        parts.append(hh * lax.rsqrt(ms + EPS))
    hn = jnp.concatenate(parts, axis=1) * hn_ref[...]
    zm = zm_ref[...].astype(F32)
    yb = hn * jax.nn.sigmoid(om_ref[...].astype(F32)) * (zm * jax.nn.sigmoid(zm))
    pa = jnp.dot(ya_ref[...], woa_ref[...], preferred_element_type=F32)
    pb = jnp.dot(yb.astype(BF16), wob_ref[...], preferred_element_type=F32)
    merged = (jax.nn.sigmoid(ga_ref[...].astype(F32)) * pa
              + jax.nn.sigmoid(gb_ref[...].astype(F32)) * pb)
    out = x_ref[...] + jnp.dot(merged.astype(BF16), wout_ref[...], preferred_element_type=F32)
    if final_norm:
        out = _rms(out, nf_ref[...])
    o_ref[...] = out


def _merge(x2, ya, hf, hb, big, hn, woa, wob, wout, nf, final_norm):
    t = x2.shape[0]
    tm = min(256, t)
    row = lambda c: (lambda i: (i, c))
    const = lambda shape: pl.BlockSpec(shape, lambda i: (0, 0), pipeline_mode=pl.Buffered(1))
    return pl.pallas_call(
        functools.partial(_merge_kernel, final_norm=final_norm),
        grid=(t // tm,),
        in_specs=[pl.BlockSpec((tm, D_MODEL), row(0)),
                  pl.BlockSpec((tm, A_WIDTH), row(0)),
                  pl.BlockSpec((tm, M_WIDTH), row(0)),
                  pl.BlockSpec((tm, M_WIDTH), row(0)),
                  pl.BlockSpec((tm, M_WIDTH), row(2)),
                  pl.BlockSpec((tm, M_WIDTH), row(3)),
                  pl.BlockSpec((tm, D_MODEL), row(2)),
                  pl.BlockSpec((tm, D_MODEL), row(3)),
                  const((1, M_WIDTH)),
                  const((A_WIDTH, D_MODEL)),
                  const((M_WIDTH, D_MODEL)),
                  const((D_MODEL, D_MODEL)),
                  const((1, D_MODEL))],
        out_specs=pl.BlockSpec((tm, D_MODEL), row(0)),
        out_shape=jax.ShapeDtypeStruct((t, D_MODEL), F32),
        compiler_params=_params(("parallel",)),
        name="merge_out",
    )(x2, ya, hf, hb, big, big, big, big, hn, woa, wob, wout, nf)


def _rot_cols(w):
    half = QK_ROPE // 2
    return jnp.concatenate([-w[:, half:], w[:, :half]], axis=1)


def _pack_layer(w_in, w_uq, w_ukv):
    o = 0
    seg = {}
    for name, width in (("c_q", Q_LORA), ("c_kv", KV_LORA), ("k_rope", QK_ROPE), ("z_a", A_WIDTH),
                        ("q_m", M_WIDTH), ("k_m", M_WIDTH), ("v_m", M_WIDTH), ("o_m", M_WIDTH),
                        ("z_m", M_WIDTH), ("gates", 4 * M_HEADS), ("g_a", D_MODEL), ("g_b", D_MODEL)):
        seg[name] = w_in[:, o:o + width]
        o += width
    pad = jnp.zeros((D_MODEL, LANES - 4 * M_HEADS), w_in.dtype)
    w_small = jnp.concatenate([seg["c_q"], seg["c_kv"], seg["k_rope"], _rot_cols(seg["k_rope"]),
                               seg["gates"], pad], axis=1).astype(BF16)
    w_big = jnp.concatenate([seg[n] for n in ("z_a", "k_m", "o_m", "z_m", "g_a", "g_b")],
                            axis=1).astype(BF16)
    w_qvt = jnp.concatenate([seg["q_m"], seg["v_m"]], axis=1).T.astype(BF16)
    wq = w_uq.reshape(Q_LORA, A_HEADS, QK_NOPE + QK_ROPE)
    rope = wq[:, :, QK_NOPE:]
    rot = jnp.concatenate([-rope[:, :, QK_ROPE // 2:], rope[:, :, :QK_ROPE // 2]], axis=2)
    wqt = jnp.concatenate([wq, rot], axis=2).reshape(Q_LORA, A_HEADS * QK_PAD).T.astype(BF16)
    wkv = w_ukv.reshape(KV_LORA, A_HEADS, QK_NOPE + V_HEAD)
    wk = wkv[:, :, :QK_NOPE].reshape(KV_LORA, A_HEADS * QK_NOPE).astype(BF16)
    wvt = wkv[:, :, QK_NOPE:].reshape(KV_LORA, A_HEADS * V_HEAD).T.astype(BF16)
    return w_small, w_big, w_qvt, wqt, wk, wvt


def _rope_table(s):
    inv = ROPE_THETA ** (-jnp.arange(0, QK_ROPE, 2, dtype=F32) / QK_ROPE)
    ang = jnp.arange(s, dtype=F32)[:, None] * inv[None, :]
    cos, sin = jnp.cos(ang), jnp.sin(ang)
    return jnp.concatenate([cos, cos, sin, sin], axis=1)


def _trunk(x, layers, norm_f):
    b, s, _ = x.shape
    x2 = x.reshape(b * s, D_MODEL)
    cs = _rope_table(s)
    cst = cs.T
    q_scale = float((QK_NOPE + QK_ROPE) ** -0.5 * 1.4426950408889634)
    for li, ly in enumerate(layers):
        small = _norm_matmul(x2, ly["norm_in"], ly["w_small"], F32, 512, SMALL_COLS)
        big = _norm_matmul(x2, ly["norm_in"], ly["w_big"], BF16, 1024, 1024)
        qt, kc, vt = _mla_prep(small, cs, cst, ly["q_a_norm"], ly["kv_a_norm"], ly["wqt"], ly["wk"],
                               ly["wvt"], b, s, q_scale)
        ya = _flash(qt, kc, vt, big, b, s)
        qvt = _norm_matmul(x2, ly["norm_in"], ly["w_qvt"], BF16, 1024, 1024, feature_major=True)
        col, row = _gates(small, ly["b_i"], ly["b_f"])
        hf, hb = _mlstm(qvt, big, row, col, b, s)
        x2 = _merge(x2, ya, hf, hb, big, ly["m_head_norm"], ly["w_oa"], ly["w_ob"], ly["w_out"],
                    norm_f, li == len(layers) - 1)
    return x2.reshape(b, s, D_MODEL)


def kernel(x_prompt, x_sample, norm_in, w_in, b_gates, q_a_norm, w_uq, kv_a_norm, w_ukv, w_oa,
           m_head_norm, w_ob, w_out, norm_f):
    layers = []
    for l in range(w_in.shape[0]):
        w_small, w_big, w_qvt, wqt, wk, wvt = _pack_layer(w_in[l], w_uq[l], w_ukv[l])
        layers.append(dict(
            norm_in=norm_in[l].reshape(1, D_MODEL), w_small=w_small, w_big=w_big, w_qvt=w_qvt,
            wqt=wqt, wk=wk, wvt=wvt,
            q_a_norm=q_a_norm[l].reshape(1, Q_LORA), kv_a_norm=kv_a_norm[l].reshape(1, KV_LORA),
            b_i=b_gates[l, :GATE_CH].reshape(GATE_CH, 1).astype(F32),
            b_f=b_gates[l, GATE_CH:].reshape(GATE_CH, 1).astype(F32),
            m_head_norm=m_head_norm[l].reshape(1, M_WIDTH),
            w_oa=w_oa[l].astype(BF16), w_ob=w_ob[l].astype(BF16), w_out=w_out[l].astype(BF16)))
    nf = norm_f.reshape(1, D_MODEL)
    return (_trunk(x_prompt, layers, nf), _trunk(x_sample, layers, nf))
```

```python
import functools

import jax
import jax.numpy as jnp
from jax import lax
from jax.experimental import pallas as pl
from jax.experimental.pallas import tpu as pltpu

D_MODEL = 2048
A_HEADS = 8
Q_LORA = 512
KV_LORA = 512
QK_NOPE = 128
QK_ROPE = 64
V_HEAD = 128
ROPE_THETA = 10000.0
A_WIDTH = A_HEADS * V_HEAD
M_HEADS = 8
M_QK = 128
M_V = 128
CHUNK = 64
M_WIDTH = M_HEADS * M_V
EPS = 1e-6

LANES = 128
QK_PAD = 256
KV_CHUNK = 512
FLASH_UNROLL = 16
FLASH_TQ = 512
FLASH_AHEAD = 2
AV_EXT = V_HEAD + 16
VMEM_LIMIT = 56 * 1024 * 1024

SMALL_COLS = Q_LORA + KV_LORA + 2 * LANES
BIG_COLS = 4 * 1024 + 2 * D_MODEL

F32 = jnp.float32
BF16 = jnp.bfloat16


def _rms(xf, g):
    ms = jnp.mean(xf * xf, axis=-1, keepdims=True)
    return xf * lax.rsqrt(ms + EPS) * g


def _params(sem, flags=None):
    return pltpu.CompilerParams(dimension_semantics=sem, vmem_limit_bytes=VMEM_LIMIT, flags=flags)


_NT = (((1,), (1,)), ((), ()))


def _norm_matmul_kernel(x_ref, g_ref, w_ref, o_ref, h_scr, *, rows, feature_major):
    @pl.when(pl.program_id(1) == 0)
    def _():
        def body(r, c):
            sl = pl.ds(pl.multiple_of(r * rows, rows), rows)
            h_scr[sl, :] = _rms(x_ref[sl, :], g_ref[...]).astype(h_scr.dtype)
            return c
        lax.fori_loop(0, x_ref.shape[0] // rows, body, 0)

    if feature_major:
        out = lax.dot_general(w_ref[...], h_scr[...], _NT, preferred_element_type=F32)
        for c in range(o_ref.shape[0]):
            o_ref[c] = out[:, c * o_ref.shape[2]:(c + 1) * o_ref.shape[2]].astype(o_ref.dtype)
    else:
        out = jnp.dot(h_scr[...], w_ref[...], preferred_element_type=F32)
        o_ref[...] = out.astype(o_ref.dtype)


def _norm_matmul(x2, gain, w, out_dtype, tm, tn, feature_major=False):
    t, k = x2.shape
    n = w.shape[0] if feature_major else w.shape[1]
    tm = min(tm, t)
    rows = min(256, tm)
    if feature_major:
        w_spec = pl.BlockSpec((tn, k), lambda i, j: (j, 0))
        o_spec = pl.BlockSpec((tm // MCHUNK, tn, MCHUNK), lambda i, j: (i, j, 0))
        o_shape = (t // MCHUNK, n, MCHUNK)
    else:
        w_spec = pl.BlockSpec((k, tn), lambda i, j: (0, j))
        o_spec = pl.BlockSpec((tm, tn), lambda i, j: (i, j))
        o_shape = (t, n)
    return pl.pallas_call(
        functools.partial(_norm_matmul_kernel, rows=rows, feature_major=feature_major),
        grid=(t // tm, n // tn),
        in_specs=[pl.BlockSpec((tm, k), lambda i, j: (i, 0)),
                  pl.BlockSpec((1, k), lambda i, j: (0, 0)),
                  w_spec],
        out_specs=o_spec,
        out_shape=jax.ShapeDtypeStruct(o_shape, out_dtype),
        scratch_shapes=[pltpu.VMEM((tm, k), BF16)],
        compiler_params=_params(("parallel", "arbitrary")),
        name="in_proj_t" if feature_major else "in_proj",
    )(x2, gain, w)


def _mla_prep_kernel(cq_ref, ckv_ref, rest_ref, cs_ref, cst_ref, gq_ref, gkv_ref,
                     wqt_ref, wk_ref, wvt_ref, qt_ref, k_ref, vt_ref, *, q_scale):
    hq = _rms(cq_ref[...], gq_ref[...]).astype(BF16)
    hkv = _rms(ckv_ref[...], gkv_ref[...]).astype(BF16)
    qt = lax.dot_general(wqt_ref[...], hq, _NT, preferred_element_type=F32)
    vt = lax.dot_general(wvt_ref[...], hkv, _NT, preferred_element_type=F32)
    kn = jnp.dot(hkv, wk_ref[...], preferred_element_type=F32)
    t = rest_ref[...] * cs_ref[...]
    lane = lax.broadcasted_iota(jnp.int32, t.shape, 1)
    k_r = jnp.where(lane < QK_ROPE, t + pltpu.roll(t, QK_ROPE, 1), 0.0)
    cst = cst_ref[...]
    pad = jnp.zeros((QK_PAD - QK_NOPE - QK_ROPE, qt.shape[1]), F32)
    ones_blk = (lax.broadcasted_iota(jnp.int32, (AV_EXT - V_HEAD, qt.shape[1]), 0) == 0).astype(BF16)
    for h in range(A_HEADS):
        qh = qt[h * QK_PAD:(h + 1) * QK_PAD]
        tq = qh[QK_NOPE:] * cst
        qt_ref[h] = jnp.concatenate(
            [qh[:QK_NOPE] * q_scale, (tq[:QK_ROPE] + tq[QK_ROPE:]) * q_scale, pad], axis=0).astype(BF16)
        k_ref[h] = jnp.concatenate([kn[:, h * QK_NOPE:(h + 1) * QK_NOPE], k_r], axis=1).astype(BF16)
        vt_ref[h] = jnp.concatenate([vt[h * V_HEAD:(h + 1) * V_HEAD].astype(BF16), ones_blk], axis=0)


def _mla_prep(small, cs, cst, gq, gkv, wqt, wk, wvt, b, s, q_scale):
    tm = min(KV_CHUNK, s // 2)
    nb = s // tm
    full = lambda bi, i: (0, 0)
    return pl.pallas_call(
        functools.partial(_mla_prep_kernel, q_scale=q_scale),
        grid=(b, nb),
        in_specs=[pl.BlockSpec((tm, Q_LORA), lambda bi, i: (bi * nb + i, 0)),
                  pl.BlockSpec((tm, KV_LORA), lambda bi, i: (bi * nb + i, 1)),
                  pl.BlockSpec((tm, LANES), lambda bi, i: (bi * nb + i, (Q_LORA + KV_LORA) // LANES)),
                  pl.BlockSpec((tm, LANES), lambda bi, i: (i, 0)),
                  pl.BlockSpec((LANES, tm), lambda bi, i: (0, i)),
                  pl.BlockSpec((1, Q_LORA), full),
                  pl.BlockSpec((1, KV_LORA), full),
                  pl.BlockSpec((A_HEADS * QK_PAD, Q_LORA), full),
                  pl.BlockSpec((KV_LORA, A_HEADS * QK_NOPE), full),
                  pl.BlockSpec((A_HEADS * V_HEAD, KV_LORA), full)],
        out_specs=[pl.BlockSpec((None, A_HEADS, QK_PAD, tm), lambda bi, i: (bi, 0, 0, i)),
                   pl.BlockSpec((None, A_HEADS, tm, QK_PAD), lambda bi, i: (bi, 0, i, 0)),
                   pl.BlockSpec((None, A_HEADS, None, AV_EXT, tm), lambda bi, i: (bi, 0, i, 0, 0))],
        out_shape=[jax.ShapeDtypeStruct((b, A_HEADS, QK_PAD, s), BF16),
                   jax.ShapeDtypeStruct((b, A_HEADS, s, QK_PAD), BF16),
                   jax.ShapeDtypeStruct((b, A_HEADS, nb, AV_EXT, tm), BF16)],
        compiler_params=_params(("parallel", "parallel")),
        name="mla_prep",
    )(small, small, small, cs, cst, gq, gkv, wqt, wk, wvt)


def _flash_plan(nkb):
    unroll = next(u for u in (FLASH_UNROLL, 8, 2) if nkb % u == 0)
    ahead = FLASH_AHEAD if (nkb % (2 * FLASH_AHEAD) == 0 and unroll % (2 * FLASH_AHEAD) == 0) else 1
    return unroll, ahead


def _flash_kernel(qt_ref, qtn_ref, k_ref, vt_ref, z_ref, o_ref, q2_scr, s_scr, cmax_scr, acc_scr):
    tq = qt_ref.shape[1]
    nkb = vt_ref.shape[0]
    tk = k_ref.shape[0] // nkb
    unroll, ahead = _flash_plan(nkb)
    slots = s_scr.shape[0]
    acc_scr[...] = jnp.zeros(acc_scr.shape, F32)
    q2_scr[0] = qt_ref[...]
    q2_scr[1] = qtn_ref[...]

    def scores(kb, qsel, slot):
        start = kb * tk if isinstance(kb, int) else pl.multiple_of(kb * tk, tk)
        st = jnp.dot(k_ref[pl.ds(start, tk), :], q2_scr[qsel], preferred_element_type=F32)
        s_scr[slot] = st
        return jnp.max(st, axis=0, keepdims=True)

    @pl.when(pl.program_id(2) == 0)
    def _():
        for a in range(ahead):
            cmax_scr[a] = scores(a, 0, a)

    def softmax_values(kb, slot, cmax, m_prev):
        m_new = jnp.maximum(m_prev, cmax)
        pt = jnp.exp2((s_scr[slot] - m_new).astype(BF16))
        acc_scr[...] = (jnp.exp2(m_prev - m_new) * acc_scr[...]
                        + jnp.dot(vt_ref[kb], pt, preferred_element_type=F32))
        return m_new

    def trip(j, carry):
        m, cmax = carry[0], list(carry[1:])
        for u in range(unroll):
            kb = unroll * j + u
            nxt = kb + ahead
            if u >= unroll - ahead:
                wrap = nxt >= nkb
                c_new = scores(jnp.where(wrap, nxt - nkb, nxt), jnp.where(wrap, 1, 0),
                               (u + ahead) % slots)
            else:
                c_new = scores(nxt, 0, (u + ahead) % slots)
            m = softmax_values(kb, u % slots, cmax[0], m)
            cmax = cmax[1:] + [c_new]
        return (m, *cmax)

    init = (jnp.full((1, tq), -jnp.inf, F32), *[cmax_scr[a] for a in range(ahead)])
    out = lax.fori_loop(0, nkb // unroll, trip, init)
    for a in range(ahead):
        cmax_scr[a] = out[1 + a]
    z = z_ref[...].astype(F32)
    acc = acc_scr[...]
    o = (acc[:V_HEAD] / acc[V_HEAD:V_HEAD + 1]).T
    o_ref[...] = (o * (z * jax.nn.sigmoid(z))).astype(o_ref.dtype)


def _flash(qt, kc, vt, big, b, s):
    tq = min(FLASH_TQ, s)
    nq = s // tq
    nkb, tk = vt.shape[2], vt.shape[4]
    _, ahead = _flash_plan(nkb)
    return pl.pallas_call(
        _flash_kernel,
        grid=(b, A_HEADS, nq),
        in_specs=[pl.BlockSpec((None, None, QK_PAD, tq), lambda bi, h, i: (bi, h, 0, i)),
                  pl.BlockSpec((None, None, QK_PAD, tq),
                               lambda bi, h, i: (bi, h, 0, jnp.minimum(i + 1, nq - 1))),
                  pl.BlockSpec((None, None, s, QK_PAD), lambda bi, h, i: (bi, h, 0, 0)),
                  pl.BlockSpec((None, None, nkb, AV_EXT, tk), lambda bi, h, i: (bi, h, 0, 0, 0)),
                  pl.BlockSpec((tq, V_HEAD), lambda bi, h, i: (bi * nq + i, h))],
        out_specs=pl.BlockSpec((tq, V_HEAD), lambda bi, h, i: (bi * nq + i, h)),
        out_shape=jax.ShapeDtypeStruct((b * s, A_WIDTH), BF16),
        scratch_shapes=[pltpu.VMEM((2, QK_PAD, tq), BF16),
                        pltpu.VMEM((2 * ahead, tk, tq), F32),
                        pltpu.VMEM((ahead, 1, tq), F32),
                        pltpu.VMEM((AV_EXT, tq), F32)],
        compiler_params=_params(("parallel", "parallel", "arbitrary")),
        name="mla_flash",
    )(qt, qt, kc, vt, big)


GATE_CH = 2 * M_HEADS
MCHUNK = 2 * LANES
GATE_CHUNKS = 16
ROW_B, ROW_R, ROW_CM, ROW_G, ROW_RMAX = (i * GATE_CH for i in range(5))
ROW_PACK = 5 * GATE_CH


def _gate_kernel(x_ref, bi_ref, bf_ref, col_ref, row_ref):
    nc = row_ref.shape[0]
    shape = (nc * GATE_CH, LANES)
    pos = lax.broadcasted_iota(jnp.int32, shape, 1)
    is_fw = lax.broadcasted_iota(jnp.int32, shape, 0) % GATE_CH < M_HEADS
    shifts = (1, 2, 4, 8, 16, 32, 64)

    def scan(y, op, ident):
        for sft in shifts:
            y_f = jnp.where(pos >= sft, pltpu.roll(y, sft, 1), ident)
            y_b = jnp.where(pos < LANES - sft, pltpu.roll(y, LANES - sft, 1), ident)
            y = op(y, jnp.where(is_fw, y_f, y_b))
        return y

    def total(y, op):
        for sft in shifts:
            y = op(y, pltpu.roll(y, sft, 1))
        return y

    zpad = jnp.zeros((LANES - GATE_CH, LANES), F32)
    log_i, log_f = ([], []), ([], [])
    for c in range(nc):
        for half in range(2):
            rs = slice(c * MCHUNK + half * LANES, c * MCHUNK + (half + 1) * LANES)
            xt = x_ref[rs, :].T
            log_i[half].append(xt[0:GATE_CH] + bi_ref[...])
            log_f[half].append(jax.nn.log_sigmoid(xt[GATE_CH:2 * GATE_CH] + bf_ref[...]))
    li0, li1 = (jnp.concatenate(v, axis=0) for v in log_i)
    lf0, lf1 = (jnp.concatenate(v, axis=0) for v in log_f)
    t0, t1 = total(lf0, jnp.add), total(lf1, jnp.add)
    b0 = scan(lf0, jnp.add, 0.0) + jnp.where(is_fw, 0.0, t1)
    b1 = scan(lf1, jnp.add, 0.0) + jnp.where(is_fw, t0, 0.0)
    r0, r1 = li0 - b0, li1 - b1
    a0, a1 = total(r0, jnp.maximum), total(r1, jnp.maximum)
    e0, e1 = scan(r0, jnp.maximum, -jnp.inf), scan(r1, jnp.maximum, -jnp.inf)
    cm0 = jnp.where(is_fw, e0, jnp.maximum(e0, a1))
    cm1 = jnp.where(is_fw, jnp.maximum(e1, a0), e1)
    g, rmax = t0 + t1, jnp.maximum(a0, a1)
    for c in range(nc):
        cs = slice(c * GATE_CH, (c + 1) * GATE_CH)
        for off, v0, v1 in ((ROW_B, b0, b1), (ROW_R, r0, r1), (ROW_CM, cm0, cm1),
                            (ROW_G, g, g), (ROW_RMAX, rmax, rmax)):
            row_ref[c, off:off + GATE_CH, 0:LANES] = v0[cs]
            row_ref[c, off:off + GATE_CH, LANES:MCHUNK] = v1[cs]
        for half, r in ((0, r0), (1, r1)):
            rs = slice(c * MCHUNK + half * LANES, c * MCHUNK + (half + 1) * LANES)
            col_ref[rs, :] = jnp.concatenate([r[cs], zpad], axis=0).T


def _gates(small, b_i, b_f):
    t = small.shape[0]
    nc = t // MCHUNK
    cpb = min(GATE_CHUNKS, nc)
    return pl.pallas_call(
        _gate_kernel,
        grid=(nc // cpb,),
        in_specs=[pl.BlockSpec((cpb * MCHUNK, LANES), lambda i: (i, SMALL_COLS // LANES - 1)),
                  pl.BlockSpec((GATE_CH, 1), lambda i: (0, 0)),
                  pl.BlockSpec((GATE_CH, 1), lambda i: (0, 0))],
        out_specs=[pl.BlockSpec((cpb * MCHUNK, LANES), lambda i: (i, 0)),
                   pl.BlockSpec((cpb, ROW_PACK, MCHUNK), lambda i: (i, 0, 0))],
        out_shape=[jax.ShapeDtypeStruct((t, LANES), F32),
                   jax.ShapeDtypeStruct((nc, ROW_PACK, MCHUNK), F32)],
        compiler_params=_params(("parallel",)),
        name="mlstm_gates",
    )(small, b_i, b_f)


V_EXT = M_V + 16


def _mlstm_kernel(qtf_ref, kf_ref, vtf_ref, rowf_ref, colf_ref,
                  qtb_ref, kb_ref, vtb_ref, rowb_ref, colb_ref,
                  hf_ref, hb_ref, c_scr, m_scr):
    @pl.when(pl.program_id(1) == 0)
    def _():
        c_scr[...] = jnp.zeros(c_scr.shape, F32)
        m_scr[...] = jnp.zeros(m_scr.shape, F32)

    ll = lax.broadcasted_iota(jnp.int32, (MCHUNK, MCHUNK), 0)
    jj = lax.broadcasted_iota(jnp.int32, (MCHUNK, MCHUNK), 1)
    ones_blk = (lax.broadcasted_iota(jnp.int32, (V_EXT - M_V, MCHUNK), 0) == 0).astype(F32)
    inv_scale = float(M_QK) ** 0.5
    m_all = m_scr[...]
    m_next = []
    dirs = ((qtf_ref, kf_ref, vtf_ref, rowf_ref, colf_ref[...], hf_ref, ll <= jj),
            (qtb_ref, kb_ref, vtb_ref, rowb_ref, colb_ref[...], hb_ref, ll >= jj))
    units = [(d, h) for d in range(2) for h in range(M_HEADS)]

    def row(d, off, h):
        r = off + d * M_HEADS + h
        return dirs[d][3][r:r + 1, :]

    st, cq = {}, {}
    for d, h in units:
        qt_ref, k_ref = dirs[d][0], dirs[d][1]
        hs = slice(h * M_QK, (h + 1) * M_QK)
        qt = qt_ref[hs, :]
        st[d, h] = jnp.dot(k_ref[:, hs], qt, preferred_element_type=F32)
        cq[d, h] = jnp.dot(c_scr[d, h].astype(BF16), qt, preferred_element_type=F32)

    for d, h in units:
        _, _, vt_ref, _, col, h_ref, mask = dirs[d]
        ch = d * M_HEADS + h
        hs = slice(h * M_QK, (h + 1) * M_QK)
        m_prev = m_all[ch:ch + 1, :]
        mm = jnp.maximum(m_prev, row(d, ROW_CM, h))
        r_col = col[:, ch:ch + 1]
        pt = (jnp.exp(jnp.where(mask, r_col - mm, -jnp.inf)) * st[d, h]).astype(BF16)
        vt_ext = jnp.concatenate([vt_ref[hs, :], ones_blk.astype(BF16)], axis=0)
        tot = (jnp.dot(vt_ext, pt, preferred_element_type=F32)
               + jnp.exp(m_prev - mm) * cq[d, h])
        floor = jnp.exp(-row(d, ROW_B, h) - mm) * inv_scale
        ht = tot[:M_V] / jnp.maximum(jnp.abs(tot[M_V:M_V + 1]), floor)
        h_ref[:, hs] = ht.T

    for d, h in units:
        k_ref, vt_ref = dirs[d][1], dirs[d][2]
        ch = d * M_HEADS + h
        hs = slice(h * M_QK, (h + 1) * M_QK)
        m_prev = m_all[ch:ch + 1, :]
        mx = jnp.maximum(m_prev, row(d, ROW_RMAX, h))
        w_row = jnp.exp(row(d, ROW_R, h) - mx)
        lhs = jnp.concatenate([(vt_ref[hs, :].astype(F32) * w_row).astype(BF16),
                               (ones_blk * w_row).astype(BF16)], axis=0)
        c_scr[d, h] = (jnp.exp(m_prev - mx)[:, :M_QK] * c_scr[d, h]
                       + jnp.dot(lhs, k_ref[:, hs], preferred_element_type=F32))
        m_next.append(row(d, ROW_G, h) + mx)
    m_scr[...] = jnp.concatenate(m_next, axis=0)


def _mlstm(qvt, big, row, col, b, s):
    nc = s // MCHUNK
    fw = lambda bi, i: bi * nc + i
    bw = lambda bi, i: bi * nc + (nc - 1 - i)

    def specs(blk):
        return [pl.BlockSpec((None, M_WIDTH, MCHUNK), lambda bi, i: (blk(bi, i), 0, 0)),
                pl.BlockSpec((MCHUNK, M_WIDTH), lambda bi, i: (blk(bi, i), 1)),
                pl.BlockSpec((None, M_WIDTH, MCHUNK), lambda bi, i: (blk(bi, i), 1, 0)),
                pl.BlockSpec((None, ROW_PACK, MCHUNK), lambda bi, i: (blk(bi, i), 0, 0)),
                pl.BlockSpec((MCHUNK, LANES), lambda bi, i: (blk(bi, i), 0))]

    out = jax.ShapeDtypeStruct((b * s, M_WIDTH), F32)
    return pl.pallas_call(
        _mlstm_kernel,
        grid=(b, nc),
        in_specs=specs(fw) + specs(bw),
        out_specs=[pl.BlockSpec((MCHUNK, M_WIDTH), lambda bi, i: (fw(bi, i), 0)),
                   pl.BlockSpec((MCHUNK, M_WIDTH), lambda bi, i: (bw(bi, i), 0))],
        out_shape=[out, out],
        scratch_shapes=[pltpu.VMEM((2, M_HEADS, V_EXT, M_QK), F32),
                        pltpu.VMEM((GATE_CH, MCHUNK), F32)],
        compiler_params=_params(("parallel", "arbitrary")),
        name="mlstm",
    )(qvt, big, qvt, row, col, qvt, big, qvt, row, col)


def _merge_kernel(x_ref, ya_ref, hf_ref, hb_ref, om_ref, zm_ref, ga_ref, gb_ref, hn_ref,
                  woa_ref, wob_ref, wout_ref, nf_ref, o_ref, *, final_norm):
    hm = hf_ref[...] + hb_ref[...]
    parts = []
    for h in range(M_HEADS):
        hh = hm[:, h * M_V:(h + 1) * M_V]
        ms = jnp.mean(hh * hh, axis=-1, keepdims=True)
        parts.append(hh * lax.rsqrt(ms + EPS))
    hn = jnp.concatenate(parts, axis=1) * hn_ref[...]
    zm = zm_ref[...].astype(F32)
    yb = hn * jax.nn.sigmoid(om_ref[...].astype(F32)) * (zm * jax.nn.sigmoid(zm))
    pa = jnp.dot(ya_ref[...], woa_ref[...], preferred_element_type=F32)
    pb = jnp.dot(yb.astype(BF16), wob_ref[...], preferred_element_type=F32)
    merged = (jax.nn.sigmoid(ga_ref[...].astype(F32)) * pa
              + jax.nn.sigmoid(gb_ref[...].astype(F32)) * pb)
    out = x_ref[...] + jnp.dot(merged.astype(BF16), wout_ref[...], preferred_element_type=F32)
    if final_norm:
        out = _rms(out, nf_ref[...])
    o_ref[...] = out


def _merge(x2, ya, hf, hb, big, hn, woa, wob, wout, nf, final_norm):
    t = x2.shape[0]
    tm = min(256, t)
    row = lambda c: (lambda i: (i, c))
    const = lambda shape: pl.BlockSpec(shape, lambda i: (0, 0), pipeline_mode=pl.Buffered(1))
    return pl.pallas_call(
        functools.partial(_merge_kernel, final_norm=final_norm),
        grid=(t // tm,),
        in_specs=[pl.BlockSpec((tm, D_MODEL), row(0)),
                  pl.BlockSpec((tm, A_WIDTH), row(0)),
                  pl.BlockSpec((tm, M_WIDTH), row(0)),
                  pl.BlockSpec((tm, M_WIDTH), row(0)),
                  pl.BlockSpec((tm, M_WIDTH), row(2)),
                  pl.BlockSpec((tm, M_WIDTH), row(3)),
                  pl.BlockSpec((tm, D_MODEL), row(2)),
                  pl.BlockSpec((tm, D_MODEL), row(3)),
                  const((1, M_WIDTH)),
                  const((A_WIDTH, D_MODEL)),
                  const((M_WIDTH, D_MODEL)),
                  const((D_MODEL, D_MODEL)),
                  const((1, D_MODEL))],
        out_specs=pl.BlockSpec((tm, D_MODEL), row(0)),
        out_shape=jax.ShapeDtypeStruct((t, D_MODEL), F32),
        compiler_params=_params(("parallel",)),
        name="merge_out",
    )(x2, ya, hf, hb, big, big, big, big, hn, woa, wob, wout, nf)


def _rot_cols(w):
    half = QK_ROPE // 2
    return jnp.concatenate([-w[:, half:], w[:, :half]], axis=1)


def _pack_layer(w_in, w_uq, w_ukv):
    o = 0
    seg = {}
    for name, width in (("c_q", Q_LORA), ("c_kv", KV_LORA), ("k_rope", QK_ROPE), ("z_a", A_WIDTH),
                        ("q_m", M_WIDTH), ("k_m", M_WIDTH), ("v_m", M_WIDTH), ("o_m", M_WIDTH),
                        ("z_m", M_WIDTH), ("gates", 4 * M_HEADS), ("g_a", D_MODEL), ("g_b", D_MODEL)):
        seg[name] = w_in[:, o:o + width]
        o += width
    pad = jnp.zeros((D_MODEL, LANES - 4 * M_HEADS), w_in.dtype)
    w_small = jnp.concatenate([seg["c_q"], seg["c_kv"], seg["k_rope"], _rot_cols(seg["k_rope"]),
                               seg["gates"], pad], axis=1).astype(BF16)
    w_big = jnp.concatenate([seg[n] for n in ("z_a", "k_m", "o_m", "z_m", "g_a", "g_b")],
                            axis=1).astype(BF16)
    w_qvt = jnp.concatenate([seg["q_m"], seg["v_m"]], axis=1).T.astype(BF16)
    wq = w_uq.reshape(Q_LORA, A_HEADS, QK_NOPE + QK_ROPE)
    rope = wq[:, :, QK_NOPE:]
    rot = jnp.concatenate([-rope[:, :, QK_ROPE // 2:], rope[:, :, :QK_ROPE // 2]], axis=2)
    wqt = jnp.concatenate([wq, rot], axis=2).reshape(Q_LORA, A_HEADS * QK_PAD).T.astype(BF16)
    wkv = w_ukv.reshape(KV_LORA, A_HEADS, QK_NOPE + V_HEAD)
    wk = wkv[:, :, :QK_NOPE].reshape(KV_LORA, A_HEADS * QK_NOPE).astype(BF16)
    wvt = wkv[:, :, QK_NOPE:].reshape(KV_LORA, A_HEADS * V_HEAD).T.astype(BF16)
    return w_small, w_big, w_qvt, wqt, wk, wvt


def _rope_table(s):
    inv = ROPE_THETA ** (-jnp.arange(0, QK_ROPE, 2, dtype=F32) / QK_ROPE)
    ang = jnp.arange(s, dtype=F32)[:, None] * inv[None, :]
    cos, sin = jnp.cos(ang), jnp.sin(ang)
    return jnp.concatenate([cos, cos, sin, sin], axis=1)


def _trunk(x, layers, norm_f):
    b, s, _ = x.shape
    x2 = x.reshape(b * s, D_MODEL)
    cs = _rope_table(s)
    cst = cs.T
    q_scale = float((QK_NOPE + QK_ROPE) ** -0.5 * 1.4426950408889634)
    for li, ly in enumerate(layers):
        small = _norm_matmul(x2, ly["norm_in"], ly["w_small"], F32, 1024, SMALL_COLS)
        big = _norm_matmul(x2, ly["norm_in"], ly["w_big"], BF16, 1024, 2048)
        qt, kc, vt = _mla_prep(small, cs, cst, ly["q_a_norm"], ly["kv_a_norm"], ly["wqt"], ly["wk"],
                               ly["wvt"], b, s, q_scale)
        ya = _flash(qt, kc, vt, big, b, s)
        qvt = _norm_matmul(x2, ly["norm_in"], ly["w_qvt"], BF16, 1024, 2048, feature_major=True)
        col, row = _gates(small, ly["b_i"], ly["b_f"])
        hf, hb = _mlstm(qvt, big, row, col, b, s)
        x2 = _merge(x2, ya, hf, hb, big, ly["m_head_norm"], ly["w_oa"], ly["w_ob"], ly["w_out"],
                    norm_f, li == len(layers) - 1)
    return x2.reshape(b, s, D_MODEL)


def kernel(x_prompt, x_sample, norm_in, w_in, b_gates, q_a_norm, w_uq, kv_a_norm, w_ukv, w_oa,
           m_head_norm, w_ob, w_out, norm_f):
    layers = []
    for l in range(w_in.shape[0]):
        w_small, w_big, w_qvt, wqt, wk, wvt = _pack_layer(w_in[l], w_uq[l], w_ukv[l])
        layers.append(dict(
            norm_in=norm_in[l].reshape(1, D_MODEL), w_small=w_small, w_big=w_big, w_qvt=w_qvt,
            wqt=wqt, wk=wk, wvt=wvt,
            q_a_norm=q_a_norm[l].reshape(1, Q_LORA), kv_a_norm=kv_a_norm[l].reshape(1, KV_LORA),
            b_i=b_gates[l, :GATE_CH].reshape(GATE_CH, 1).astype(F32),
            b_f=b_gates[l, GATE_CH:].reshape(GATE_CH, 1).astype(F32),
            m_head_norm=m_head_norm[l].reshape(1, M_WIDTH),
            w_oa=w_oa[l].astype(BF16), w_ob=w_ob[l].astype(BF16), w_out=w_out[l].astype(BF16)))
    nf = norm_f.reshape(1, D_MODEL)
    return (_trunk(x_prompt, layers, nf), _trunk(x_sample, layers, nf))
```

```python
import functools

import jax
import jax.numpy as jnp
from jax import lax
from jax.experimental import pallas as pl
from jax.experimental.pallas import tpu as pltpu

D_MODEL = 2048
A_HEADS = 8
Q_LORA = 512
KV_LORA = 512
QK_NOPE = 128
QK_ROPE = 64
V_HEAD = 128
ROPE_THETA = 10000.0
A_WIDTH = A_HEADS * V_HEAD
M_HEADS = 8
M_QK = 128
M_V = 128
M_WIDTH = M_HEADS * M_V
EPS = 1e-6

LANES = 128
QK_PAD = 256
KV_CHUNK = 512
FLASH_UNROLL = 16
FLASH_TQ = 512
FLASH_AHEAD = 2
AV_EXT = V_HEAD + 16
PROJ_TM = 1024
PROJ_TN = 2048
NORM_ROWS = 256
MERGE_TM = 256
VMEM_LIMIT = 56 * 1024 * 1024

SMALL_COLS = Q_LORA + KV_LORA + 2 * LANES

F32 = jnp.float32
BF16 = jnp.bfloat16


def _rms(xf, g):
    ms = jnp.mean(xf * xf, axis=-1, keepdims=True)
    return xf * lax.rsqrt(ms + EPS) * g


def _params(sem):
    return pltpu.CompilerParams(dimension_semantics=sem, vmem_limit_bytes=VMEM_LIMIT)


_NT = (((1,), (1,)), ((), ()))


def _norm_matmul_kernel(x_ref, g_ref, w_ref, o_ref, h_scr, *, rows, feature_major):
    @pl.when(pl.program_id(1) == 0)
    def _():
        def body(r, c):
            sl = pl.ds(pl.multiple_of(r * rows, rows), rows)
            h_scr[sl, :] = _rms(x_ref[sl, :], g_ref[...]).astype(h_scr.dtype)
            return c
        lax.fori_loop(0, x_ref.shape[0] // rows, body, 0)

    if feature_major:
        out = lax.dot_general(w_ref[...], h_scr[...], _NT, preferred_element_type=F32)
        for c in range(o_ref.shape[0]):
            o_ref[c] = out[:, c * o_ref.shape[2]:(c + 1) * o_ref.shape[2]].astype(o_ref.dtype)
    else:
        out = jnp.dot(h_scr[...], w_ref[...], preferred_element_type=F32)
        o_ref[...] = out.astype(o_ref.dtype)


def _norm_matmul(x2, gain, w, out_dtype, tm, tn, feature_major=False):
    t, k = x2.shape
    n = w.shape[0] if feature_major else w.shape[1]
    tm = min(tm, t)
    rows = min(NORM_ROWS, tm)
    if feature_major:
        w_spec = pl.BlockSpec((tn, k), lambda i, j: (j, 0))
        o_spec = pl.BlockSpec((tm // MCHUNK, tn, MCHUNK), lambda i, j: (i, j, 0))
        o_shape = (t // MCHUNK, n, MCHUNK)
    else:
        w_spec = pl.BlockSpec((k, tn), lambda i, j: (0, j))
        o_spec = pl.BlockSpec((tm, tn), lambda i, j: (i, j))
        o_shape = (t, n)
    return pl.pallas_call(
        functools.partial(_norm_matmul_kernel, rows=rows, feature_major=feature_major),
        grid=(t // tm, n // tn),
        in_specs=[pl.BlockSpec((tm, k), lambda i, j: (i, 0)),
                  pl.BlockSpec((1, k), lambda i, j: (0, 0)),
                  w_spec],
        out_specs=o_spec,
        out_shape=jax.ShapeDtypeStruct(o_shape, out_dtype),
        scratch_shapes=[pltpu.VMEM((tm, k), BF16)],
        compiler_params=_params(("parallel", "arbitrary")),
        name="in_proj_t" if feature_major else "in_proj",
    )(x2, gain, w)


def _mla_prep_kernel(cq_ref, ckv_ref, rest_ref, cs_ref, cst_ref, gq_ref, gkv_ref,
                     wqt_ref, wk_ref, wvt_ref, qt_ref, k_ref, vt_ref, *, q_scale):
    hq = _rms(cq_ref[...], gq_ref[...]).astype(BF16)
    hkv = _rms(ckv_ref[...], gkv_ref[...]).astype(BF16)
    qt = lax.dot_general(wqt_ref[...], hq, _NT, preferred_element_type=F32)
    vt = lax.dot_general(wvt_ref[...], hkv, _NT, preferred_element_type=F32)
    kn = jnp.dot(hkv, wk_ref[...], preferred_element_type=F32)
    t = rest_ref[...] * cs_ref[...]
    lane = lax.broadcasted_iota(jnp.int32, t.shape, 1)
    k_r = jnp.where(lane < QK_ROPE, t + pltpu.roll(t, QK_ROPE, 1), 0.0)
    cst = cst_ref[...]
    pad = jnp.zeros((QK_PAD - QK_NOPE - QK_ROPE, qt.shape[1]), F32)
    ones_blk = (lax.broadcasted_iota(jnp.int32, (AV_EXT - V_HEAD, qt.shape[1]), 0) == 0).astype(BF16)
    for h in range(A_HEADS):
        qh = qt[h * QK_PAD:(h + 1) * QK_PAD]
        tq = qh[QK_NOPE:] * cst
        qt_ref[h] = jnp.concatenate(
            [qh[:QK_NOPE] * q_scale, (tq[:QK_ROPE] + tq[QK_ROPE:]) * q_scale, pad], axis=0).astype(BF16)
        k_ref[h] = jnp.concatenate([kn[:, h * QK_NOPE:(h + 1) * QK_NOPE], k_r], axis=1).astype(BF16)
        vt_ref[h] = jnp.concatenate([vt[h * V_HEAD:(h + 1) * V_HEAD].astype(BF16), ones_blk], axis=0)


def _mla_prep(small, cs, cst, gq, gkv, wqt, wk, wvt, b, s, q_scale):
    tm = min(KV_CHUNK, s // 2)
    nb = s // tm
    full = lambda bi, i: (0, 0)
    return pl.pallas_call(
        functools.partial(_mla_prep_kernel, q_scale=q_scale),
        grid=(b, nb),
        in_specs=[pl.BlockSpec((tm, Q_LORA), lambda bi, i: (bi * nb + i, 0)),
                  pl.BlockSpec((tm, KV_LORA), lambda bi, i: (bi * nb + i, 1)),
                  pl.BlockSpec((tm, LANES), lambda bi, i: (bi * nb + i, (Q_LORA + KV_LORA) // LANES)),
                  pl.BlockSpec((tm, LANES), lambda bi, i: (i, 0)),
                  pl.BlockSpec((LANES, tm), lambda bi, i: (0, i)),
                  pl.BlockSpec((1, Q_LORA), full),
                  pl.BlockSpec((1, KV_LORA), full),
                  pl.BlockSpec((A_HEADS * QK_PAD, Q_LORA), full),
                  pl.BlockSpec((KV_LORA, A_HEADS * QK_NOPE), full),
                  pl.BlockSpec((A_HEADS * V_HEAD, KV_LORA), full)],
        out_specs=[pl.BlockSpec((None, A_HEADS, QK_PAD, tm), lambda bi, i: (bi, 0, 0, i)),
                   pl.BlockSpec((None, A_HEADS, tm, QK_PAD), lambda bi, i: (bi, 0, i, 0)),
                   pl.BlockSpec((None, A_HEADS, None, AV_EXT, tm), lambda bi, i: (bi, 0, i, 0, 0))],
        out_shape=[jax.ShapeDtypeStruct((b, A_HEADS, QK_PAD, s), BF16),
                   jax.ShapeDtypeStruct((b, A_HEADS, s, QK_PAD), BF16),
                   jax.ShapeDtypeStruct((b, A_HEADS, nb, AV_EXT, tm), BF16)],
        compiler_params=_params(("parallel", "parallel")),
        name="mla_prep",
    )(small, small, small, cs, cst, gq, gkv, wqt, wk, wvt)


def _flash_plan(nkb):
    unroll = next(u for u in (FLASH_UNROLL, 8, 2) if nkb % u == 0)
    ahead = FLASH_AHEAD if (nkb % (2 * FLASH_AHEAD) == 0 and unroll % (2 * FLASH_AHEAD) == 0) else 1
    return unroll, ahead


def _flash_kernel(qt_ref, qtn_ref, k_ref, vt_ref, z_ref, o_ref, q2_scr, s_scr, cmax_scr, acc_scr):
    tq = qt_ref.shape[1]
    nkb = vt_ref.shape[0]
    tk = k_ref.shape[0] // nkb
    unroll, ahead = _flash_plan(nkb)
    slots = s_scr.shape[0]
    acc_scr[...] = jnp.zeros(acc_scr.shape, F32)
    q2_scr[0] = qt_ref[...]
    q2_scr[1] = qtn_ref[...]

    def scores(kb, qsel, slot):
        start = kb * tk if isinstance(kb, int) else pl.multiple_of(kb * tk, tk)
        st = jnp.dot(k_ref[pl.ds(start, tk), :], q2_scr[qsel], preferred_element_type=F32)
        s_scr[slot] = st
        return jnp.max(st, axis=0, keepdims=True)

    @pl.when(pl.program_id(2) == 0)
    def _():
        for a in range(ahead):
            cmax_scr[a] = scores(a, 0, a)

    def softmax_values(kb, slot, cmax, m_prev):
        m_new = jnp.maximum(m_prev, cmax)
        pt = jnp.exp2((s_scr[slot] - m_new).astype(BF16))
        acc_scr[...] = (jnp.exp2(m_prev - m_new) * acc_scr[...]
                        + jnp.dot(vt_ref[kb], pt, preferred_element_type=F32))
        return m_new

    def trip(j, carry):
        m, cmax = carry[0], list(carry[1:])
        for u in range(unroll):
            kb = unroll * j + u
            nxt = kb + ahead
            if u >= unroll - ahead:
                wrap = nxt >= nkb
                c_new = scores(jnp.where(wrap, nxt - nkb, nxt), jnp.where(wrap, 1, 0),
                               (u + ahead) % slots)
            else:
                c_new = scores(nxt, 0, (u + ahead) % slots)
            m = softmax_values(kb, u % slots, cmax[0], m)
            cmax = cmax[1:] + [c_new]
        return (m, *cmax)

    init = (jnp.full((1, tq), -jnp.inf, F32), *[cmax_scr[a] for a in range(ahead)])
    out = lax.fori_loop(0, nkb // unroll, trip, init)
    for a in range(ahead):
        cmax_scr[a] = out[1 + a]
    z = z_ref[...].astype(F32)
    acc = acc_scr[...]
    o = (acc[:V_HEAD] / acc[V_HEAD:V_HEAD + 1]).T
    o_ref[...] = (o * (z * jax.nn.sigmoid(z))).astype(o_ref.dtype)


def _flash(qt, kc, vt, big, b, s):
    tq = min(FLASH_TQ, s)
    nq = s // tq
    nkb, tk = vt.shape[2], vt.shape[4]
    _, ahead = _flash_plan(nkb)
    return pl.pallas_call(
        _flash_kernel,
        grid=(b, A_HEADS, nq),
        in_specs=[pl.BlockSpec((None, None, QK_PAD, tq), lambda bi, h, i: (bi, h, 0, i)),
                  pl.BlockSpec((None, None, QK_PAD, tq),
                               lambda bi, h, i: (bi, h, 0, jnp.minimum(i + 1, nq - 1))),
                  pl.BlockSpec((None, None, s, QK_PAD), lambda bi, h, i: (bi, h, 0, 0)),
                  pl.BlockSpec((None, None, nkb, AV_EXT, tk), lambda bi, h, i: (bi, h, 0, 0, 0)),
                  pl.BlockSpec((tq, V_HEAD), lambda bi, h, i: (bi * nq + i, h))],
        out_specs=pl.BlockSpec((tq, V_HEAD), lambda bi, h, i: (bi * nq + i, h)),
        out_shape=jax.ShapeDtypeStruct((b * s, A_WIDTH), BF16),
        scratch_shapes=[pltpu.VMEM((2, QK_PAD, tq), BF16),
                        pltpu.VMEM((2 * ahead, tk, tq), F32),
                        pltpu.VMEM((ahead, 1, tq), F32),
                        pltpu.VMEM((AV_EXT, tq), F32)],
        compiler_params=_params(("parallel", "parallel", "arbitrary")),
        name="mla_flash",
    )(qt, qt, kc, vt, big)


GATE_CH = 2 * M_HEADS
MCHUNK = 2 * LANES
GATE_CHUNKS = 16
ROW_B, ROW_R, ROW_CM, ROW_G, ROW_RMAX = (i * GATE_CH for i in range(5))
ROW_PACK = 5 * GATE_CH


def _gate_kernel(x_ref, bi_ref, bf_ref, col_ref, row_ref):
    nc = row_ref.shape[0]
    shape = (nc * GATE_CH, LANES)
    pos = lax.broadcasted_iota(jnp.int32, shape, 1)
    is_fw = lax.broadcasted_iota(jnp.int32, shape, 0) % GATE_CH < M_HEADS
    shifts = (1, 2, 4, 8, 16, 32, 64)

    def scan(y, op, ident):
        for sft in shifts:
            y_f = jnp.where(pos >= sft, pltpu.roll(y, sft, 1), ident)
            y_b = jnp.where(pos < LANES - sft, pltpu.roll(y, LANES - sft, 1), ident)
            y = op(y, jnp.where(is_fw, y_f, y_b))
        return y

    def total(y, op):
        for sft in shifts:
            y = op(y, pltpu.roll(y, sft, 1))
        return y

    zpad = jnp.zeros((LANES - GATE_CH, LANES), F32)
    log_i, log_f = ([], []), ([], [])
    for c in range(nc):
        for half in range(2):
            rs = slice(c * MCHUNK + half * LANES, c * MCHUNK + (half + 1) * LANES)
            xt = x_ref[rs, :].T
            log_i[half].append(xt[0:GATE_CH] + bi_ref[...])
            log_f[half].append(jax.nn.log_sigmoid(xt[GATE_CH:2 * GATE_CH] + bf_ref[...]))
    li0, li1 = (jnp.concatenate(v, axis=0) for v in log_i)
    lf0, lf1 = (jnp.concatenate(v, axis=0) for v in log_f)
    t0, t1 = total(lf0, jnp.add), total(lf1, jnp.add)
    b0 = scan(lf0, jnp.add, 0.0) + jnp.where(is_fw, 0.0, t1)
    b1 = scan(lf1, jnp.add, 0.0) + jnp.where(is_fw, t0, 0.0)
    r0, r1 = li0 - b0, li1 - b1
    a0, a1 = total(r0, jnp.maximum), total(r1, jnp.maximum)
    e0, e1 = scan(r0, jnp.maximum, -jnp.inf), scan(r1, jnp.maximum, -jnp.inf)
    cm0 = jnp.where(is_fw, e0, jnp.maximum(e0, a1))
    cm1 = jnp.where(is_fw, jnp.maximum(e1, a0), e1)
    g, rmax = t0 + t1, jnp.maximum(a0, a1)
    for c in range(nc):
        cs = slice(c * GATE_CH, (c + 1) * GATE_CH)
        for off, v0, v1 in ((ROW_B, b0, b1), (ROW_R, r0, r1), (ROW_CM, cm0, cm1),
                            (ROW_G, g, g), (ROW_RMAX, rmax, rmax)):
            row_ref[c, off:off + GATE_CH, 0:LANES] = v0[cs]
            row_ref[c, off:off + GATE_CH, LANES:MCHUNK] = v1[cs]
        for half, r in ((0, r0), (1, r1)):
            rs = slice(c * MCHUNK + half * LANES, c * MCHUNK + (half + 1) * LANES)
            col_ref[rs, :] = jnp.concatenate([r[cs], zpad], axis=0).T


def _gates(small, b_i, b_f):
    t = small.shape[0]
    nc = t // MCHUNK
    cpb = min(GATE_CHUNKS, nc)
    return pl.pallas_call(
        _gate_kernel,
        grid=(nc // cpb,),
        in_specs=[pl.BlockSpec((cpb * MCHUNK, LANES), lambda i: (i, SMALL_COLS // LANES - 1)),
                  pl.BlockSpec((GATE_CH, 1), lambda i: (0, 0)),
                  pl.BlockSpec((GATE_CH, 1), lambda i: (0, 0))],
        out_specs=[pl.BlockSpec((cpb * MCHUNK, LANES), lambda i: (i, 0)),
                   pl.BlockSpec((cpb, ROW_PACK, MCHUNK), lambda i: (i, 0, 0))],
        out_shape=[jax.ShapeDtypeStruct((t, LANES), F32),
                   jax.ShapeDtypeStruct((nc, ROW_PACK, MCHUNK), F32)],
        compiler_params=_params(("parallel",)),
        name="mlstm_gates",
    )(small, b_i, b_f)


V_EXT = M_V + 16


def _mlstm_kernel(qtf_ref, kf_ref, vtf_ref, rowf_ref, colf_ref,
                  qtb_ref, kb_ref, vtb_ref, rowb_ref, colb_ref,
                  hf_ref, hb_ref, c_scr, m_scr):
    @pl.when(pl.program_id(1) == 0)
    def _():
        c_scr[...] = jnp.zeros(c_scr.shape, F32)
        m_scr[...] = jnp.zeros(m_scr.shape, F32)

    ll = lax.broadcasted_iota(jnp.int32, (MCHUNK, MCHUNK), 0)
    jj = lax.broadcasted_iota(jnp.int32, (MCHUNK, MCHUNK), 1)
    ones_blk = (lax.broadcasted_iota(jnp.int32, (V_EXT - M_V, MCHUNK), 0) == 0).astype(F32)
    inv_scale = float(M_QK) ** 0.5
    m_all = m_scr[...]
    m_next = []
    dirs = ((qtf_ref, kf_ref, vtf_ref, rowf_ref, colf_ref[...], hf_ref, ll <= jj),
            (qtb_ref, kb_ref, vtb_ref, rowb_ref, colb_ref[...], hb_ref, ll >= jj))
    units = [(d, h) for d in range(2) for h in range(M_HEADS)]

    def row(d, off, h):
        r = off + d * M_HEADS + h
        return dirs[d][3][r:r + 1, :]

    st, cq = {}, {}
    for d, h in units:
        qt_ref, k_ref = dirs[d][0], dirs[d][1]
        hs = slice(h * M_QK, (h + 1) * M_QK)
        qt = qt_ref[hs, :]
        st[d, h] = jnp.dot(k_ref[:, hs], qt, preferred_element_type=F32)
        cq[d, h] = jnp.dot(c_scr[d, h].astype(BF16), qt, preferred_element_type=F32)

    for d, h in units:
        _, _, vt_ref, _, col, h_ref, mask = dirs[d]
        ch = d * M_HEADS + h
        hs = slice(h * M_QK, (h + 1) * M_QK)
        m_prev = m_all[ch:ch + 1, :]
        mm = jnp.maximum(m_prev, row(d, ROW_CM, h))
        r_col = col[:, ch:ch + 1]
        pt = (jnp.exp(jnp.where(mask, r_col - mm, -jnp.inf)) * st[d, h]).astype(BF16)
        vt_ext = jnp.concatenate([vt_ref[hs, :], ones_blk.astype(BF16)], axis=0)
        tot = (jnp.dot(vt_ext, pt, preferred_element_type=F32)
               + jnp.exp(m_prev - mm) * cq[d, h])
        floor = jnp.exp(-row(d, ROW_B, h) - mm) * inv_scale
        ht = tot[:M_V] / jnp.maximum(jnp.abs(tot[M_V:M_V + 1]), floor)
        h_ref[:, hs] = ht.T

    for d, h in units:
        k_ref, vt_ref = dirs[d][1], dirs[d][2]
        ch = d * M_HEADS + h
        hs = slice(h * M_QK, (h + 1) * M_QK)
        m_prev = m_all[ch:ch + 1, :]
        mx = jnp.maximum(m_prev, row(d, ROW_RMAX, h))
        w_row = jnp.exp(row(d, ROW_R, h) - mx)
        lhs = jnp.concatenate([(vt_ref[hs, :].astype(F32) * w_row).astype(BF16),
                               (ones_blk * w_row).astype(BF16)], axis=0)
        c_scr[d, h] = (jnp.exp(m_prev - mx)[:, :M_QK] * c_scr[d, h]
                       + jnp.dot(lhs, k_ref[:, hs], preferred_element_type=F32))
        m_next.append(row(d, ROW_G, h) + mx)
    m_scr[...] = jnp.concatenate(m_next, axis=0)


def _mlstm(qvt, big, row, col, b, s):
    nc = s // MCHUNK
    fw = lambda bi, i: bi * nc + i
    bw = lambda bi, i: bi * nc + (nc - 1 - i)

    def specs(blk):
        return [pl.BlockSpec((None, M_WIDTH, MCHUNK), lambda bi, i: (blk(bi, i), 0, 0)),
                pl.BlockSpec((MCHUNK, M_WIDTH), lambda bi, i: (blk(bi, i), 1)),
                pl.BlockSpec((None, M_WIDTH, MCHUNK), lambda bi, i: (blk(bi, i), 1, 0)),
                pl.BlockSpec((None, ROW_PACK, MCHUNK), lambda bi, i: (blk(bi, i), 0, 0)),
                pl.BlockSpec((MCHUNK, LANES), lambda bi, i: (blk(bi, i), 0))]

    out = jax.ShapeDtypeStruct((b * s, M_WIDTH), F32)
    return pl.pallas_call(
        _mlstm_kernel,
        grid=(b, nc),
        in_specs=specs(fw) + specs(bw),
        out_specs=[pl.BlockSpec((MCHUNK, M_WIDTH), lambda bi, i: (fw(bi, i), 0)),
                   pl.BlockSpec((MCHUNK, M_WIDTH), lambda bi, i: (bw(bi, i), 0))],
        out_shape=[out, out],
        scratch_shapes=[pltpu.VMEM((2, M_HEADS, V_EXT, M_QK), F32),
                        pltpu.VMEM((GATE_CH, MCHUNK), F32)],
        compiler_params=_params(("parallel", "arbitrary")),
        name="mlstm",
    )(qvt, big, qvt, row, col, qvt, big, qvt, row, col)


def _merge_kernel(x_ref, ya_ref, hf_ref, hb_ref, om_ref, zm_ref, ga_ref, gb_ref, hn_ref,
                  woa_ref, wob_ref, wout_ref, nf_ref, o_ref, *, final_norm):
    hm = hf_ref[...] + hb_ref[...]
    parts = []
    for h in range(M_HEADS):
        hh = hm[:, h * M_V:(h + 1) * M_V]
        ms = jnp.mean(hh * hh, axis=-1, keepdims=True)
        parts.append(hh * lax.rsqrt(ms + EPS))
    hn = jnp.concatenate(parts, axis=1) * hn_ref[...]
    zm = zm_ref[...].astype(F32)
    yb = hn * jax.nn.sigmoid(om_ref[...].astype(F32)) * (zm * jax.nn.sigmoid(zm))
    pa = jnp.dot(ya_ref[...], woa_ref[...], preferred_element_type=F32)
    pb = jnp.dot(yb.astype(BF16), wob_ref[...], preferred_element_type=F32)
    merged = (jax.nn.sigmoid(ga_ref[...].astype(F32)) * pa
              + jax.nn.sigmoid(gb_ref[...].astype(F32)) * pb)
    out = x_ref[...] + jnp.dot(merged.astype(BF16), wout_ref[...], preferred_element_type=F32)
    if final_norm:
        out = _rms(out, nf_ref[...])
    o_ref[...] = out


def _merge(x2, ya, hf, hb, big, hn, woa, wob, wout, nf, final_norm):
    t = x2.shape[0]
    tm = min(MERGE_TM, t)
    row = lambda c: (lambda i: (i, c))
    const = lambda shape: pl.BlockSpec(shape, lambda i: (0, 0), pipeline_mode=pl.Buffered(1))
    return pl.pallas_call(
        functools.partial(_merge_kernel, final_norm=final_norm),
        grid=(t // tm,),
        in_specs=[pl.BlockSpec((tm, D_MODEL), row(0)),
                  pl.BlockSpec((tm, A_WIDTH), row(0)),
                  pl.BlockSpec((tm, M_WIDTH), row(0)),
                  pl.BlockSpec((tm, M_WIDTH), row(0)),
                  pl.BlockSpec((tm, M_WIDTH), row(2)),
                  pl.BlockSpec((tm, M_WIDTH), row(3)),
                  pl.BlockSpec((tm, D_MODEL), row(2)),
                  pl.BlockSpec((tm, D_MODEL), row(3)),
                  const((1, M_WIDTH)),
                  const((A_WIDTH, D_MODEL)),
                  const((M_WIDTH, D_MODEL)),
                  const((D_MODEL, D_MODEL)),
                  const((1, D_MODEL))],
        out_specs=pl.BlockSpec((tm, D_MODEL), row(0)),
        out_shape=jax.ShapeDtypeStruct((t, D_MODEL), F32),
        compiler_params=_params(("parallel",)),
        name="merge_out",
    )(x2, ya, hf, hb, big, big, big, big, hn, woa, wob, wout, nf)


def _rot_cols(w):
    half = QK_ROPE // 2
    return jnp.concatenate([-w[:, half:], w[:, :half]], axis=1)


def _pack_layer(w_in, w_uq, w_ukv):
    o = 0
    seg = {}
    for name, width in (("c_q", Q_LORA), ("c_kv", KV_LORA), ("k_rope", QK_ROPE), ("z_a", A_WIDTH),
                        ("q_m", M_WIDTH), ("k_m", M_WIDTH), ("v_m", M_WIDTH), ("o_m", M_WIDTH),
                        ("z_m", M_WIDTH), ("gates", 4 * M_HEADS), ("g_a", D_MODEL), ("g_b", D_MODEL)):
        seg[name] = w_in[:, o:o + width]
        o += width
    pad = jnp.zeros((D_MODEL, LANES - 4 * M_HEADS), w_in.dtype)
    w_small = jnp.concatenate([seg["c_q"], seg["c_kv"], seg["k_rope"], _rot_cols(seg["k_rope"]),
                               seg["gates"], pad], axis=1).astype(BF16)
    w_big = jnp.concatenate([seg[n] for n in ("z_a", "k_m", "o_m", "z_m", "g_a", "g_b")],
                            axis=1).astype(BF16)
    w_qvt = jnp.concatenate([seg["q_m"], seg["v_m"]], axis=1).T.astype(BF16)
    wq = w_uq.reshape(Q_LORA, A_HEADS, QK_NOPE + QK_ROPE)
    rope = wq[:, :, QK_NOPE:]
    rot = jnp.concatenate([-rope[:, :, QK_ROPE // 2:], rope[:, :, :QK_ROPE // 2]], axis=2)
    wqt = jnp.concatenate([wq, rot], axis=2).reshape(Q_LORA, A_HEADS * QK_PAD).T.astype(BF16)
    wkv = w_ukv.reshape(KV_LORA, A_HEADS, QK_NOPE + V_HEAD)
    wk = wkv[:, :, :QK_NOPE].reshape(KV_LORA, A_HEADS * QK_NOPE).astype(BF16)
    wvt = wkv[:, :, QK_NOPE:].reshape(KV_LORA, A_HEADS * V_HEAD).T.astype(BF16)
    return w_small, w_big, w_qvt, wqt, wk, wvt


def _rope_table(s):
    inv = ROPE_THETA ** (-jnp.arange(0, QK_ROPE, 2, dtype=F32) / QK_ROPE)
    ang = jnp.arange(s, dtype=F32)[:, None] * inv[None, :]
    cos, sin = jnp.cos(ang), jnp.sin(ang)
    return jnp.concatenate([cos, cos, sin, sin], axis=1)


def _trunk(x, layers, norm_f):
    b, s, d = x.shape
    assert d == D_MODEL and s % MCHUNK == 0 and s % (2 * LANES) == 0, x.shape
    x2 = x.reshape(b * s, D_MODEL)
    cs = _rope_table(s)
    cst = cs.T
    q_scale = float((QK_NOPE + QK_ROPE) ** -0.5 * 1.4426950408889634)
    for li, ly in enumerate(layers):
        small = _norm_matmul(x2, ly["norm_in"], ly["w_small"], F32, PROJ_TM, SMALL_COLS)
        big = _norm_matmul(x2, ly["norm_in"], ly["w_big"], BF16, PROJ_TM, PROJ_TN)
        qt, kc, vt = _mla_prep(small, cs, cst, ly["q_a_norm"], ly["kv_a_norm"], ly["wqt"], ly["wk"],
                               ly["wvt"], b, s, q_scale)
        ya = _flash(qt, kc, vt, big, b, s)
        qvt = _norm_matmul(x2, ly["norm_in"], ly["w_qvt"], BF16, PROJ_TM, PROJ_TN, feature_major=True)
        col, row = _gates(small, ly["b_i"], ly["b_f"])
        hf, hb = _mlstm(qvt, big, row, col, b, s)
        x2 = _merge(x2, ya, hf, hb, big, ly["m_head_norm"], ly["w_oa"], ly["w_ob"], ly["w_out"],
                    norm_f, li == len(layers) - 1)
    return x2.reshape(b, s, D_MODEL)


def kernel(x_prompt, x_sample, norm_in, w_in, b_gates, q_a_norm, w_uq, kv_a_norm, w_ukv, w_oa,
           m_head_norm, w_ob, w_out, norm_f):
    layers = []
    for l in range(w_in.shape[0]):
        w_small, w_big, w_qvt, wqt, wk, wvt = _pack_layer(w_in[l], w_uq[l], w_ukv[l])
        layers.append(dict(
            norm_in=norm_in[l].reshape(1, D_MODEL), w_small=w_small, w_big=w_big, w_qvt=w_qvt,
            wqt=wqt, wk=wk, wvt=wvt,
            q_a_norm=q_a_norm[l].reshape(1, Q_LORA), kv_a_norm=kv_a_norm[l].reshape(1, KV_LORA),
            b_i=b_gates[l, :GATE_CH].reshape(GATE_CH, 1).astype(F32),
            b_f=b_gates[l, GATE_CH:].reshape(GATE_CH, 1).astype(F32),
            m_head_norm=m_head_norm[l].reshape(1, M_WIDTH),
            w_oa=w_oa[l].astype(BF16), w_ob=w_ob[l].astype(BF16), w_out=w_out[l].astype(BF16)))
    nf = norm_f.reshape(1, D_MODEL)
    return (_trunk(x_prompt, layers, nf), _trunk(x_sample, layers, nf))
```

```python
import functools

import jax
import jax.numpy as jnp
from jax import lax
from jax.experimental import pallas as pl
from jax.experimental.pallas import tpu as pltpu

D_MODEL = 2048
A_HEADS = 8
Q_LORA = 512
KV_LORA = 512
QK_NOPE = 128
QK_ROPE = 64
V_HEAD = 128
ROPE_THETA = 10000.0
A_WIDTH = A_HEADS * V_HEAD
M_HEADS = 8
M_QK = 128
M_V = 128
M_WIDTH = M_HEADS * M_V
EPS = 1e-6

LANES = 128
QK_PAD = 256
KV_CHUNK = 512
FLASH_UNROLL = 8
FLASH_HEADS = 2
FLASH_TQ = 512
FLASH_AHEAD = 2
AV_EXT = V_HEAD + 16
PROJ_TM = 1024
PROJ_TN = 2048
NORM_ROWS = 256
MERGE_TM = 256
VMEM_LIMIT = 56 * 1024 * 1024

SMALL_COLS = Q_LORA + KV_LORA + 2 * LANES

F32 = jnp.float32
BF16 = jnp.bfloat16


def _rms(xf, g):
    ms = jnp.mean(xf * xf, axis=-1, keepdims=True)
    return xf * lax.rsqrt(ms + EPS) * g


def _params(sem):
    return pltpu.CompilerParams(dimension_semantics=sem, vmem_limit_bytes=VMEM_LIMIT)


_NT = (((1,), (1,)), ((), ()))


def _norm_matmul_kernel(x_ref, g_ref, w_ref, o_ref, h_scr, *, rows, feature_major):
    @pl.when(pl.program_id(1) == 0)
    def _():
        def body(r, c):
            sl = pl.ds(pl.multiple_of(r * rows, rows), rows)
            h_scr[sl, :] = _rms(x_ref[sl, :], g_ref[...]).astype(h_scr.dtype)
            return c
        lax.fori_loop(0, x_ref.shape[0] // rows, body, 0)

    if feature_major:
        out = lax.dot_general(w_ref[...], h_scr[...], _NT, preferred_element_type=F32)
        for c in range(o_ref.shape[0]):
            o_ref[c] = out[:, c * o_ref.shape[2]:(c + 1) * o_ref.shape[2]].astype(o_ref.dtype)
    else:
        out = jnp.dot(h_scr[...], w_ref[...], preferred_element_type=F32)
        o_ref[...] = out.astype(o_ref.dtype)


def _norm_matmul(x2, gain, w, out_dtype, tm, tn, feature_major=False):
    t, k = x2.shape
    n = w.shape[0] if feature_major else w.shape[1]
    tm = min(tm, t)
    rows = min(NORM_ROWS, tm)
    if feature_major:
        w_spec = pl.BlockSpec((tn, k), lambda i, j: (j, 0))
        o_spec = pl.BlockSpec((tm // MCHUNK, tn, MCHUNK), lambda i, j: (i, j, 0))
        o_shape = (t // MCHUNK, n, MCHUNK)
    else:
        w_spec = pl.BlockSpec((k, tn), lambda i, j: (0, j))
        o_spec = pl.BlockSpec((tm, tn), lambda i, j: (i, j))
        o_shape = (t, n)
    return pl.pallas_call(
        functools.partial(_norm_matmul_kernel, rows=rows, feature_major=feature_major),
        grid=(t // tm, n // tn),
        in_specs=[pl.BlockSpec((tm, k), lambda i, j: (i, 0)),
                  pl.BlockSpec((1, k), lambda i, j: (0, 0)),
                  w_spec],
        out_specs=o_spec,
        out_shape=jax.ShapeDtypeStruct(o_shape, out_dtype),
        scratch_shapes=[pltpu.VMEM((tm, k), BF16)],
        compiler_params=_params(("parallel", "arbitrary")),
        name="in_proj_t" if feature_major else "in_proj",
    )(x2, gain, w)


def _mla_prep_kernel(cq_ref, ckv_ref, rest_ref, cs_ref, cst_ref, gq_ref, gkv_ref,
                     wqt_ref, wk_ref, wvt_ref, qt_ref, k_ref, vt_ref, *, q_scale):
    hq = _rms(cq_ref[...], gq_ref[...]).astype(BF16)
    hkv = _rms(ckv_ref[...], gkv_ref[...]).astype(BF16)
    qt = lax.dot_general(wqt_ref[...], hq, _NT, preferred_element_type=F32)
    vt = lax.dot_general(wvt_ref[...], hkv, _NT, preferred_element_type=F32)
    kn = jnp.dot(hkv, wk_ref[...], preferred_element_type=F32)
    t = rest_ref[...] * cs_ref[...]
    lane = lax.broadcasted_iota(jnp.int32, t.shape, 1)
    k_r = jnp.where(lane < QK_ROPE, t + pltpu.roll(t, QK_ROPE, 1), 0.0)
    cst = cst_ref[...]
    pad = jnp.zeros((QK_PAD - QK_NOPE - QK_ROPE, qt.shape[1]), F32)
    ones_blk = (lax.broadcasted_iota(jnp.int32, (AV_EXT - V_HEAD, qt.shape[1]), 0) == 0).astype(BF16)
    for h in range(A_HEADS):
        qh = qt[h * QK_PAD:(h + 1) * QK_PAD]
        tq = qh[QK_NOPE:] * cst
        qt_ref[h] = jnp.concatenate(
            [qh[:QK_NOPE] * q_scale, (tq[:QK_ROPE] + tq[QK_ROPE:]) * q_scale, pad], axis=0).astype(BF16)
        k_ref[h] = jnp.concatenate([kn[:, h * QK_NOPE:(h + 1) * QK_NOPE], k_r], axis=1).astype(BF16)
        vt_ref[h] = jnp.concatenate([vt[h * V_HEAD:(h + 1) * V_HEAD].astype(BF16), ones_blk], axis=0)


def _mla_prep(small, cs, cst, gq, gkv, wqt, wk, wvt, b, s, q_scale):
    tm = min(KV_CHUNK, s // 2)
    nb = s // tm
    full = lambda bi, i: (0, 0)
    return pl.pallas_call(
        functools.partial(_mla_prep_kernel, q_scale=q_scale),
        grid=(b, nb),
        in_specs=[pl.BlockSpec((tm, Q_LORA), lambda bi, i: (bi * nb + i, 0)),
                  pl.BlockSpec((tm, KV_LORA), lambda bi, i: (bi * nb + i, 1)),
                  pl.BlockSpec((tm, LANES), lambda bi, i: (bi * nb + i, (Q_LORA + KV_LORA) // LANES)),
                  pl.BlockSpec((tm, LANES), lambda bi, i: (i, 0)),
                  pl.BlockSpec((LANES, tm), lambda bi, i: (0, i)),
                  pl.BlockSpec((1, Q_LORA), full),
                  pl.BlockSpec((1, KV_LORA), full),
                  pl.BlockSpec((A_HEADS * QK_PAD, Q_LORA), full),
                  pl.BlockSpec((KV_LORA, A_HEADS * QK_NOPE), full),
                  pl.BlockSpec((A_HEADS * V_HEAD, KV_LORA), full)],
        out_specs=[pl.BlockSpec((None, A_HEADS, QK_PAD, tm), lambda bi, i: (bi, 0, 0, i)),
                   pl.BlockSpec((None, A_HEADS, tm, QK_PAD), lambda bi, i: (bi, 0, i, 0)),
                   pl.BlockSpec((None, A_HEADS, None, AV_EXT, tm), lambda bi, i: (bi, 0, i, 0, 0))],
        out_shape=[jax.ShapeDtypeStruct((b, A_HEADS, QK_PAD, s), BF16),
                   jax.ShapeDtypeStruct((b, A_HEADS, s, QK_PAD), BF16),
                   jax.ShapeDtypeStruct((b, A_HEADS, nb, AV_EXT, tm), BF16)],
        compiler_params=_params(("parallel", "parallel")),
        name="mla_prep",
    )(small, small, small, cs, cst, gq, gkv, wqt, wk, wvt)


def _flash_plan(nkb):
    unroll = next(u for u in (FLASH_UNROLL, 8, 2) if nkb % u == 0)
    ahead = FLASH_AHEAD if (nkb % (2 * FLASH_AHEAD) == 0 and unroll % (2 * FLASH_AHEAD) == 0) else 1
    return unroll, ahead


def _flash_kernel(qt_ref, qtn_ref, k_ref, vt_ref, z_ref, o_ref, q2_scr, s_scr, cmax_scr, acc_scr):
    nh, tq = qt_ref.shape[0], qt_ref.shape[2]
    nkb = vt_ref.shape[1]
    tk = k_ref.shape[1] // nkb
    unroll, ahead = _flash_plan(nkb)
    slots = s_scr.shape[1]
    acc_scr[...] = jnp.zeros(acc_scr.shape, F32)
    q2_scr[0] = qt_ref[...]
    q2_scr[1] = qtn_ref[...]

    def scores(hh, kb, qsel, slot):
        start = kb * tk if isinstance(kb, int) else pl.multiple_of(kb * tk, tk)
        st = jnp.dot(k_ref[hh, pl.ds(start, tk), :], q2_scr[qsel, hh], preferred_element_type=F32)
        s_scr[hh, slot] = st
        return jnp.max(st, axis=0, keepdims=True)

    @pl.when(pl.program_id(2) == 0)
    def _():
        for hh in range(nh):
            for a in range(ahead):
                cmax_scr[hh, a] = scores(hh, a, 0, a)

    def softmax_values(hh, kb, slot, cmax, m_prev):
        m_new = jnp.maximum(m_prev, cmax)
        pt = jnp.exp2((s_scr[hh, slot] - m_new).astype(BF16))
        acc_scr[hh] = (jnp.exp2(m_prev - m_new) * acc_scr[hh]
                       + jnp.dot(vt_ref[hh, kb], pt, preferred_element_type=F32))
        return m_new

    def trip(j, carry):
        m = list(carry[:nh])
        cmax = [list(carry[nh + hh * ahead:nh + (hh + 1) * ahead]) for hh in range(nh)]
        for u in range(unroll):
            kb = unroll * j + u
            nxt, qsel = kb + ahead, 0
            if u >= unroll - ahead:
                wrap = nxt >= nkb
                nxt, qsel = jnp.where(wrap, nxt - nkb, nxt), jnp.where(wrap, 1, 0)
            for hh in range(nh):
                c_new = scores(hh, nxt, qsel, (u + ahead) % slots)
                m[hh] = softmax_values(hh, kb, u % slots, cmax[hh][0], m[hh])
                cmax[hh] = cmax[hh][1:] + [c_new]
        return (*m, *[c for per_head in cmax for c in per_head])

    init = (*[jnp.full((1, tq), -jnp.inf, F32)] * nh,
            *[cmax_scr[hh, a] for hh in range(nh) for a in range(ahead)])
    out = lax.fori_loop(0, nkb // unroll, trip, init)
    for hh in range(nh):
        for a in range(ahead):
            cmax_scr[hh, a] = out[nh + hh * ahead + a]
        hs = slice(hh * V_HEAD, (hh + 1) * V_HEAD)
        z = z_ref[:, hs].astype(F32)
        acc = acc_scr[hh]
        o = (acc[:V_HEAD] / acc[V_HEAD:V_HEAD + 1]).T
        o_ref[:, hs] = (o * (z * jax.nn.sigmoid(z))).astype(o_ref.dtype)


def _flash(qt, kc, vt, big, b, s):
    tq = min(FLASH_TQ, s)
    nq = s // tq
    nkb, tk = vt.shape[2], vt.shape[4]
    _, ahead = _flash_plan(nkb)
    nh = FLASH_HEADS
    kv_bytes = nh * s * (QK_PAD + AV_EXT) * jnp.dtype(BF16).itemsize
    kv_mode = dict(pipeline_mode=pl.Buffered(1)) if 2 * kv_bytes > VMEM_LIMIT // 2 else {}
    return pl.pallas_call(
        _flash_kernel,
        grid=(b, A_HEADS // nh, nq),
        in_specs=[pl.BlockSpec((None, nh, QK_PAD, tq), lambda bi, h, i: (bi, h, 0, i)),
                  pl.BlockSpec((None, nh, QK_PAD, tq),
                               lambda bi, h, i: (bi, h, 0, jnp.minimum(i + 1, nq - 1))),
                  pl.BlockSpec((None, nh, s, QK_PAD), lambda bi, h, i: (bi, h, 0, 0), **kv_mode),
                  pl.BlockSpec((None, nh, nkb, AV_EXT, tk), lambda bi, h, i: (bi, h, 0, 0, 0),
                               **kv_mode),
                  pl.BlockSpec((tq, nh * V_HEAD), lambda bi, h, i: (bi * nq + i, h))],
        out_specs=pl.BlockSpec((tq, nh * V_HEAD), lambda bi, h, i: (bi * nq + i, h)),
        out_shape=jax.ShapeDtypeStruct((b * s, A_WIDTH), BF16),
        scratch_shapes=[pltpu.VMEM((2, nh, QK_PAD, tq), BF16),
                        pltpu.VMEM((nh, 2 * ahead, tk, tq), F32),
                        pltpu.VMEM((nh, ahead, 1, tq), F32),
                        pltpu.VMEM((nh, AV_EXT, tq), F32)],
        compiler_params=_params(("parallel", "parallel", "arbitrary")),
        name="mla_flash",
    )(qt, qt, kc, vt, big)


GATE_CH = 2 * M_HEADS
MCHUNK = 2 * LANES
GATE_CHUNKS = 16
ROW_B, ROW_R, ROW_CM, ROW_G, ROW_RMAX = (i * GATE_CH for i in range(5))
ROW_PACK = 5 * GATE_CH


def _gate_kernel(x_ref, bi_ref, bf_ref, col_ref, row_ref):
    nc = row_ref.shape[0]
    shape = (nc * GATE_CH, LANES)
    pos = lax.broadcasted_iota(jnp.int32, shape, 1)
    is_fw = lax.broadcasted_iota(jnp.int32, shape, 0) % GATE_CH < M_HEADS
    shifts = (1, 2, 4, 8, 16, 32, 64)

    def scan(y, op, ident):
        for sft in shifts:
            y_f = jnp.where(pos >= sft, pltpu.roll(y, sft, 1), ident)
            y_b = jnp.where(pos < LANES - sft, pltpu.roll(y, LANES - sft, 1), ident)
            y = op(y, jnp.where(is_fw, y_f, y_b))
        return y

    def total(y, op):
        for sft in shifts:
            y = op(y, pltpu.roll(y, sft, 1))
        return y

    zpad = jnp.zeros((LANES - GATE_CH, LANES), F32)
    log_i, log_f = ([], []), ([], [])
    for c in range(nc):
        for half in range(2):
            rs = slice(c * MCHUNK + half * LANES, c * MCHUNK + (half + 1) * LANES)
            xt = x_ref[rs, :].T
            log_i[half].append(xt[0:GATE_CH] + bi_ref[...])
            log_f[half].append(jax.nn.log_sigmoid(xt[GATE_CH:2 * GATE_CH] + bf_ref[...]))
    li0, li1 = (jnp.concatenate(v, axis=0) for v in log_i)
    lf0, lf1 = (jnp.concatenate(v, axis=0) for v in log_f)
    t0, t1 = total(lf0, jnp.add), total(lf1, jnp.add)
    b0 = scan(lf0, jnp.add, 0.0) + jnp.where(is_fw, 0.0, t1)
    b1 = scan(lf1, jnp.add, 0.0) + jnp.where(is_fw, t0, 0.0)
    r0, r1 = li0 - b0, li1 - b1
    a0, a1 = total(r0, jnp.maximum), total(r1, jnp.maximum)
    e0, e1 = scan(r0, jnp.maximum, -jnp.inf), scan(r1, jnp.maximum, -jnp.inf)
    cm0 = jnp.where(is_fw, e0, jnp.maximum(e0, a1))
    cm1 = jnp.where(is_fw, jnp.maximum(e1, a0), e1)
    g, rmax = t0 + t1, jnp.maximum(a0, a1)
    for c in range(nc):
        cs = slice(c * GATE_CH, (c + 1) * GATE_CH)
        for off, v0, v1 in ((ROW_B, b0, b1), (ROW_R, r0, r1), (ROW_CM, cm0, cm1),
                            (ROW_G, g, g), (ROW_RMAX, rmax, rmax)):
            row_ref[c, off:off + GATE_CH, 0:LANES] = v0[cs]
            row_ref[c, off:off + GATE_CH, LANES:MCHUNK] = v1[cs]
        for half, r in ((0, r0), (1, r1)):
            rs = slice(c * MCHUNK + half * LANES, c * MCHUNK + (half + 1) * LANES)
            col_ref[rs, :] = jnp.concatenate([r[cs], zpad], axis=0).T


def _gates(small, b_i, b_f):
    t = small.shape[0]
    nc = t // MCHUNK
    cpb = min(GATE_CHUNKS, nc)
    return pl.pallas_call(
        _gate_kernel,
        grid=(nc // cpb,),
        in_specs=[pl.BlockSpec((cpb * MCHUNK, LANES), lambda i: (i, SMALL_COLS // LANES - 1)),
                  pl.BlockSpec((GATE_CH, 1), lambda i: (0, 0)),
                  pl.BlockSpec((GATE_CH, 1), lambda i: (0, 0))],
        out_specs=[pl.BlockSpec((cpb * MCHUNK, LANES), lambda i: (i, 0)),
                   pl.BlockSpec((cpb, ROW_PACK, MCHUNK), lambda i: (i, 0, 0))],
        out_shape=[jax.ShapeDtypeStruct((t, LANES), F32),
                   jax.ShapeDtypeStruct((nc, ROW_PACK, MCHUNK), F32)],
        compiler_params=_params(("parallel",)),
        name="mlstm_gates",
    )(small, b_i, b_f)


V_EXT = M_V + 16


def _mlstm_kernel(qtf_ref, kf_ref, vtf_ref, rowf_ref, colf_ref,
                  qtb_ref, kb_ref, vtb_ref, rowb_ref, colb_ref,
                  hf_ref, hb_ref, c_scr, m_scr):
    @pl.when(pl.program_id(1) == 0)
    def _():
        c_scr[...] = jnp.zeros(c_scr.shape, F32)
        m_scr[...] = jnp.zeros(m_scr.shape, F32)

    ll = lax.broadcasted_iota(jnp.int32, (MCHUNK, MCHUNK), 0)
    jj = lax.broadcasted_iota(jnp.int32, (MCHUNK, MCHUNK), 1)
    ones_blk = (lax.broadcasted_iota(jnp.int32, (V_EXT - M_V, MCHUNK), 0) == 0).astype(F32)
    inv_scale = float(M_QK) ** 0.5
    m_all = m_scr[...]
    m_next = []
    dirs = ((qtf_ref, kf_ref, vtf_ref, rowf_ref, colf_ref[...], hf_ref, ll <= jj),
            (qtb_ref, kb_ref, vtb_ref, rowb_ref, colb_ref[...], hb_ref, ll >= jj))
    units = [(d, h) for d in range(2) for h in range(M_HEADS)]

    def row(d, off, h):
        r = off + d * M_HEADS + h
        return dirs[d][3][r:r + 1, :]

    st, cq = {}, {}
    for d, h in units:
        qt_ref, k_ref = dirs[d][0], dirs[d][1]
        hs = slice(h * M_QK, (h + 1) * M_QK)
        qt = qt_ref[hs, :]
        st[d, h] = jnp.dot(k_ref[:, hs], qt, preferred_element_type=F32)
        cq[d, h] = jnp.dot(c_scr[d, h].astype(BF16), qt, preferred_element_type=F32)

    for d, h in units:
        _, _, vt_ref, _, col, h_ref, mask = dirs[d]
        ch = d * M_HEADS + h
        hs = slice(h * M_QK, (h + 1) * M_QK)
        m_prev = m_all[ch:ch + 1, :]
        mm = jnp.maximum(m_prev, row(d, ROW_CM, h))
        r_col = col[:, ch:ch + 1]
        pt = (jnp.exp(jnp.where(mask, r_col - mm, -jnp.inf)) * st[d, h]).astype(BF16)
        vt_ext = jnp.concatenate([vt_ref[hs, :], ones_blk.astype(BF16)], axis=0)
        tot = (jnp.dot(vt_ext, pt, preferred_element_type=F32)
               + jnp.exp(m_prev - mm) * cq[d, h])
        floor = jnp.exp(-row(d, ROW_B, h) - mm) * inv_scale
        ht = tot[:M_V] / jnp.maximum(jnp.abs(tot[M_V:M_V + 1]), floor)
        h_ref[:, hs] = ht.T

    for d, h in units:
        k_ref, vt_ref = dirs[d][1], dirs[d][2]
        ch = d * M_HEADS + h
        hs = slice(h * M_QK, (h + 1) * M_QK)
        m_prev = m_all[ch:ch + 1, :]
        mx = jnp.maximum(m_prev, row(d, ROW_RMAX, h))
        w_row = jnp.exp(row(d, ROW_R, h) - mx)
        lhs = jnp.concatenate([(vt_ref[hs, :].astype(F32) * w_row).astype(BF16),
                               (ones_blk * w_row).astype(BF16)], axis=0)
        c_scr[d, h] = (jnp.exp(m_prev - mx)[:, :M_QK] * c_scr[d, h]
                       + jnp.dot(lhs, k_ref[:, hs], preferred_element_type=F32))
        m_next.append(row(d, ROW_G, h) + mx)
    m_scr[...] = jnp.concatenate(m_next, axis=0)


def _mlstm(qvt, big, row, col, b, s):
    nc = s // MCHUNK
    fw = lambda bi, i: bi * nc + i
    bw = lambda bi, i: bi * nc + (nc - 1 - i)

    def specs(blk):
        return [pl.BlockSpec((None, M_WIDTH, MCHUNK), lambda bi, i: (blk(bi, i), 0, 0)),
                pl.BlockSpec((MCHUNK, M_WIDTH), lambda bi, i: (blk(bi, i), 1)),
                pl.BlockSpec((None, M_WIDTH, MCHUNK), lambda bi, i: (blk(bi, i), 1, 0)),
                pl.BlockSpec((None, ROW_PACK, MCHUNK), lambda bi, i: (blk(bi, i), 0, 0)),
                pl.BlockSpec((MCHUNK, LANES), lambda bi, i: (blk(bi, i), 0))]

    out = jax.ShapeDtypeStruct((b * s, M_WIDTH), F32)
    return pl.pallas_call(
        _mlstm_kernel,
        grid=(b, nc),
        in_specs=specs(fw) + specs(bw),
        out_specs=[pl.BlockSpec((MCHUNK, M_WIDTH), lambda bi, i: (fw(bi, i), 0)),
                   pl.BlockSpec((MCHUNK, M_WIDTH), lambda bi, i: (bw(bi, i), 0))],
        out_shape=[out, out],
        scratch_shapes=[pltpu.VMEM((2, M_HEADS, V_EXT, M_QK), F32),
                        pltpu.VMEM((GATE_CH, MCHUNK), F32)],
        compiler_params=_params(("parallel", "arbitrary")),
        name="mlstm",
    )(qvt, big, qvt, row, col, qvt, big, qvt, row, col)


def _merge_kernel(x_ref, ya_ref, hf_ref, hb_ref, om_ref, zm_ref, ga_ref, gb_ref, hn_ref,
                  woa_ref, wob_ref, wout_ref, nf_ref, o_ref, *, final_norm):
    hm = hf_ref[...] + hb_ref[...]
    parts = []
    for h in range(M_HEADS):
        hh = hm[:, h * M_V:(h + 1) * M_V]
        ms = jnp.mean(hh * hh, axis=-1, keepdims=True)
        parts.append(hh * lax.rsqrt(ms + EPS))
    hn = jnp.concatenate(parts, axis=1) * hn_ref[...]
    zm = zm_ref[...].astype(F32)
    yb = hn * jax.nn.sigmoid(om_ref[...].astype(F32)) * (zm * jax.nn.sigmoid(zm))
    pa = jnp.dot(ya_ref[...], woa_ref[...], preferred_element_type=F32)
    pb = jnp.dot(yb.astype(BF16), wob_ref[...], preferred_element_type=F32)
    merged = (jax.nn.sigmoid(ga_ref[...].astype(F32)) * pa
              + jax.nn.sigmoid(gb_ref[...].astype(F32)) * pb)
    out = x_ref[...] + jnp.dot(merged.astype(BF16), wout_ref[...], preferred_element_type=F32)
    if final_norm:
        out = _rms(out, nf_ref[...])
    o_ref[...] = out


def _merge(x2, ya, hf, hb, big, hn, woa, wob, wout, nf, final_norm):
    t = x2.shape[0]
    tm = min(MERGE_TM, t)
    row = lambda c: (lambda i: (i, c))
    const = lambda shape: pl.BlockSpec(shape, lambda i: (0, 0), pipeline_mode=pl.Buffered(1))
    return pl.pallas_call(
        functools.partial(_merge_kernel, final_norm=final_norm),
        grid=(t // tm,),
        in_specs=[pl.BlockSpec((tm, D_MODEL), row(0)),
                  pl.BlockSpec((tm, A_WIDTH), row(0)),
                  pl.BlockSpec((tm, M_WIDTH), row(0)),
                  pl.BlockSpec((tm, M_WIDTH), row(0)),
                  pl.BlockSpec((tm, M_WIDTH), row(2)),
                  pl.BlockSpec((tm, M_WIDTH), row(3)),
                  pl.BlockSpec((tm, D_MODEL), row(2)),
                  pl.BlockSpec((tm, D_MODEL), row(3)),
                  const((1, M_WIDTH)),
                  const((A_WIDTH, D_MODEL)),
                  const((M_WIDTH, D_MODEL)),
                  const((D_MODEL, D_MODEL)),
                  const((1, D_MODEL))],
        out_specs=pl.BlockSpec((tm, D_MODEL), row(0)),
        out_shape=jax.ShapeDtypeStruct((t, D_MODEL), F32),
        compiler_params=_params(("parallel",)),
        name="merge_out",
    )(x2, ya, hf, hb, big, big, big, big, hn, woa, wob, wout, nf)


def _rot_cols(w):
    half = QK_ROPE // 2
    return jnp.concatenate([-w[:, half:], w[:, :half]], axis=1)


def _pack_layer(w_in, w_uq, w_ukv):
    o = 0
    seg = {}
    for name, width in (("c_q", Q_LORA), ("c_kv", KV_LORA), ("k_rope", QK_ROPE), ("z_a", A_WIDTH),
                        ("q_m", M_WIDTH), ("k_m", M_WIDTH), ("v_m", M_WIDTH), ("o_m", M_WIDTH),
                        ("z_m", M_WIDTH), ("gates", 4 * M_HEADS), ("g_a", D_MODEL), ("g_b", D_MODEL)):
        seg[name] = w_in[:, o:o + width]
        o += width
    pad = jnp.zeros((D_MODEL, LANES - 4 * M_HEADS), w_in.dtype)
    w_small = jnp.concatenate([seg["c_q"], seg["c_kv"], seg["k_rope"], _rot_cols(seg["k_rope"]),
                               seg["gates"], pad], axis=1).astype(BF16)
    w_big = jnp.concatenate([seg[n] for n in ("z_a", "k_m", "o_m", "z_m", "g_a", "g_b")],
                            axis=1).astype(BF16)
    w_qvt = jnp.concatenate([seg["q_m"], seg["v_m"]], axis=1).T.astype(BF16)
    wq = w_uq.reshape(Q_LORA, A_HEADS, QK_NOPE + QK_ROPE)
    rope = wq[:, :, QK_NOPE:]
    rot = jnp.concatenate([-rope[:, :, QK_ROPE // 2:], rope[:, :, :QK_ROPE // 2]], axis=2)
    wqt = jnp.concatenate([wq, rot], axis=2).reshape(Q_LORA, A_HEADS * QK_PAD).T.astype(BF16)
    wkv = w_ukv.reshape(KV_LORA, A_HEADS, QK_NOPE + V_HEAD)
    wk = wkv[:, :, :QK_NOPE].reshape(KV_LORA, A_HEADS * QK_NOPE).astype(BF16)
    wvt = wkv[:, :, QK_NOPE:].reshape(KV_LORA, A_HEADS * V_HEAD).T.astype(BF16)
    return w_small, w_big, w_qvt, wqt, wk, wvt


def _rope_table(s):
    inv = ROPE_THETA ** (-jnp.arange(0, QK_ROPE, 2, dtype=F32) / QK_ROPE)
    ang = jnp.arange(s, dtype=F32)[:, None] * inv[None, :]
    cos, sin = jnp.cos(ang), jnp.sin(ang)
    return jnp.concatenate([cos, cos, sin, sin], axis=1)


def _trunk(x, layers, norm_f):
    b, s, d = x.shape
    assert d == D_MODEL and s % MCHUNK == 0 and s % (2 * LANES) == 0, x.shape
    x2 = x.reshape(b * s, D_MODEL)
    cs = _rope_table(s)
    cst = cs.T
    q_scale = float((QK_NOPE + QK_ROPE) ** -0.5 * 1.4426950408889634)
    for li, ly in enumerate(layers):
        small = _norm_matmul(x2, ly["norm_in"], ly["w_small"], F32, PROJ_TM, SMALL_COLS)
        big = _norm_matmul(x2, ly["norm_in"], ly["w_big"], BF16, PROJ_TM, PROJ_TN)
        qt, kc, vt = _mla_prep(small, cs, cst, ly["q_a_norm"], ly["kv_a_norm"], ly["wqt"], ly["wk"],
                               ly["wvt"], b, s, q_scale)
        ya = _flash(qt, kc, vt, big, b, s)
        qvt = _norm_matmul(x2, ly["norm_in"], ly["w_qvt"], BF16, PROJ_TM, PROJ_TN, feature_major=True)
        col, row = _gates(small, ly["b_i"], ly["b_f"])
        hf, hb = _mlstm(qvt, big, row, col, b, s)
        x2 = _merge(x2, ya, hf, hb, big, ly["m_head_norm"], ly["w_oa"], ly["w_ob"], ly["w_out"],
                    norm_f, li == len(layers) - 1)
    return x2.reshape(b, s, D_MODEL)


def kernel(x_prompt, x_sample, norm_in, w_in, b_gates, q_a_norm, w_uq, kv_a_norm, w_ukv, w_oa,
           m_head_norm, w_ob, w_out, norm_f):
    layers = []
    for l in range(w_in.shape[0]):
        w_small, w_big, w_qvt, wqt, wk, wvt = _pack_layer(w_in[l], w_uq[l], w_ukv[l])
        layers.append(dict(
            norm_in=norm_in[l].reshape(1, D_MODEL), w_small=w_small, w_big=w_big, w_qvt=w_qvt,
            wqt=wqt, wk=wk, wvt=wvt,
            q_a_norm=q_a_norm[l].reshape(1, Q_LORA), kv_a_norm=kv_a_norm[l].reshape(1, KV_LORA),
            b_i=b_gates[l, :GATE_CH].reshape(GATE_CH, 1).astype(F32),
            b_f=b_gates[l, GATE_CH:].reshape(GATE_CH, 1).astype(F32),
            m_head_norm=m_head_norm[l].reshape(1, M_WIDTH),
            w_oa=w_oa[l].astype(BF16), w_ob=w_ob[l].astype(BF16), w_out=w_out[l].astype(BF16)))
    nf = norm_f.reshape(1, D_MODEL)
    return (_trunk(x_prompt, layers, nf), _trunk(x_sample, layers, nf))
```

```python
import functools

import jax
import jax.numpy as jnp
from jax import lax
from jax.experimental import pallas as pl
from jax.experimental.pallas import tpu as pltpu

D_MODEL = 2048
A_HEADS = 8
Q_LORA = 512
KV_LORA = 512
QK_NOPE = 128
QK_ROPE = 64
V_HEAD = 128
ROPE_THETA = 10000.0
A_WIDTH = A_HEADS * V_HEAD
M_HEADS = 8
M_QK = 128
M_V = 128
M_WIDTH = M_HEADS * M_V
EPS = 1e-6

LANES = 128
QK_PAD = 256
KV_CHUNK = 512
FLASH_UNROLL = 8
FLASH_HEADS = 2
FLASH_TQ = 512
FLASH_AHEAD = 2
AV_EXT = V_HEAD + 16
PROJ_TM = 1024
PROJ_TN = 2048
NORM_ROWS = 256
MERGE_TM = 256
MERGE_HEADS = 2
VMEM_LIMIT = 56 * 1024 * 1024

SMALL_COLS = Q_LORA + KV_LORA + 2 * LANES

F32 = jnp.float32
BF16 = jnp.bfloat16


def _rms(xf, g):
    ms = jnp.mean(xf * xf, axis=-1, keepdims=True)
    return xf * lax.rsqrt(ms + EPS) * g


def _params(sem):
    return pltpu.CompilerParams(dimension_semantics=sem, vmem_limit_bytes=VMEM_LIMIT)


_NT = (((1,), (1,)), ((), ()))


def _norm_kernel(x_ref, g_ref, h_ref, *, rows):
    def body(r, c):
        sl = pl.ds(pl.multiple_of(r * rows, rows), rows)
        h_ref[sl, :] = _rms(x_ref[sl, :], g_ref[...]).astype(h_ref.dtype)
        return c
    lax.fori_loop(0, x_ref.shape[0] // rows, body, 0)


def _norm(x2, gain):
    t, k = x2.shape
    tm = min(PROJ_TM, t)
    return pl.pallas_call(
        functools.partial(_norm_kernel, rows=min(NORM_ROWS, tm)),
        grid=(t // tm,),
        in_specs=[pl.BlockSpec((tm, k), lambda i: (i, 0)), pl.BlockSpec((1, k), lambda i: (0, 0))],
        out_specs=pl.BlockSpec((tm, k), lambda i: (i, 0)),
        out_shape=jax.ShapeDtypeStruct((t, k), BF16),
        compiler_params=_params(("parallel",)),
        name="in_norm",
    )(x2, gain)


def _proj_kernel(h_ref, w_ref, o_ref, *, feature_major):
    if feature_major:
        out = lax.dot_general(w_ref[...], h_ref[...], _NT, preferred_element_type=F32)
        for c in range(o_ref.shape[0]):
            o_ref[c] = out[:, c * o_ref.shape[2]:(c + 1) * o_ref.shape[2]].astype(o_ref.dtype)
    else:
        out = jnp.dot(h_ref[...], w_ref[...], preferred_element_type=F32)
        o_ref[...] = out.astype(o_ref.dtype)


def _project(h2, w, out_dtype, tm, tn, feature_major=False):
    t, k = h2.shape
    n = w.shape[0] if feature_major else w.shape[1]
    tm = min(tm, t)
    if feature_major:
        w_spec = pl.BlockSpec((tn, k), lambda i, j: (j, 0))
        o_spec = pl.BlockSpec((tm // MCHUNK, tn, MCHUNK), lambda i, j: (i, j, 0))
        o_shape = (t // MCHUNK, n, MCHUNK)
    else:
        w_spec = pl.BlockSpec((k, tn), lambda i, j: (0, j))
        o_spec = pl.BlockSpec((tm, tn), lambda i, j: (i, j))
        o_shape = (t, n)
    return pl.pallas_call(
        functools.partial(_proj_kernel, feature_major=feature_major),
        grid=(t // tm, n // tn),
        in_specs=[pl.BlockSpec((tm, k), lambda i, j: (i, 0)), w_spec],
        out_specs=o_spec,
        out_shape=jax.ShapeDtypeStruct(o_shape, out_dtype),
        compiler_params=_params(("parallel", "arbitrary")),
        name="in_proj_t" if feature_major else "in_proj",
    )(h2, w)


def _mla_prep_kernel(cq_ref, ckv_ref, rest_ref, cs_ref, cst_ref, gq_ref, gkv_ref,
                     wqt_ref, wk_ref, wvt_ref, qt_ref, k_ref, vt_ref, *, q_scale):
    hq = _rms(cq_ref[...], gq_ref[...]).astype(BF16)
    hkv = _rms(ckv_ref[...], gkv_ref[...]).astype(BF16)
    qt = lax.dot_general(wqt_ref[...], hq, _NT, preferred_element_type=F32)
    vt = lax.dot_general(wvt_ref[...], hkv, _NT, preferred_element_type=F32)
    kn = jnp.dot(hkv, wk_ref[...], preferred_element_type=F32)
    t = rest_ref[...] * cs_ref[...]
    lane = lax.broadcasted_iota(jnp.int32, t.shape, 1)
    k_r = jnp.where(lane < QK_ROPE, t + pltpu.roll(t, QK_ROPE, 1), 0.0)
    cst = cst_ref[...]
    pad = jnp.zeros((QK_PAD - QK_NOPE - QK_ROPE, qt.shape[1]), F32)
    ones_blk = (lax.broadcasted_iota(jnp.int32, (AV_EXT - V_HEAD, qt.shape[1]), 0) == 0).astype(BF16)
    for h in range(A_HEADS):
        qh = qt[h * QK_PAD:(h + 1) * QK_PAD]
        tq = qh[QK_NOPE:] * cst
        qt_ref[h] = jnp.concatenate(
            [qh[:QK_NOPE] * q_scale, (tq[:QK_ROPE] + tq[QK_ROPE:]) * q_scale, pad], axis=0).astype(BF16)
        k_ref[h] = jnp.concatenate([kn[:, h * QK_NOPE:(h + 1) * QK_NOPE], k_r], axis=1).astype(BF16)
        vt_ref[h] = jnp.concatenate([vt[h * V_HEAD:(h + 1) * V_HEAD].astype(BF16), ones_blk], axis=0)


def _mla_prep(small, cs, cst, gq, gkv, wqt, wk, wvt, b, s, q_scale):
    tm = min(KV_CHUNK, s // 2)
    nb = s // tm
    full = lambda bi, i: (0, 0)
    return pl.pallas_call(
        functools.partial(_mla_prep_kernel, q_scale=q_scale),
        grid=(b, nb),
        in_specs=[pl.BlockSpec((tm, Q_LORA), lambda bi, i: (bi * nb + i, 0)),
                  pl.BlockSpec((tm, KV_LORA), lambda bi, i: (bi * nb + i, 1)),
                  pl.BlockSpec((tm, LANES), lambda bi, i: (bi * nb + i, (Q_LORA + KV_LORA) // LANES)),
                  pl.BlockSpec((tm, LANES), lambda bi, i: (i, 0)),
                  pl.BlockSpec((LANES, tm), lambda bi, i: (0, i)),
                  pl.BlockSpec((1, Q_LORA), full),
                  pl.BlockSpec((1, KV_LORA), full),
                  pl.BlockSpec((A_HEADS * QK_PAD, Q_LORA), full),
                  pl.BlockSpec((KV_LORA, A_HEADS * QK_NOPE), full),
                  pl.BlockSpec((A_HEADS * V_HEAD, KV_LORA), full)],
        out_specs=[pl.BlockSpec((None, A_HEADS, QK_PAD, tm), lambda bi, i: (bi, 0, 0, i)),
                   pl.BlockSpec((None, A_HEADS, tm, QK_PAD), lambda bi, i: (bi, 0, i, 0)),
                   pl.BlockSpec((None, A_HEADS, None, AV_EXT, tm), lambda bi, i: (bi, 0, i, 0, 0))],
        out_shape=[jax.ShapeDtypeStruct((b, A_HEADS, QK_PAD, s), BF16),
                   jax.ShapeDtypeStruct((b, A_HEADS, s, QK_PAD), BF16),
                   jax.ShapeDtypeStruct((b, A_HEADS, nb, AV_EXT, tm), BF16)],
        compiler_params=_params(("parallel", "parallel")),
        name="mla_prep",
    )(small, small, small, cs, cst, gq, gkv, wqt, wk, wvt)


def _flash_plan(nkb):
    unroll = next(u for u in (FLASH_UNROLL, 8, 2) if nkb % u == 0)
    ahead = FLASH_AHEAD if (nkb % (2 * FLASH_AHEAD) == 0 and unroll % (2 * FLASH_AHEAD) == 0) else 1
    return unroll, ahead


def _flash_kernel(qt_ref, qtn_ref, k_ref, vt_ref, z_ref, o_ref, q2_scr, s_scr, cmax_scr, acc_scr):
    nh, tq = qt_ref.shape[0], qt_ref.shape[2]
    nkb = vt_ref.shape[1]
    tk = k_ref.shape[1] // nkb
    unroll, ahead = _flash_plan(nkb)
    slots = s_scr.shape[1]
    acc_scr[...] = jnp.zeros(acc_scr.shape, F32)
    q2_scr[0] = qt_ref[...]
    q2_scr[1] = qtn_ref[...]

    def scores(hh, kb, qsel, slot):
        start = kb * tk if isinstance(kb, int) else pl.multiple_of(kb * tk, tk)
        st = jnp.dot(k_ref[hh, pl.ds(start, tk), :], q2_scr[qsel, hh], preferred_element_type=F32)
        s_scr[hh, slot] = st
        return jnp.max(st, axis=0, keepdims=True)

    @pl.when(pl.program_id(2) == 0)
    def _():
        for hh in range(nh):
            for a in range(ahead):
                cmax_scr[hh, a] = scores(hh, a, 0, a)

    def softmax_values(hh, kb, slot, cmax, m_prev):
        m_new = jnp.maximum(m_prev, cmax)
        pt = jnp.exp2((s_scr[hh, slot] - m_new).astype(BF16))
        acc_scr[hh] = (jnp.exp2(m_prev - m_new) * acc_scr[hh]
                       + jnp.dot(vt_ref[hh, kb], pt, preferred_element_type=F32))
        return m_new

    def trip(j, carry):
        m = list(carry[:nh])
        cmax = [list(carry[nh + hh * ahead:nh + (hh + 1) * ahead]) for hh in range(nh)]
        for u in range(unroll):
            kb = unroll * j + u
            nxt, qsel = kb + ahead, 0
            if u >= unroll - ahead:
                wrap = nxt >= nkb
                nxt, qsel = jnp.where(wrap, nxt - nkb, nxt), jnp.where(wrap, 1, 0)
            for hh in range(nh):
                c_new = scores(hh, nxt, qsel, (u + ahead) % slots)
                m[hh] = softmax_values(hh, kb, u % slots, cmax[hh][0], m[hh])
                cmax[hh] = cmax[hh][1:] + [c_new]
        return (*m, *[c for per_head in cmax for c in per_head])

    init = (*[jnp.full((1, tq), -jnp.inf, F32)] * nh,
            *[cmax_scr[hh, a] for hh in range(nh) for a in range(ahead)])
    out = lax.fori_loop(0, nkb // unroll, trip, init)
    for hh in range(nh):
        for a in range(ahead):
            cmax_scr[hh, a] = out[nh + hh * ahead + a]
        hs = slice(hh * V_HEAD, (hh + 1) * V_HEAD)
        z = z_ref[:, hs].astype(F32)
        acc = acc_scr[hh]
        o = (acc[:V_HEAD] / acc[V_HEAD:V_HEAD + 1]).T
        o_ref[:, hs] = (o * (z * jax.nn.sigmoid(z))).astype(o_ref.dtype)


def _flash(qt, kc, vt, big, b, s):
    tq = min(FLASH_TQ, s)
    nq = s // tq
    nkb, tk = vt.shape[2], vt.shape[4]
    _, ahead = _flash_plan(nkb)
    nh = FLASH_HEADS
    kv_bytes = nh * s * (QK_PAD + AV_EXT) * jnp.dtype(BF16).itemsize
    kv_mode = dict(pipeline_mode=pl.Buffered(1)) if 2 * kv_bytes > VMEM_LIMIT // 2 else {}
    return pl.pallas_call(
        _flash_kernel,
        grid=(b, A_HEADS // nh, nq),
        in_specs=[pl.BlockSpec((None, nh, QK_PAD, tq), lambda bi, h, i: (bi, h, 0, i)),
                  pl.BlockSpec((None, nh, QK_PAD, tq),
                               lambda bi, h, i: (bi, h, 0, jnp.minimum(i + 1, nq - 1))),
                  pl.BlockSpec((None, nh, s, QK_PAD), lambda bi, h, i: (bi, h, 0, 0), **kv_mode),
                  pl.BlockSpec((None, nh, nkb, AV_EXT, tk), lambda bi, h, i: (bi, h, 0, 0, 0),
                               **kv_mode),
                  pl.BlockSpec((tq, nh * V_HEAD), lambda bi, h, i: (bi * nq + i, h))],
        out_specs=pl.BlockSpec((tq, nh * V_HEAD), lambda bi, h, i: (bi * nq + i, h)),
        out_shape=jax.ShapeDtypeStruct((b * s, A_WIDTH), BF16),
        scratch_shapes=[pltpu.VMEM((2, nh, QK_PAD, tq), BF16),
                        pltpu.VMEM((nh, 2 * ahead, tk, tq), F32),
                        pltpu.VMEM((nh, ahead, 1, tq), F32),
                        pltpu.VMEM((nh, AV_EXT, tq), F32)],
        compiler_params=_params(("parallel", "parallel", "arbitrary")),
        name="mla_flash",
    )(qt, qt, kc, vt, big)


GATE_CH = 2 * M_HEADS
MCHUNK = 2 * LANES
GATE_CHUNKS = 16
ROW_B, ROW_R, ROW_CM, ROW_G, ROW_RMAX = (i * GATE_CH for i in range(5))
ROW_PACK = 5 * GATE_CH


def _gate_kernel(x_ref, bi_ref, bf_ref, col_ref, row_ref):
    nc = row_ref.shape[0]
    shape = (nc * GATE_CH, LANES)
    pos = lax.broadcasted_iota(jnp.int32, shape, 1)
    is_fw = lax.broadcasted_iota(jnp.int32, shape, 0) % GATE_CH < M_HEADS
    shifts = (1, 2, 4, 8, 16, 32, 64)

    def scan(y, op, ident):
        for sft in shifts:
            y_f = jnp.where(pos >= sft, pltpu.roll(y, sft, 1), ident)
            y_b = jnp.where(pos < LANES - sft, pltpu.roll(y, LANES - sft, 1), ident)
            y = op(y, jnp.where(is_fw, y_f, y_b))
        return y

    def total(y, op):
        for sft in shifts:
            y = op(y, pltpu.roll(y, sft, 1))
        return y

    zpad = jnp.zeros((LANES - GATE_CH, LANES), F32)
    log_i, log_f = ([], []), ([], [])
    for c in range(nc):
        for half in range(2):
            rs = slice(c * MCHUNK + half * LANES, c * MCHUNK + (half + 1) * LANES)
            xt = x_ref[rs, :].T
            log_i[half].append(xt[0:GATE_CH] + bi_ref[...])
            log_f[half].append(jax.nn.log_sigmoid(xt[GATE_CH:2 * GATE_CH] + bf_ref[...]))
    li0, li1 = (jnp.concatenate(v, axis=0) for v in log_i)
    lf0, lf1 = (jnp.concatenate(v, axis=0) for v in log_f)
    t0, t1 = total(lf0, jnp.add), total(lf1, jnp.add)
    b0 = scan(lf0, jnp.add, 0.0) + jnp.where(is_fw, 0.0, t1)
    b1 = scan(lf1, jnp.add, 0.0) + jnp.where(is_fw, t0, 0.0)
    r0, r1 = li0 - b0, li1 - b1
    a0, a1 = total(r0, jnp.maximum), total(r1, jnp.maximum)
    e0, e1 = scan(r0, jnp.maximum, -jnp.inf), scan(r1, jnp.maximum, -jnp.inf)
    cm0 = jnp.where(is_fw, e0, jnp.maximum(e0, a1))
    cm1 = jnp.where(is_fw, jnp.maximum(e1, a0), e1)
    g, rmax = t0 + t1, jnp.maximum(a0, a1)
    for c in range(nc):
        cs = slice(c * GATE_CH, (c + 1) * GATE_CH)
        for off, v0, v1 in ((ROW_B, b0, b1), (ROW_R, r0, r1), (ROW_CM, cm0, cm1),
                            (ROW_G, g, g), (ROW_RMAX, rmax, rmax)):
            row_ref[c, off:off + GATE_CH, 0:LANES] = v0[cs]
            row_ref[c, off:off + GATE_CH, LANES:MCHUNK] = v1[cs]
        for half, r in ((0, r0), (1, r1)):
            rs = slice(c * MCHUNK + half * LANES, c * MCHUNK + (half + 1) * LANES)
            col_ref[rs, :] = jnp.concatenate([r[cs], zpad], axis=0).T


def _gates(small, b_i, b_f):
    t = small.shape[0]
    nc = t // MCHUNK
    cpb = min(GATE_CHUNKS, nc)
    return pl.pallas_call(
        _gate_kernel,
        grid=(nc // cpb,),
        in_specs=[pl.BlockSpec((cpb * MCHUNK, LANES), lambda i: (i, SMALL_COLS // LANES - 1)),
                  pl.BlockSpec((GATE_CH, 1), lambda i: (0, 0)),
                  pl.BlockSpec((GATE_CH, 1), lambda i: (0, 0))],
        out_specs=[pl.BlockSpec((cpb * MCHUNK, LANES), lambda i: (i, 0)),
                   pl.BlockSpec((cpb, ROW_PACK, MCHUNK), lambda i: (i, 0, 0))],
        out_shape=[jax.ShapeDtypeStruct((t, LANES), F32),
                   jax.ShapeDtypeStruct((nc, ROW_PACK, MCHUNK), F32)],
        compiler_params=_params(("parallel",)),
        name="mlstm_gates",
    )(small, b_i, b_f)


V_EXT = M_V + 16


def _mlstm_kernel(qtf_ref, kf_ref, vtf_ref, rowf_ref, colf_ref,
                  qtb_ref, kb_ref, vtb_ref, rowb_ref, colb_ref,
                  hf_ref, hb_ref, c_scr, m_scr):
    @pl.when(pl.program_id(1) == 0)
    def _():
        c_scr[...] = jnp.zeros(c_scr.shape, F32)
        m_scr[...] = jnp.zeros(m_scr.shape, F32)

    ll = lax.broadcasted_iota(jnp.int32, (MCHUNK, MCHUNK), 0)
    jj = lax.broadcasted_iota(jnp.int32, (MCHUNK, MCHUNK), 1)
    ones_blk = (lax.broadcasted_iota(jnp.int32, (V_EXT - M_V, MCHUNK), 0) == 0).astype(F32)
    inv_scale = float(M_QK) ** 0.5
    m_all = m_scr[...]
    m_next = []
    dirs = ((qtf_ref, kf_ref, vtf_ref, rowf_ref, colf_ref[...], hf_ref, ll <= jj),
            (qtb_ref, kb_ref, vtb_ref, rowb_ref, colb_ref[...], hb_ref, ll >= jj))
    units = [(d, h) for d in range(2) for h in range(M_HEADS)]

    def row(d, off, h):
        r = off + d * M_HEADS + h
        return dirs[d][3][r:r + 1, :]

    st, cq = {}, {}
    for d, h in units:
        qt_ref, k_ref = dirs[d][0], dirs[d][1]
        hs = slice(h * M_QK, (h + 1) * M_QK)
        qt = qt_ref[hs, :]
        st[d, h] = jnp.dot(k_ref[:, hs], qt, preferred_element_type=F32)
        cq[d, h] = jnp.dot(c_scr[d, h].astype(BF16), qt, preferred_element_type=F32)

    for d, h in units:
        _, _, vt_ref, _, col, h_ref, mask = dirs[d]
        ch = d * M_HEADS + h
        hs = slice(h * M_QK, (h + 1) * M_QK)
        m_prev = m_all[ch:ch + 1, :]
        mm = jnp.maximum(m_prev, row(d, ROW_CM, h))
        r_col = col[:, ch:ch + 1]
        pt = (jnp.exp(jnp.where(mask, r_col - mm, -jnp.inf)) * st[d, h]).astype(BF16)
        vt_ext = jnp.concatenate([vt_ref[hs, :], ones_blk.astype(BF16)], axis=0)
        tot = (jnp.dot(vt_ext, pt, preferred_element_type=F32)
               + jnp.exp(m_prev - mm) * cq[d, h])
        floor = jnp.exp(-row(d, ROW_B, h) - mm) * inv_scale
        ht = tot[:M_V] / jnp.maximum(jnp.abs(tot[M_V:M_V + 1]), floor)
        h_ref[:, hs] = ht.T

    for d, h in units:
        k_ref, vt_ref = dirs[d][1], dirs[d][2]
        ch = d * M_HEADS + h
        hs = slice(h * M_QK, (h + 1) * M_QK)
        m_prev = m_all[ch:ch + 1, :]
        mx = jnp.maximum(m_prev, row(d, ROW_RMAX, h))
        w_row = jnp.exp(row(d, ROW_R, h) - mx)
        lhs = jnp.concatenate([(vt_ref[hs, :].astype(F32) * w_row).astype(BF16),
                               (ones_blk * w_row).astype(BF16)], axis=0)
        c_scr[d, h] = (jnp.exp(m_prev - mx)[:, :M_QK] * c_scr[d, h]
                       + jnp.dot(lhs, k_ref[:, hs], preferred_element_type=F32))
        m_next.append(row(d, ROW_G, h) + mx)
    m_scr[...] = jnp.concatenate(m_next, axis=0)


def _mlstm(qvt, big, row, col, b, s):
    nc = s // MCHUNK
    fw = lambda bi, i: bi * nc + i
    bw = lambda bi, i: bi * nc + (nc - 1 - i)

    def specs(blk):
        return [pl.BlockSpec((None, M_WIDTH, MCHUNK), lambda bi, i: (blk(bi, i), 0, 0)),
                pl.BlockSpec((MCHUNK, M_WIDTH), lambda bi, i: (blk(bi, i), 1)),
                pl.BlockSpec((None, M_WIDTH, MCHUNK), lambda bi, i: (blk(bi, i), 1, 0)),
                pl.BlockSpec((None, ROW_PACK, MCHUNK), lambda bi, i: (blk(bi, i), 0, 0)),
                pl.BlockSpec((MCHUNK, LANES), lambda bi, i: (blk(bi, i), 0))]

    out = jax.ShapeDtypeStruct((b * s, M_WIDTH), F32)
    return pl.pallas_call(
        _mlstm_kernel,
        grid=(b, nc),
        in_specs=specs(fw) + specs(bw),
        out_specs=[pl.BlockSpec((MCHUNK, M_WIDTH), lambda bi, i: (fw(bi, i), 0)),
                   pl.BlockSpec((MCHUNK, M_WIDTH), lambda bi, i: (bw(bi, i), 0))],
        out_shape=[out, out],
        scratch_shapes=[pltpu.VMEM((2, M_HEADS, V_EXT, M_QK), F32),
                        pltpu.VMEM((GATE_CH, MCHUNK), F32)],
        compiler_params=_params(("parallel", "arbitrary")),
        name="mlstm",
    )(qvt, big, qvt, row, col, qvt, big, qvt, row, col)


def _merge_kernel(x_ref, ya_ref, hf_ref, hb_ref, om_ref, zm_ref, ga_ref, gb_ref, hn_ref,
                  woa_ref, wob_ref, wout_ref, nf_ref, o_ref, hnext_ref=None, *, final_norm):
    pa = jnp.dot(ya_ref[...], woa_ref[...], preferred_element_type=F32)
    pb = None
    for g in range(M_HEADS // MERGE_HEADS):
        parts = []
        for h in range(g * MERGE_HEADS, (g + 1) * MERGE_HEADS):
            hs = slice(h * M_V, (h + 1) * M_V)
            hh = hf_ref[:, hs] + hb_ref[:, hs]
            ms = jnp.mean(hh * hh, axis=-1, keepdims=True)
            zm = zm_ref[:, hs].astype(F32)
            parts.append(hh * lax.rsqrt(ms + EPS) * hn_ref[:, hs]
                         * jax.nn.sigmoid(om_ref[:, hs].astype(F32)) * (zm * jax.nn.sigmoid(zm)))
        gs = slice(g * MERGE_HEADS * M_V, (g + 1) * MERGE_HEADS * M_V)
        t = jnp.dot(jnp.concatenate(parts, axis=1).astype(BF16), wob_ref[gs, :],
                    preferred_element_type=F32)
        pb = t if pb is None else pb + t
    merged = (jax.nn.sigmoid(ga_ref[...].astype(F32)) * pa
              + jax.nn.sigmoid(gb_ref[...].astype(F32)) * pb)
    out = x_ref[...] + jnp.dot(merged.astype(BF16), wout_ref[...], preferred_element_type=F32)
    if final_norm:
        o_ref[...] = _rms(out, nf_ref[...])
    else:
        o_ref[...] = out
        hnext_ref[...] = _rms(out, nf_ref[...]).astype(hnext_ref.dtype)


def _merge(x2, ya, hf, hb, big, hn, woa, wob, wout, nf, final_norm):
    t = x2.shape[0]
    tm = min(MERGE_TM, t)
    row = lambda c: (lambda i: (i, c))
    const = lambda shape: pl.BlockSpec(shape, lambda i: (0, 0), pipeline_mode=pl.Buffered(1))
    o_spec = pl.BlockSpec((tm, D_MODEL), row(0))
    o_shape = jax.ShapeDtypeStruct((t, D_MODEL), F32)
    if not final_norm:
        o_spec = [o_spec, pl.BlockSpec((tm, D_MODEL), row(0))]
        o_shape = [o_shape, jax.ShapeDtypeStruct((t, D_MODEL), BF16)]
    return pl.pallas_call(
        functools.partial(_merge_kernel, final_norm=final_norm),
        grid=(t // tm,),
        in_specs=[pl.BlockSpec((tm, D_MODEL), row(0)),
                  pl.BlockSpec((tm, A_WIDTH), row(0)),
                  pl.BlockSpec((tm, M_WIDTH), row(0)),
                  pl.BlockSpec((tm, M_WIDTH), row(0)),
                  pl.BlockSpec((tm, M_WIDTH), row(2)),
                  pl.BlockSpec((tm, M_WIDTH), row(3)),
                  pl.BlockSpec((tm, D_MODEL), row(2)),
                  pl.BlockSpec((tm, D_MODEL), row(3)),
                  const((1, M_WIDTH)),
                  const((A_WIDTH, D_MODEL)),
                  const((M_WIDTH, D_MODEL)),
                  const((D_MODEL, D_MODEL)),
                  const((1, D_MODEL))],
        out_specs=o_spec,
        out_shape=o_shape,
        compiler_params=_params(("parallel",)),
        name="merge_out",
    )(x2, ya, hf, hb, big, big, big, big, hn, woa, wob, wout, nf)


def _rot_cols(w):
    half = QK_ROPE // 2
    return jnp.concatenate([-w[:, half:], w[:, :half]], axis=1)


def _pack_layer(w_in, w_uq, w_ukv):
    o = 0
    seg = {}
    for name, width in (("c_q", Q_LORA), ("c_kv", KV_LORA), ("k_rope", QK_ROPE), ("z_a", A_WIDTH),
                        ("q_m", M_WIDTH), ("k_m", M_WIDTH), ("v_m", M_WIDTH), ("o_m", M_WIDTH),
                        ("z_m", M_WIDTH), ("gates", 4 * M_HEADS), ("g_a", D_MODEL), ("g_b", D_MODEL)):
        seg[name] = w_in[:, o:o + width]
        o += width
    pad = jnp.zeros((D_MODEL, LANES - 4 * M_HEADS), w_in.dtype)
    w_small = jnp.concatenate([seg["c_q"], seg["c_kv"], seg["k_rope"], _rot_cols(seg["k_rope"]),
                               seg["gates"], pad], axis=1).astype(BF16)
    w_big = jnp.concatenate([seg[n] for n in ("z_a", "k_m", "o_m", "z_m", "g_a", "g_b")],
                            axis=1).astype(BF16)
    w_qvt = jnp.concatenate([seg["q_m"], seg["v_m"]], axis=1).T.astype(BF16)
    wq = w_uq.reshape(Q_LORA, A_HEADS, QK_NOPE + QK_ROPE)
    rope = wq[:, :, QK_NOPE:]
    rot = jnp.concatenate([-rope[:, :, QK_ROPE // 2:], rope[:, :, :QK_ROPE // 2]], axis=2)
    wqt = jnp.concatenate([wq, rot], axis=2).reshape(Q_LORA, A_HEADS * QK_PAD).T.astype(BF16)
    wkv = w_ukv.reshape(KV_LORA, A_HEADS, QK_NOPE + V_HEAD)
    wk = wkv[:, :, :QK_NOPE].reshape(KV_LORA, A_HEADS * QK_NOPE).astype(BF16)
    wvt = wkv[:, :, QK_NOPE:].reshape(KV_LORA, A_HEADS * V_HEAD).T.astype(BF16)
    return w_small, w_big, w_qvt, wqt, wk, wvt


def _rope_table(s):
    inv = ROPE_THETA ** (-jnp.arange(0, QK_ROPE, 2, dtype=F32) / QK_ROPE)
    ang = jnp.arange(s, dtype=F32)[:, None] * inv[None, :]
    cos, sin = jnp.cos(ang), jnp.sin(ang)
    return jnp.concatenate([cos, cos, sin, sin], axis=1)


def _trunk(x, layers, norm_f):
    b, s, d = x.shape
    assert d == D_MODEL and s % MCHUNK == 0 and s % (2 * LANES) == 0, x.shape
    x2 = x.reshape(b * s, D_MODEL)
    cs = _rope_table(s)
    cst = cs.T
    q_scale = float((QK_NOPE + QK_ROPE) ** -0.5 * 1.4426950408889634)
    h2 = _norm(x2, layers[0]["norm_in"])
    for li, ly in enumerate(layers):
        last = li == len(layers) - 1
        small = _project(h2, ly["w_small"], F32, PROJ_TM, SMALL_COLS)
        big = _project(h2, ly["w_big"], BF16, PROJ_TM, PROJ_TN)
        qt, kc, vt = _mla_prep(small, cs, cst, ly["q_a_norm"], ly["kv_a_norm"], ly["wqt"], ly["wk"],
                               ly["wvt"], b, s, q_scale)
        ya = _flash(qt, kc, vt, big, b, s)
        qvt = _project(h2, ly["w_qvt"], BF16, PROJ_TM, PROJ_TN, feature_major=True)
        col, row = _gates(small, ly["b_i"], ly["b_f"])
        hf, hb = _mlstm(qvt, big, row, col, b, s)
        res = _merge(x2, ya, hf, hb, big, ly["m_head_norm"], ly["w_oa"], ly["w_ob"], ly["w_out"],
                     norm_f if last else layers[li + 1]["norm_in"], last)
        x2, h2 = (res, None) if last else res
    return x2.reshape(b, s, D_MODEL)


def kernel(x_prompt, x_sample, norm_in, w_in, b_gates, q_a_norm, w_uq, kv_a_norm, w_ukv, w_oa,
           m_head_norm, w_ob, w_out, norm_f):
    layers = []
    for l in range(w_in.shape[0]):
        w_small, w_big, w_qvt, wqt, wk, wvt = _pack_layer(w_in[l], w_uq[l], w_ukv[l])
        layers.append(dict(
            norm_in=norm_in[l].reshape(1, D_MODEL), w_small=w_small, w_big=w_big, w_qvt=w_qvt,
            wqt=wqt, wk=wk, wvt=wvt,
            q_a_norm=q_a_norm[l].reshape(1, Q_LORA), kv_a_norm=kv_a_norm[l].reshape(1, KV_LORA),
            b_i=b_gates[l, :GATE_CH].reshape(GATE_CH, 1).astype(F32),
            b_f=b_gates[l, GATE_CH:].reshape(GATE_CH, 1).astype(F32),
            m_head_norm=m_head_norm[l].reshape(1, M_WIDTH),
            w_oa=w_oa[l].astype(BF16), w_ob=w_ob[l].astype(BF16), w_out=w_out[l].astype(BF16)))
    nf = norm_f.reshape(1, D_MODEL)
    return (_trunk(x_prompt, layers, nf), _trunk(x_sample, layers, nf))
```

```python
import functools

import jax
import jax.numpy as jnp
from jax import lax
from jax.experimental import pallas as pl
from jax.experimental.pallas import tpu as pltpu

D_MODEL = 2048
A_HEADS = 8
Q_LORA = 512
KV_LORA = 512
QK_NOPE = 128
QK_ROPE = 64
V_HEAD = 128
ROPE_THETA = 10000.0
A_WIDTH = A_HEADS * V_HEAD
M_HEADS = 8
M_QK = 128
M_V = 128
M_WIDTH = M_HEADS * M_V
EPS = 1e-6

LANES = 128
QK_PAD = 256
KV_CHUNK = 512
FLASH_UNROLL = 8
FLASH_HEADS = 2
FLASH_TQ = 512
FLASH_AHEAD = 2
AV_EXT = V_HEAD + 16
PROJ_TM = 1024
PROJ_TN = 2048
NORM_ROWS = 256
MERGE_TM = 256
VMEM_LIMIT = 56 * 1024 * 1024

SMALL_COLS = Q_LORA + KV_LORA + 2 * LANES

F32 = jnp.float32
BF16 = jnp.bfloat16


def _rms(xf, g):
    ms = jnp.mean(xf * xf, axis=-1, keepdims=True)
    return xf * lax.rsqrt(ms + EPS) * g


def _params(sem):
    return pltpu.CompilerParams(dimension_semantics=sem, vmem_limit_bytes=VMEM_LIMIT)


_NT = (((1,), (1,)), ((), ()))


def _norm_kernel(x_ref, g_ref, h_ref, *, rows):
    def body(r, c):
        sl = pl.ds(pl.multiple_of(r * rows, rows), rows)
        h_ref[sl, :] = _rms(x_ref[sl, :], g_ref[...]).astype(h_ref.dtype)
        return c
    lax.fori_loop(0, x_ref.shape[0] // rows, body, 0)


def _norm(x2, gain):
    t, k = x2.shape
    tm = min(PROJ_TM, t)
    return pl.pallas_call(
        functools.partial(_norm_kernel, rows=min(NORM_ROWS, tm)),
        grid=(t // tm,),
        in_specs=[pl.BlockSpec((tm, k), lambda i: (i, 0)), pl.BlockSpec((1, k), lambda i: (0, 0))],
        out_specs=pl.BlockSpec((tm, k), lambda i: (i, 0)),
        out_shape=jax.ShapeDtypeStruct((t, k), BF16),
        compiler_params=_params(("parallel",)),
        name="in_norm",
    )(x2, gain)


def _proj_kernel(h_ref, w_ref, o_ref, *, feature_major):
    if feature_major:
        out = lax.dot_general(w_ref[...], h_ref[...], _NT, preferred_element_type=F32)
        for c in range(o_ref.shape[0]):
            o_ref[c] = out[:, c * o_ref.shape[2]:(c + 1) * o_ref.shape[2]].astype(o_ref.dtype)
    else:
        out = jnp.dot(h_ref[...], w_ref[...], preferred_element_type=F32)
        o_ref[...] = out.astype(o_ref.dtype)


def _project(h2, w, out_dtype, tm, tn, feature_major=False):
    t, k = h2.shape
    n = w.shape[0] if feature_major else w.shape[1]
    tm = min(tm, t)
    if feature_major:
        w_spec = pl.BlockSpec((tn, k), lambda i, j: (j, 0))
        o_spec = pl.BlockSpec((tm // MCHUNK, tn, MCHUNK), lambda i, j: (i, j, 0))
        o_shape = (t // MCHUNK, n, MCHUNK)
    else:
        w_spec = pl.BlockSpec((k, tn), lambda i, j: (0, j))
        o_spec = pl.BlockSpec((tm, tn), lambda i, j: (i, j))
        o_shape = (t, n)
    return pl.pallas_call(
        functools.partial(_proj_kernel, feature_major=feature_major),
        grid=(t // tm, n // tn),
        in_specs=[pl.BlockSpec((tm, k), lambda i, j: (i, 0)), w_spec],
        out_specs=o_spec,
        out_shape=jax.ShapeDtypeStruct(o_shape, out_dtype),
        compiler_params=_params(("parallel", "arbitrary")),
        name="in_proj_t" if feature_major else "in_proj",
    )(h2, w)


def _mla_prep_kernel(cq_ref, ckv_ref, rest_ref, cs_ref, cst_ref, gq_ref, gkv_ref,
                     wqt_ref, wk_ref, wvt_ref, qt_ref, k_ref, vt_ref, *, q_scale):
    hq = _rms(cq_ref[...], gq_ref[...]).astype(BF16)
    hkv = _rms(ckv_ref[...], gkv_ref[...]).astype(BF16)
    qt = lax.dot_general(wqt_ref[...], hq, _NT, preferred_element_type=F32)
    vt = lax.dot_general(wvt_ref[...], hkv, _NT, preferred_element_type=F32)
    kn = jnp.dot(hkv, wk_ref[...], preferred_element_type=F32)
    t = rest_ref[...] * cs_ref[...]
    lane = lax.broadcasted_iota(jnp.int32, t.shape, 1)
    k_r = jnp.where(lane < QK_ROPE, t + pltpu.roll(t, QK_ROPE, 1), 0.0)
    cst = cst_ref[...]
    pad = jnp.zeros((QK_PAD - QK_NOPE - QK_ROPE, qt.shape[1]), F32)
    ones_blk = (lax.broadcasted_iota(jnp.int32, (AV_EXT - V_HEAD, qt.shape[1]), 0) == 0).astype(BF16)
    for h in range(A_HEADS):
        qh = qt[h * QK_PAD:(h + 1) * QK_PAD]
        tq = qh[QK_NOPE:] * cst
        qt_ref[h] = jnp.concatenate(
            [qh[:QK_NOPE] * q_scale, (tq[:QK_ROPE] + tq[QK_ROPE:]) * q_scale, pad], axis=0).astype(BF16)
        k_ref[h] = jnp.concatenate([kn[:, h * QK_NOPE:(h + 1) * QK_NOPE], k_r], axis=1).astype(BF16)
        vt_ref[h] = jnp.concatenate([vt[h * V_HEAD:(h + 1) * V_HEAD].astype(BF16), ones_blk], axis=0)


def _mla_prep(small, cs, cst, gq, gkv, wqt, wk, wvt, b, s, q_scale):
    tm = min(KV_CHUNK, s // 2)
    nb = s // tm
    full = lambda bi, i: (0, 0)
    return pl.pallas_call(
        functools.partial(_mla_prep_kernel, q_scale=q_scale),
        grid=(b, nb),
        in_specs=[pl.BlockSpec((tm, Q_LORA), lambda bi, i: (bi * nb + i, 0)),
                  pl.BlockSpec((tm, KV_LORA), lambda bi, i: (bi * nb + i, 1)),
                  pl.BlockSpec((tm, LANES), lambda bi, i: (bi * nb + i, (Q_LORA + KV_LORA) // LANES)),
                  pl.BlockSpec((tm, LANES), lambda bi, i: (i, 0)),
                  pl.BlockSpec((LANES, tm), lambda bi, i: (0, i)),
                  pl.BlockSpec((1, Q_LORA), full),
                  pl.BlockSpec((1, KV_LORA), full),
                  pl.BlockSpec((A_HEADS * QK_PAD, Q_LORA), full),
                  pl.BlockSpec((KV_LORA, A_HEADS * QK_NOPE), full),
                  pl.BlockSpec((A_HEADS * V_HEAD, KV_LORA), full)],
        out_specs=[pl.BlockSpec((None, A_HEADS, QK_PAD, tm), lambda bi, i: (bi, 0, 0, i)),
                   pl.BlockSpec((None, A_HEADS, tm, QK_PAD), lambda bi, i: (bi, 0, i, 0)),
                   pl.BlockSpec((None, A_HEADS, None, AV_EXT, tm), lambda bi, i: (bi, 0, i, 0, 0))],
        out_shape=[jax.ShapeDtypeStruct((b, A_HEADS, QK_PAD, s), BF16),
                   jax.ShapeDtypeStruct((b, A_HEADS, s, QK_PAD), BF16),
                   jax.ShapeDtypeStruct((b, A_HEADS, nb, AV_EXT, tm), BF16)],
        compiler_params=_params(("parallel", "parallel")),
        name="mla_prep",
    )(small, small, small, cs, cst, gq, gkv, wqt, wk, wvt)


def _flash_plan(nkb):
    unroll = next(u for u in (FLASH_UNROLL, 8, 2) if nkb % u == 0)
    ahead = FLASH_AHEAD if (nkb % (2 * FLASH_AHEAD) == 0 and unroll % (2 * FLASH_AHEAD) == 0) else 1
    return unroll, ahead


def _flash_kernel(qt_ref, qtn_ref, k_ref, vt_ref, z_ref, o_ref, q2_scr, s_scr, cmax_scr, acc_scr):
    nh, tq = qt_ref.shape[0], qt_ref.shape[2]
    nkb = vt_ref.shape[1]
    tk = k_ref.shape[1] // nkb
    unroll, ahead = _flash_plan(nkb)
    slots = s_scr.shape[1]
    acc_scr[...] = jnp.zeros(acc_scr.shape, F32)
    q2_scr[0] = qt_ref[...]
    q2_scr[1] = qtn_ref[...]

    def scores(hh, kb, qsel, slot):
        start = kb * tk if isinstance(kb, int) else pl.multiple_of(kb * tk, tk)
        st = jnp.dot(k_ref[hh, pl.ds(start, tk), :], q2_scr[qsel, hh], preferred_element_type=F32)
        s_scr[hh, slot] = st
        return jnp.max(st, axis=0, keepdims=True)

    @pl.when(pl.program_id(2) == 0)
    def _():
        for hh in range(nh):
            for a in range(ahead):
                cmax_scr[hh, a] = scores(hh, a, 0, a)

    def softmax_values(hh, kb, slot, cmax, m_prev):
        m_new = jnp.maximum(m_prev, cmax)
        pt = jnp.exp2((s_scr[hh, slot] - m_new).astype(BF16))
        acc_scr[hh] = (jnp.exp2(m_prev - m_new) * acc_scr[hh]
                       + jnp.dot(vt_ref[hh, kb], pt, preferred_element_type=F32))
        return m_new

    def trip(j, carry):
        m = list(carry[:nh])
        cmax = [list(carry[nh + hh * ahead:nh + (hh + 1) * ahead]) for hh in range(nh)]
        for u in range(unroll):
            kb = unroll * j + u
            nxt, qsel = kb + ahead, 0
            if u >= unroll - ahead:
                wrap = nxt >= nkb
                nxt, qsel = jnp.where(wrap, nxt - nkb, nxt), jnp.where(wrap, 1, 0)
            for hh in range(nh):
                c_new = scores(hh, nxt, qsel, (u + ahead) % slots)
                m[hh] = softmax_values(hh, kb, u % slots, cmax[hh][0], m[hh])
                cmax[hh] = cmax[hh][1:] + [c_new]
        return (*m, *[c for per_head in cmax for c in per_head])

    init = (*[jnp.full((1, tq), -jnp.inf, F32)] * nh,
            *[cmax_scr[hh, a] for hh in range(nh) for a in range(ahead)])
    out = lax.fori_loop(0, nkb // unroll, trip, init)
    for hh in range(nh):
        for a in range(ahead):
            cmax_scr[hh, a] = out[nh + hh * ahead + a]
        hs = slice(hh * V_HEAD, (hh + 1) * V_HEAD)
        z = z_ref[:, hs].astype(F32)
        acc = acc_scr[hh]
        o = (acc[:V_HEAD] / acc[V_HEAD:V_HEAD + 1]).T
        o_ref[:, hs] = (o * (z * jax.nn.sigmoid(z))).astype(o_ref.dtype)


def _flash(qt, kc, vt, big, b, s):
    tq = min(FLASH_TQ, s)
    nq = s // tq
    nkb, tk = vt.shape[2], vt.shape[4]
    _, ahead = _flash_plan(nkb)
    nh = FLASH_HEADS
    kv_bytes = nh * s * (QK_PAD + AV_EXT) * jnp.dtype(BF16).itemsize
    kv_mode = dict(pipeline_mode=pl.Buffered(1)) if 2 * kv_bytes > VMEM_LIMIT // 2 else {}
    return pl.pallas_call(
        _flash_kernel,
        grid=(b, A_HEADS // nh, nq),
        in_specs=[pl.BlockSpec((None, nh, QK_PAD, tq), lambda bi, h, i: (bi, h, 0, i)),
                  pl.BlockSpec((None, nh, QK_PAD, tq),
                               lambda bi, h, i: (bi, h, 0, jnp.minimum(i + 1, nq - 1))),
                  pl.BlockSpec((None, nh, s, QK_PAD), lambda bi, h, i: (bi, h, 0, 0), **kv_mode),
                  pl.BlockSpec((None, nh, nkb, AV_EXT, tk), lambda bi, h, i: (bi, h, 0, 0, 0),
                               **kv_mode),
                  pl.BlockSpec((tq, nh * V_HEAD), lambda bi, h, i: (bi * nq + i, h))],
        out_specs=pl.BlockSpec((tq, nh * V_HEAD), lambda bi, h, i: (bi * nq + i, h)),
        out_shape=jax.ShapeDtypeStruct((b * s, A_WIDTH), BF16),
        scratch_shapes=[pltpu.VMEM((2, nh, QK_PAD, tq), BF16),
                        pltpu.VMEM((nh, 2 * ahead, tk, tq), F32),
                        pltpu.VMEM((nh, ahead, 1, tq), F32),
                        pltpu.VMEM((nh, AV_EXT, tq), F32)],
        compiler_params=_params(("parallel", "parallel", "arbitrary")),
        name="mla_flash",
    )(qt, qt, kc, vt, big)


GATE_CH = 2 * M_HEADS
MCHUNK = 2 * LANES
GATE_CHUNKS = 16
ROW_B, ROW_R, ROW_CM, ROW_G, ROW_RMAX = (i * GATE_CH for i in range(5))
ROW_PACK = 5 * GATE_CH


def _gate_kernel(x_ref, bi_ref, bf_ref, col_ref, row_ref):
    nc = row_ref.shape[0]
    shape = (nc * GATE_CH, LANES)
    pos = lax.broadcasted_iota(jnp.int32, shape, 1)
    is_fw = lax.broadcasted_iota(jnp.int32, shape, 0) % GATE_CH < M_HEADS
    shifts = (1, 2, 4, 8, 16, 32, 64)

    def scan(y, op, ident):
        for sft in shifts:
            y_f = jnp.where(pos >= sft, pltpu.roll(y, sft, 1), ident)
            y_b = jnp.where(pos < LANES - sft, pltpu.roll(y, LANES - sft, 1), ident)
            y = op(y, jnp.where(is_fw, y_f, y_b))
        return y

    def total(y, op):
        for sft in shifts:
            y = op(y, pltpu.roll(y, sft, 1))
        return y

    zpad = jnp.zeros((LANES - GATE_CH, LANES), F32)
    log_i, log_f = ([], []), ([], [])
    for c in range(nc):
        for half in range(2):
            rs = slice(c * MCHUNK + half * LANES, c * MCHUNK + (half + 1) * LANES)
            xt = x_ref[rs, :].T
            log_i[half].append(xt[0:GATE_CH] + bi_ref[...])
            log_f[half].append(jax.nn.log_sigmoid(xt[GATE_CH:2 * GATE_CH] + bf_ref[...]))
    li0, li1 = (jnp.concatenate(v, axis=0) for v in log_i)
    lf0, lf1 = (jnp.concatenate(v, axis=0) for v in log_f)
    t0, t1 = total(lf0, jnp.add), total(lf1, jnp.add)
    b0 = scan(lf0, jnp.add, 0.0) + jnp.where(is_fw, 0.0, t1)
    b1 = scan(lf1, jnp.add, 0.0) + jnp.where(is_fw, t0, 0.0)
    r0, r1 = li0 - b0, li1 - b1
    a0, a1 = total(r0, jnp.maximum), total(r1, jnp.maximum)
    e0, e1 = scan(r0, jnp.maximum, -jnp.inf), scan(r1, jnp.maximum, -jnp.inf)
    cm0 = jnp.where(is_fw, e0, jnp.maximum(e0, a1))
    cm1 = jnp.where(is_fw, jnp.maximum(e1, a0), e1)
    g, rmax = t0 + t1, jnp.maximum(a0, a1)
    for c in range(nc):
        cs = slice(c * GATE_CH, (c + 1) * GATE_CH)
        for off, v0, v1 in ((ROW_B, b0, b1), (ROW_R, r0, r1), (ROW_CM, cm0, cm1),
                            (ROW_G, g, g), (ROW_RMAX, rmax, rmax)):
            row_ref[c, off:off + GATE_CH, 0:LANES] = v0[cs]
            row_ref[c, off:off + GATE_CH, LANES:MCHUNK] = v1[cs]
        for half, r in ((0, r0), (1, r1)):
            rs = slice(c * MCHUNK + half * LANES, c * MCHUNK + (half + 1) * LANES)
            col_ref[rs, :] = jnp.concatenate([r[cs], zpad], axis=0).T


def _gates(small, b_i, b_f):
    t = small.shape[0]
    nc = t // MCHUNK
    cpb = min(GATE_CHUNKS, nc)
    return pl.pallas_call(
        _gate_kernel,
        grid=(nc // cpb,),
        in_specs=[pl.BlockSpec((cpb * MCHUNK, LANES), lambda i: (i, SMALL_COLS // LANES - 1)),
                  pl.BlockSpec((GATE_CH, 1), lambda i: (0, 0)),
                  pl.BlockSpec((GATE_CH, 1), lambda i: (0, 0))],
        out_specs=[pl.BlockSpec((cpb * MCHUNK, LANES), lambda i: (i, 0)),
                   pl.BlockSpec((cpb, ROW_PACK, MCHUNK), lambda i: (i, 0, 0))],
        out_shape=[jax.ShapeDtypeStruct((t, LANES), F32),
                   jax.ShapeDtypeStruct((nc, ROW_PACK, MCHUNK), F32)],
        compiler_params=_params(("parallel",)),
        name="mlstm_gates",
    )(small, b_i, b_f)


V_EXT = M_V + 16


def _mlstm_kernel(qtf_ref, kf_ref, vtf_ref, rowf_ref, colf_ref,
                  qtb_ref, kb_ref, vtb_ref, rowb_ref, colb_ref,
                  hf_ref, hb_ref, c_scr, m_scr):
    @pl.when(pl.program_id(1) == 0)
    def _():
        c_scr[...] = jnp.zeros(c_scr.shape, F32)
        m_scr[...] = jnp.zeros(m_scr.shape, F32)

    ll = lax.broadcasted_iota(jnp.int32, (MCHUNK, MCHUNK), 0)
    jj = lax.broadcasted_iota(jnp.int32, (MCHUNK, MCHUNK), 1)
    ones_blk = (lax.broadcasted_iota(jnp.int32, (V_EXT - M_V, MCHUNK), 0) == 0).astype(F32)
    inv_scale = float(M_QK) ** 0.5
    m_all = m_scr[...]
    m_next = []
    dirs = ((qtf_ref, kf_ref, vtf_ref, rowf_ref, colf_ref[...], hf_ref, ll <= jj),
            (qtb_ref, kb_ref, vtb_ref, rowb_ref, colb_ref[...], hb_ref, ll >= jj))
    units = [(d, h) for d in range(2) for h in range(M_HEADS)]

    def row(d, off, h):
        r = off + d * M_HEADS + h
        return dirs[d][3][r:r + 1, :]

    st, cq = {}, {}
    for d, h in units:
        qt_ref, k_ref = dirs[d][0], dirs[d][1]
        hs = slice(h * M_QK, (h + 1) * M_QK)
        qt = qt_ref[hs, :]
        st[d, h] = jnp.dot(k_ref[:, hs], qt, preferred_element_type=F32)
        cq[d, h] = jnp.dot(c_scr[d, h].astype(BF16), qt, preferred_element_type=F32)

    for d, h in units:
        _, _, vt_ref, _, col, h_ref, mask = dirs[d]
        ch = d * M_HEADS + h
        hs = slice(h * M_QK, (h + 1) * M_QK)
        m_prev = m_all[ch:ch + 1, :]
        mm = jnp.maximum(m_prev, row(d, ROW_CM, h))
        r_col = col[:, ch:ch + 1]
        pt = (jnp.exp(jnp.where(mask, r_col - mm, -jnp.inf)) * st[d, h]).astype(BF16)
        vt_ext = jnp.concatenate([vt_ref[hs, :], ones_blk.astype(BF16)], axis=0)
        tot = (jnp.dot(vt_ext, pt, preferred_element_type=F32)
               + jnp.exp(m_prev - mm) * cq[d, h])
        floor = jnp.exp(-row(d, ROW_B, h) - mm) * inv_scale
        ht = tot[:M_V] / jnp.maximum(jnp.abs(tot[M_V:M_V + 1]), floor)
        h_ref[:, hs] = ht.T

    for d, h in units:
        k_ref, vt_ref = dirs[d][1], dirs[d][2]
        ch = d * M_HEADS + h
        hs = slice(h * M_QK, (h + 1) * M_QK)
        m_prev = m_all[ch:ch + 1, :]
        mx = jnp.maximum(m_prev, row(d, ROW_RMAX, h))
        w_row = jnp.exp(row(d, ROW_R, h) - mx)
        lhs = jnp.concatenate([(vt_ref[hs, :].astype(F32) * w_row).astype(BF16),
                               (ones_blk * w_row).astype(BF16)], axis=0)
        c_scr[d, h] = (jnp.exp(m_prev - mx)[:, :M_QK] * c_scr[d, h]
                       + jnp.dot(lhs, k_ref[:, hs], preferred_element_type=F32))
        m_next.append(row(d, ROW_G, h) + mx)
    m_scr[...] = jnp.concatenate(m_next, axis=0)


def _mlstm(qvt, big, row, col, b, s):
    nc = s // MCHUNK
    fw = lambda bi, i: bi * nc + i
    bw = lambda bi, i: bi * nc + (nc - 1 - i)

    def specs(blk):
        return [pl.BlockSpec((None, M_WIDTH, MCHUNK), lambda bi, i: (blk(bi, i), 0, 0)),
                pl.BlockSpec((MCHUNK, M_WIDTH), lambda bi, i: (blk(bi, i), 1)),
                pl.BlockSpec((None, M_WIDTH, MCHUNK), lambda bi, i: (blk(bi, i), 1, 0)),
                pl.BlockSpec((None, ROW_PACK, MCHUNK), lambda bi, i: (blk(bi, i), 0, 0)),
                pl.BlockSpec((MCHUNK, LANES), lambda bi, i: (blk(bi, i), 0))]

    out = jax.ShapeDtypeStruct((b * s, M_WIDTH), F32)
    return pl.pallas_call(
        _mlstm_kernel,
        grid=(b, nc),
        in_specs=specs(fw) + specs(bw),
        out_specs=[pl.BlockSpec((MCHUNK, M_WIDTH), lambda bi, i: (fw(bi, i), 0)),
                   pl.BlockSpec((MCHUNK, M_WIDTH), lambda bi, i: (bw(bi, i), 0))],
        out_shape=[out, out],
        scratch_shapes=[pltpu.VMEM((2, M_HEADS, V_EXT, M_QK), F32),
                        pltpu.VMEM((GATE_CH, MCHUNK), F32)],
        compiler_params=_params(("parallel", "arbitrary")),
        name="mlstm",
    )(qvt, big, qvt, row, col, qvt, big, qvt, row, col)


def _merge_kernel(x_ref, ya_ref, hf_ref, hb_ref, om_ref, zm_ref, ga_ref, gb_ref, hn_ref,
                  woa_ref, wob_ref, wout_ref, nf_ref, o_ref, hnext_ref=None, *, final_norm):
    hm = hf_ref[...] + hb_ref[...]
    parts = []
    for h in range(M_HEADS):
        hh = hm[:, h * M_V:(h + 1) * M_V]
        ms = jnp.mean(hh * hh, axis=-1, keepdims=True)
        parts.append(hh * lax.rsqrt(ms + EPS))
    hn = jnp.concatenate(parts, axis=1) * hn_ref[...]
    zm = zm_ref[...].astype(F32)
    yb = hn * jax.nn.sigmoid(om_ref[...].astype(F32)) * (zm * jax.nn.sigmoid(zm))
    pa = jnp.dot(ya_ref[...], woa_ref[...], preferred_element_type=F32)
    pb = jnp.dot(yb.astype(BF16), wob_ref[...], preferred_element_type=F32)
    merged = (jax.nn.sigmoid(ga_ref[...].astype(F32)) * pa
              + jax.nn.sigmoid(gb_ref[...].astype(F32)) * pb)
    out = x_ref[...] + jnp.dot(merged.astype(BF16), wout_ref[...], preferred_element_type=F32)
    if final_norm:
        o_ref[...] = _rms(out, nf_ref[...])
    else:
        o_ref[...] = out
        hnext_ref[...] = _rms(out, nf_ref[...]).astype(hnext_ref.dtype)


def _merge(x2, ya, hf, hb, big, hn, woa, wob, wout, nf, final_norm):
    t = x2.shape[0]
    tm = min(MERGE_TM, t)
    row = lambda c: (lambda i: (i, c))
    const = lambda shape: pl.BlockSpec(shape, lambda i: (0, 0), pipeline_mode=pl.Buffered(1))
    o_spec = pl.BlockSpec((tm, D_MODEL), row(0))
    o_shape = jax.ShapeDtypeStruct((t, D_MODEL), F32)
    if not final_norm:
        o_spec = [o_spec, pl.BlockSpec((tm, D_MODEL), row(0))]
        o_shape = [o_shape, jax.ShapeDtypeStruct((t, D_MODEL), BF16)]
    return pl.pallas_call(
        functools.partial(_merge_kernel, final_norm=final_norm),
        grid=(t // tm,),
        in_specs=[pl.BlockSpec((tm, D_MODEL), row(0)),
                  pl.BlockSpec((tm, A_WIDTH), row(0)),
                  pl.BlockSpec((tm, M_WIDTH), row(0)),
                  pl.BlockSpec((tm, M_WIDTH), row(0)),
                  pl.BlockSpec((tm, M_WIDTH), row(2)),
                  pl.BlockSpec((tm, M_WIDTH), row(3)),
                  pl.BlockSpec((tm, D_MODEL), row(2)),
                  pl.BlockSpec((tm, D_MODEL), row(3)),
                  const((1, M_WIDTH)),
                  const((A_WIDTH, D_MODEL)),
                  const((M_WIDTH, D_MODEL)),
                  const((D_MODEL, D_MODEL)),
                  const((1, D_MODEL))],
        out_specs=o_spec,
        out_shape=o_shape,
        compiler_params=_params(("parallel",)),
        name="merge_out",
    )(x2, ya, hf, hb, big, big, big, big, hn, woa, wob, wout, nf)


def _rot_cols(w):
    half = QK_ROPE // 2
    return jnp.concatenate([-w[:, half:], w[:, :half]], axis=1)


def _pack_layer(w_in, w_uq, w_ukv):
    o = 0
    seg = {}
    for name, width in (("c_q", Q_LORA), ("c_kv", KV_LORA), ("k_rope", QK_ROPE), ("z_a", A_WIDTH),
                        ("q_m", M_WIDTH), ("k_m", M_WIDTH), ("v_m", M_WIDTH), ("o_m", M_WIDTH),
                        ("z_m", M_WIDTH), ("gates", 4 * M_HEADS), ("g_a", D_MODEL), ("g_b", D_MODEL)):
        seg[name] = w_in[:, o:o + width]
        o += width
    pad = jnp.zeros((D_MODEL, LANES - 4 * M_HEADS), w_in.dtype)
    w_small = jnp.concatenate([seg["c_q"], seg["c_kv"], seg["k_rope"], _rot_cols(seg["k_rope"]),
                               seg["gates"], pad], axis=1).astype(BF16)
    w_big = jnp.concatenate([seg[n] for n in ("z_a", "k_m", "o_m", "z_m", "g_a", "g_b")],
                            axis=1).astype(BF16)
    w_qvt = jnp.concatenate([seg["q_m"], seg["v_m"]], axis=1).T.astype(BF16)
    wq = w_uq.reshape(Q_LORA, A_HEADS, QK_NOPE + QK_ROPE)
    rope = wq[:, :, QK_NOPE:]
    rot = jnp.concatenate([-rope[:, :, QK_ROPE // 2:], rope[:, :, :QK_ROPE // 2]], axis=2)
    wqt = jnp.concatenate([wq, rot], axis=2).reshape(Q_LORA, A_HEADS * QK_PAD).T.astype(BF16)
    wkv = w_ukv.reshape(KV_LORA, A_HEADS, QK_NOPE + V_HEAD)
    wk = wkv[:, :, :QK_NOPE].reshape(KV_LORA, A_HEADS * QK_NOPE).astype(BF16)
    wvt = wkv[:, :, QK_NOPE:].reshape(KV_LORA, A_HEADS * V_HEAD).T.astype(BF16)
    return w_small, w_big, w_qvt, wqt, wk, wvt


def _rope_table(s):
    inv = ROPE_THETA ** (-jnp.arange(0, QK_ROPE, 2, dtype=F32) / QK_ROPE)
    ang = jnp.arange(s, dtype=F32)[:, None] * inv[None, :]
    cos, sin = jnp.cos(ang), jnp.sin(ang)
    return jnp.concatenate([cos, cos, sin, sin], axis=1)


def _trunk(x, layers, norm_f):
    b, s, d = x.shape
    assert d == D_MODEL and s % MCHUNK == 0 and s % (2 * LANES) == 0, x.shape
    x2 = x.reshape(b * s, D_MODEL)
    cs = _rope_table(s)
    cst = cs.T
    q_scale = float((QK_NOPE + QK_ROPE) ** -0.5 * 1.4426950408889634)
    h2 = _norm(x2, layers[0]["norm_in"])
    for li, ly in enumerate(layers):
        last = li == len(layers) - 1
        small = _project(h2, ly["w_small"], F32, PROJ_TM, SMALL_COLS)
        big = _project(h2, ly["w_big"], BF16, PROJ_TM, PROJ_TN)
        qt, kc, vt = _mla_prep(small, cs, cst, ly["q_a_norm"], ly["kv_a_norm"], ly["wqt"], ly["wk"],
                               ly["wvt"], b, s, q_scale)
        ya = _flash(qt, kc, vt, big, b, s)
        qvt = _project(h2, ly["w_qvt"], BF16, PROJ_TM, PROJ_TN, feature_major=True)
        col, row = _gates(small, ly["b_i"], ly["b_f"])
        hf, hb = _mlstm(qvt, big, row, col, b, s)
        res = _merge(x2, ya, hf, hb, big, ly["m_head_norm"], ly["w_oa"], ly["w_ob"], ly["w_out"],
                     norm_f if last else layers[li + 1]["norm_in"], last)
        x2, h2 = (res, None) if last else res
    return x2.reshape(b, s, D_MODEL)


def kernel(x_prompt, x_sample, norm_in, w_in, b_gates, q_a_norm, w_uq, kv_a_norm, w_ukv, w_oa,
           m_head_norm, w_ob, w_out, norm_f):
    layers = []
    for l in range(w_in.shape[0]):
        w_small, w_big, w_qvt, wqt, wk, wvt = _pack_layer(w_in[l], w_uq[l], w_ukv[l])
        layers.append(dict(
            norm_in=norm_in[l].reshape(1, D_MODEL), w_small=w_small, w_big=w_big, w_qvt=w_qvt,
            wqt=wqt, wk=wk, wvt=wvt,
            q_a_norm=q_a_norm[l].reshape(1, Q_LORA), kv_a_norm=kv_a_norm[l].reshape(1, KV_LORA),
            b_i=b_gates[l, :GATE_CH].reshape(GATE_CH, 1).astype(F32),
            b_f=b_gates[l, GATE_CH:].reshape(GATE_CH, 1).astype(F32),
            m_head_norm=m_head_norm[l].reshape(1, M_WIDTH),
            w_oa=w_oa[l].astype(BF16), w_ob=w_ob[l].astype(BF16), w_out=w_out[l].astype(BF16)))
    nf = norm_f.reshape(1, D_MODEL)
    return (_trunk(x_prompt, layers, nf), _trunk(x_sample, layers, nf))
```

```python
import functools

import jax
import jax.numpy as jnp
from jax import lax
from jax.experimental import pallas as pl
from jax.experimental.pallas import tpu as pltpu

D_MODEL = 2048
A_HEADS = 8
Q_LORA = 512
KV_LORA = 512
QK_NOPE = 128
QK_ROPE = 64
V_HEAD = 128
ROPE_THETA = 10000.0
A_WIDTH = A_HEADS * V_HEAD
M_HEADS = 8
M_QK = 128
M_V = 128
M_WIDTH = M_HEADS * M_V
EPS = 1e-6

LANES = 128
QK_PAD = 256
KV_CHUNK = 512
FLASH_UNROLL = 8
FLASH_HEADS = 2
FLASH_TQ = 512
FLASH_AHEAD = 1
AV_EXT = V_HEAD + 16
PROJ_TM = 1024
PROJ_TN = 2048
NORM_ROWS = 256
MERGE_TM = 256
VMEM_LIMIT = 56 * 1024 * 1024

SMALL_COLS = Q_LORA + KV_LORA + 2 * LANES

F32 = jnp.float32
BF16 = jnp.bfloat16


def _rms(xf, g):
    ms = jnp.mean(xf * xf, axis=-1, keepdims=True)
    return xf * lax.rsqrt(ms + EPS) * g


def _params(sem):
    return pltpu.CompilerParams(dimension_semantics=sem, vmem_limit_bytes=VMEM_LIMIT)


_NT = (((1,), (1,)), ((), ()))


def _norm_kernel(x_ref, g_ref, h_ref, *, rows):
    def body(r, c):
        sl = pl.ds(pl.multiple_of(r * rows, rows), rows)
        h_ref[sl, :] = _rms(x_ref[sl, :], g_ref[...]).astype(h_ref.dtype)
        return c
    lax.fori_loop(0, x_ref.shape[0] // rows, body, 0)


def _norm(x2, gain):
    t, k = x2.shape
    tm = min(PROJ_TM, t)
    return pl.pallas_call(
        functools.partial(_norm_kernel, rows=min(NORM_ROWS, tm)),
        grid=(t // tm,),
        in_specs=[pl.BlockSpec((tm, k), lambda i: (i, 0)), pl.BlockSpec((1, k), lambda i: (0, 0))],
        out_specs=pl.BlockSpec((tm, k), lambda i: (i, 0)),
        out_shape=jax.ShapeDtypeStruct((t, k), BF16),
        compiler_params=_params(("parallel",)),
        name="in_norm",
    )(x2, gain)


def _proj_kernel(h_ref, w_ref, o_ref, *, feature_major):
    if feature_major:
        out = lax.dot_general(w_ref[...], h_ref[...], _NT, preferred_element_type=F32)
        for c in range(o_ref.shape[0]):
            o_ref[c] = out[:, c * o_ref.shape[2]:(c + 1) * o_ref.shape[2]].astype(o_ref.dtype)
    else:
        out = jnp.dot(h_ref[...], w_ref[...], preferred_element_type=F32)
        o_ref[...] = out.astype(o_ref.dtype)


def _project(h2, w, out_dtype, tm, tn, feature_major=False):
    t, k = h2.shape
    n = w.shape[0] if feature_major else w.shape[1]
    tm = min(tm, t)
    if feature_major:
        w_spec = pl.BlockSpec((tn, k), lambda i, j: (j, 0))
        o_spec = pl.BlockSpec((tm // MCHUNK, tn, MCHUNK), lambda i, j: (i, j, 0))
        o_shape = (t // MCHUNK, n, MCHUNK)
    else:
        w_spec = pl.BlockSpec((k, tn), lambda i, j: (0, j))
        o_spec = pl.BlockSpec((tm, tn), lambda i, j: (i, j))
        o_shape = (t, n)
    return pl.pallas_call(
        functools.partial(_proj_kernel, feature_major=feature_major),
        grid=(t // tm, n // tn),
        in_specs=[pl.BlockSpec((tm, k), lambda i, j: (i, 0)), w_spec],
        out_specs=o_spec,
        out_shape=jax.ShapeDtypeStruct(o_shape, out_dtype),
        compiler_params=_params(("parallel", "arbitrary")),
        name="in_proj_t" if feature_major else "in_proj",
    )(h2, w)


def _mla_prep_kernel(cq_ref, ckv_ref, rest_ref, cs_ref, cst_ref, gq_ref, gkv_ref,
                     wqt_ref, wk_ref, wvt_ref, qt_ref, k_ref, vt_ref, *, q_scale):
    hq = _rms(cq_ref[...], gq_ref[...]).astype(BF16)
    hkv = _rms(ckv_ref[...], gkv_ref[...]).astype(BF16)
    qt = lax.dot_general(wqt_ref[...], hq, _NT, preferred_element_type=F32)
    vt = lax.dot_general(wvt_ref[...], hkv, _NT, preferred_element_type=F32)
    kn = jnp.dot(hkv, wk_ref[...], preferred_element_type=F32)
    t = rest_ref[...] * cs_ref[...]
    lane = lax.broadcasted_iota(jnp.int32, t.shape, 1)
    k_r = jnp.where(lane < QK_ROPE, t + pltpu.roll(t, QK_ROPE, 1), 0.0)
    cst = cst_ref[...]
    pad = jnp.zeros((QK_PAD - QK_NOPE - QK_ROPE, qt.shape[1]), F32)
    ones_blk = (lax.broadcasted_iota(jnp.int32, (AV_EXT - V_HEAD, qt.shape[1]), 0) == 0).astype(BF16)
    for h in range(A_HEADS):
        qh = qt[h * QK_PAD:(h + 1) * QK_PAD]
        tq = qh[QK_NOPE:] * cst
        qt_ref[h] = jnp.concatenate(
            [qh[:QK_NOPE] * q_scale, (tq[:QK_ROPE] + tq[QK_ROPE:]) * q_scale, pad], axis=0).astype(BF16)
        k_ref[h] = jnp.concatenate([kn[:, h * QK_NOPE:(h + 1) * QK_NOPE], k_r], axis=1).astype(BF16)
        vt_ref[h] = jnp.concatenate([vt[h * V_HEAD:(h + 1) * V_HEAD].astype(BF16), ones_blk], axis=0)


def _mla_prep(small, cs, cst, gq, gkv, wqt, wk, wvt, b, s, q_scale):
    tm = min(KV_CHUNK, s // 2)
    nb = s // tm
    full = lambda bi, i: (0, 0)
    return pl.pallas_call(
        functools.partial(_mla_prep_kernel, q_scale=q_scale),
        grid=(b, nb),
        in_specs=[pl.BlockSpec((tm, Q_LORA), lambda bi, i: (bi * nb + i, 0)),
                  pl.BlockSpec((tm, KV_LORA), lambda bi, i: (bi * nb + i, 1)),
                  pl.BlockSpec((tm, LANES), lambda bi, i: (bi * nb + i, (Q_LORA + KV_LORA) // LANES)),
                  pl.BlockSpec((tm, LANES), lambda bi, i: (i, 0)),
                  pl.BlockSpec((LANES, tm), lambda bi, i: (0, i)),
                  pl.BlockSpec((1, Q_LORA), full),
                  pl.BlockSpec((1, KV_LORA), full),
                  pl.BlockSpec((A_HEADS * QK_PAD, Q_LORA), full),
                  pl.BlockSpec((KV_LORA, A_HEADS * QK_NOPE), full),
                  pl.BlockSpec((A_HEADS * V_HEAD, KV_LORA), full)],
        out_specs=[pl.BlockSpec((None, A_HEADS, QK_PAD, tm), lambda bi, i: (bi, 0, 0, i)),
                   pl.BlockSpec((None, A_HEADS, tm, QK_PAD), lambda bi, i: (bi, 0, i, 0)),
                   pl.BlockSpec((None, A_HEADS, None, AV_EXT, tm), lambda bi, i: (bi, 0, i, 0, 0))],
        out_shape=[jax.ShapeDtypeStruct((b, A_HEADS, QK_PAD, s), BF16),
                   jax.ShapeDtypeStruct((b, A_HEADS, s, QK_PAD), BF16),
                   jax.ShapeDtypeStruct((b, A_HEADS, nb, AV_EXT, tm), BF16)],
        compiler_params=_params(("parallel", "parallel")),
        name="mla_prep",
    )(small, small, small, cs, cst, gq, gkv, wqt, wk, wvt)


def _flash_plan(nkb):
    unroll = next(u for u in (FLASH_UNROLL, 8, 2) if nkb % u == 0)
    ahead = FLASH_AHEAD if (nkb % (2 * FLASH_AHEAD) == 0 and unroll % (2 * FLASH_AHEAD) == 0) else 1
    return unroll, ahead


def _flash_kernel(qt_ref, qtn_ref, k_ref, vt_ref, z_ref, o_ref, q2_scr, s_scr, cmax_scr, acc_scr):
    nh, tq = qt_ref.shape[0], qt_ref.shape[2]
    nkb = vt_ref.shape[1]
    tk = k_ref.shape[1] // nkb
    unroll, ahead = _flash_plan(nkb)
    slots = s_scr.shape[1]
    acc_scr[...] = jnp.zeros(acc_scr.shape, F32)
    q2_scr[0] = qt_ref[...]
    q2_scr[1] = qtn_ref[...]

    def scores(hh, kb, qsel, slot):
        start = kb * tk if isinstance(kb, int) else pl.multiple_of(kb * tk, tk)
        st = jnp.dot(k_ref[hh, pl.ds(start, tk), :], q2_scr[qsel, hh], preferred_element_type=F32)
        s_scr[hh, slot] = st
        return jnp.max(st, axis=0, keepdims=True)

    @pl.when(pl.program_id(2) == 0)
    def _():
        for hh in range(nh):
            for a in range(ahead):
                cmax_scr[hh, a] = scores(hh, a, 0, a)

    def softmax_values(hh, kb, slot, cmax, m_prev):
        m_new = jnp.maximum(m_prev, cmax)
        pt = jnp.exp2((s_scr[hh, slot] - m_new).astype(BF16))
        acc_scr[hh] = (jnp.exp2(m_prev - m_new) * acc_scr[hh]
                       + jnp.dot(vt_ref[hh, kb], pt, preferred_element_type=F32))
        return m_new

    def trip(j, carry):
        m = list(carry[:nh])
        cmax = [list(carry[nh + hh * ahead:nh + (hh + 1) * ahead]) for hh in range(nh)]
        for u in range(unroll):
            kb = unroll * j + u
            nxt, qsel = kb + ahead, 0
            if u >= unroll - ahead:
                wrap = nxt >= nkb
                nxt, qsel = jnp.where(wrap, nxt - nkb, nxt), jnp.where(wrap, 1, 0)
            for hh in range(nh):
                c_new = scores(hh, nxt, qsel, (u + ahead) % slots)
                m[hh] = softmax_values(hh, kb, u % slots, cmax[hh][0], m[hh])
                cmax[hh] = cmax[hh][1:] + [c_new]
        return (*m, *[c for per_head in cmax for c in per_head])

    init = (*[jnp.full((1, tq), -jnp.inf, F32)] * nh,
            *[cmax_scr[hh, a] for hh in range(nh) for a in range(ahead)])
    out = lax.fori_loop(0, nkb // unroll, trip, init)
    for hh in range(nh):
        for a in range(ahead):
            cmax_scr[hh, a] = out[nh + hh * ahead + a]
        hs = slice(hh * V_HEAD, (hh + 1) * V_HEAD)
        z = z_ref[:, hs].astype(F32)
        acc = acc_scr[hh]
        o = (acc[:V_HEAD] / acc[V_HEAD:V_HEAD + 1]).T
        o_ref[:, hs] = (o * (z * jax.nn.sigmoid(z))).astype(o_ref.dtype)


def _flash(qt, kc, vt, big, b, s):
    tq = min(FLASH_TQ, s)
    nq = s // tq
    nkb, tk = vt.shape[2], vt.shape[4]
    _, ahead = _flash_plan(nkb)
    nh = FLASH_HEADS
    kv_bytes = nh * s * (QK_PAD + AV_EXT) * jnp.dtype(BF16).itemsize
    kv_mode = dict(pipeline_mode=pl.Buffered(1)) if 2 * kv_bytes > VMEM_LIMIT // 2 else {}
    return pl.pallas_call(
        _flash_kernel,
        grid=(b, A_HEADS // nh, nq),
        in_specs=[pl.BlockSpec((None, nh, QK_PAD, tq), lambda bi, h, i: (bi, h, 0, i)),
                  pl.BlockSpec((None, nh, QK_PAD, tq),
                               lambda bi, h, i: (bi, h, 0, jnp.minimum(i + 1, nq - 1))),
                  pl.BlockSpec((None, nh, s, QK_PAD), lambda bi, h, i: (bi, h, 0, 0), **kv_mode),
                  pl.BlockSpec((None, nh, nkb, AV_EXT, tk), lambda bi, h, i: (bi, h, 0, 0, 0),
                               **kv_mode),
                  pl.BlockSpec((tq, nh * V_HEAD), lambda bi, h, i: (bi * nq + i, h))],
        out_specs=pl.BlockSpec((tq, nh * V_HEAD), lambda bi, h, i: (bi * nq + i, h)),
        out_shape=jax.ShapeDtypeStruct((b * s, A_WIDTH), BF16),
        scratch_shapes=[pltpu.VMEM((2, nh, QK_PAD, tq), BF16),
                        pltpu.VMEM((nh, 2 * ahead, tk, tq), F32),
                        pltpu.VMEM((nh, ahead, 1, tq), F32),
                        pltpu.VMEM((nh, AV_EXT, tq), F32)],
        compiler_params=_params(("parallel", "parallel", "arbitrary")),
        name="mla_flash",
    )(qt, qt, kc, vt, big)


GATE_CH = 2 * M_HEADS
MCHUNK = 2 * LANES
GATE_CHUNKS = 16
ROW_B, ROW_R, ROW_CM, ROW_G, ROW_RMAX = (i * GATE_CH for i in range(5))
ROW_PACK = 5 * GATE_CH


def _gate_kernel(x_ref, bi_ref, bf_ref, col_ref, row_ref):
    nc = row_ref.shape[0]
    shape = (nc * GATE_CH, LANES)
    pos = lax.broadcasted_iota(jnp.int32, shape, 1)
    is_fw = lax.broadcasted_iota(jnp.int32, shape, 0) % GATE_CH < M_HEADS
    shifts = (1, 2, 4, 8, 16, 32, 64)

    def scan(y, op, ident):
        for sft in shifts:
            y_f = jnp.where(pos >= sft, pltpu.roll(y, sft, 1), ident)
            y_b = jnp.where(pos < LANES - sft, pltpu.roll(y, LANES - sft, 1), ident)
            y = op(y, jnp.where(is_fw, y_f, y_b))
        return y

    def total(y, op):
        for sft in shifts:
            y = op(y, pltpu.roll(y, sft, 1))
        return y

    zpad = jnp.zeros((LANES - GATE_CH, LANES), F32)
    log_i, log_f = ([], []), ([], [])
    for c in range(nc):
        for half in range(2):
            rs = slice(c * MCHUNK + half * LANES, c * MCHUNK + (half + 1) * LANES)
            xt = x_ref[rs, :].T
            log_i[half].append(xt[0:GATE_CH] + bi_ref[...])
            log_f[half].append(jax.nn.log_sigmoid(xt[GATE_CH:2 * GATE_CH] + bf_ref[...]))
    li0, li1 = (jnp.concatenate(v, axis=0) for v in log_i)
    lf0, lf1 = (jnp.concatenate(v, axis=0) for v in log_f)
    t0, t1 = total(lf0, jnp.add), total(lf1, jnp.add)
    b0 = scan(lf0, jnp.add, 0.0) + jnp.where(is_fw, 0.0, t1)
    b1 = scan(lf1, jnp.add, 0.0) + jnp.where(is_fw, t0, 0.0)
    r0, r1 = li0 - b0, li1 - b1
    a0, a1 = total(r0, jnp.maximum), total(r1, jnp.maximum)
    e0, e1 = scan(r0, jnp.maximum, -jnp.inf), scan(r1, jnp.maximum, -jnp.inf)
    cm0 = jnp.where(is_fw, e0, jnp.maximum(e0, a1))
    cm1 = jnp.where(is_fw, jnp.maximum(e1, a0), e1)
    g, rmax = t0 + t1, jnp.maximum(a0, a1)
    for c in range(nc):
        cs = slice(c * GATE_CH, (c + 1) * GATE_CH)
        for off, v0, v1 in ((ROW_B, b0, b1), (ROW_R, r0, r1), (ROW_CM, cm0, cm1),
                            (ROW_G, g, g), (ROW_RMAX, rmax, rmax)):
            row_ref[c, off:off + GATE_CH, 0:LANES] = v0[cs]
            row_ref[c, off:off + GATE_CH, LANES:MCHUNK] = v1[cs]
        for half, r in ((0, r0), (1, r1)):
            rs = slice(c * MCHUNK + half * LANES, c * MCHUNK + (half + 1) * LANES)
            col_ref[rs, :] = jnp.concatenate([r[cs], zpad], axis=0).T


def _gates(small, b_i, b_f):
    t = small.shape[0]
    nc = t // MCHUNK
    cpb = min(GATE_CHUNKS, nc)
    return pl.pallas_call(
        _gate_kernel,
        grid=(nc // cpb,),
        in_specs=[pl.BlockSpec((cpb * MCHUNK, LANES), lambda i: (i, SMALL_COLS // LANES - 1)),
                  pl.BlockSpec((GATE_CH, 1), lambda i: (0, 0)),
                  pl.BlockSpec((GATE_CH, 1), lambda i: (0, 0))],
        out_specs=[pl.BlockSpec((cpb * MCHUNK, LANES), lambda i: (i, 0)),
                   pl.BlockSpec((cpb, ROW_PACK, MCHUNK), lambda i: (i, 0, 0))],
        out_shape=[jax.ShapeDtypeStruct((t, LANES), F32),
                   jax.ShapeDtypeStruct((nc, ROW_PACK, MCHUNK), F32)],
        compiler_params=_params(("parallel",)),
        name="mlstm_gates",
    )(small, b_i, b_f)


V_EXT = M_V + 16


def _mlstm_kernel(qtf_ref, kf_ref, vtf_ref, rowf_ref, colf_ref,
                  qtb_ref, kb_ref, vtb_ref, rowb_ref, colb_ref,
                  hf_ref, hb_ref, c_scr, m_scr):
    @pl.when(pl.program_id(1) == 0)
    def _():
        c_scr[...] = jnp.zeros(c_scr.shape, F32)
        m_scr[...] = jnp.zeros(m_scr.shape, F32)

    ll = lax.broadcasted_iota(jnp.int32, (MCHUNK, MCHUNK), 0)
    jj = lax.broadcasted_iota(jnp.int32, (MCHUNK, MCHUNK), 1)
    ones_blk = (lax.broadcasted_iota(jnp.int32, (V_EXT - M_V, MCHUNK), 0) == 0).astype(F32)
    inv_scale = float(M_QK) ** 0.5
    m_all = m_scr[...]
    m_next = []
    dirs = ((qtf_ref, kf_ref, vtf_ref, rowf_ref, colf_ref[...], hf_ref, ll <= jj),
            (qtb_ref, kb_ref, vtb_ref, rowb_ref, colb_ref[...], hb_ref, ll >= jj))
    units = [(d, h) for d in range(2) for h in range(M_HEADS)]

    def row(d, off, h):
        r = off + d * M_HEADS + h
        return dirs[d][3][r:r + 1, :]

    st, cq = {}, {}
    for d, h in units:
        qt_ref, k_ref = dirs[d][0], dirs[d][1]
        hs = slice(h * M_QK, (h + 1) * M_QK)
        qt = qt_ref[hs, :]
        st[d, h] = jnp.dot(k_ref[:, hs], qt, preferred_element_type=F32)
        cq[d, h] = jnp.dot(c_scr[d, h].astype(BF16), qt, preferred_element_type=F32)

    for d, h in units:
        _, _, vt_ref, _, col, h_ref, mask = dirs[d]
        ch = d * M_HEADS + h
        hs = slice(h * M_QK, (h + 1) * M_QK)
        m_prev = m_all[ch:ch + 1, :]
        mm = jnp.maximum(m_prev, row(d, ROW_CM, h))
        r_col = col[:, ch:ch + 1]
        pt = (jnp.exp(jnp.where(mask, r_col - mm, -jnp.inf)) * st[d, h]).astype(BF16)
        vt_ext = jnp.concatenate([vt_ref[hs, :], ones_blk.astype(BF16)], axis=0)
        tot = (jnp.dot(vt_ext, pt, preferred_element_type=F32)
               + jnp.exp(m_prev - mm) * cq[d, h])
        floor = jnp.exp(-row(d, ROW_B, h) - mm) * inv_scale
        ht = tot[:M_V] / jnp.maximum(jnp.abs(tot[M_V:M_V + 1]), floor)
        h_ref[:, hs] = ht.T

    for d, h in units:
        k_ref, vt_ref = dirs[d][1], dirs[d][2]
        ch = d * M_HEADS + h
        hs = slice(h * M_QK, (h + 1) * M_QK)
        m_prev = m_all[ch:ch + 1, :]
        mx = jnp.maximum(m_prev, row(d, ROW_RMAX, h))
        w_row = jnp.exp(row(d, ROW_R, h) - mx)
        lhs = jnp.concatenate([(vt_ref[hs, :].astype(F32) * w_row).astype(BF16),
                               (ones_blk * w_row).astype(BF16)], axis=0)
        c_scr[d, h] = (jnp.exp(m_prev - mx)[:, :M_QK] * c_scr[d, h]
                       + jnp.dot(lhs, k_ref[:, hs], preferred_element_type=F32))
        m_next.append(row(d, ROW_G, h) + mx)
    m_scr[...] = jnp.concatenate(m_next, axis=0)


def _mlstm(qvt, big, row, col, b, s):
    nc = s // MCHUNK
    fw = lambda bi, i: bi * nc + i
    bw = lambda bi, i: bi * nc + (nc - 1 - i)

    def specs(blk):
        return [pl.BlockSpec((None, M_WIDTH, MCHUNK), lambda bi, i: (blk(bi, i), 0, 0)),
                pl.BlockSpec((MCHUNK, M_WIDTH), lambda bi, i: (blk(bi, i), 1)),
                pl.BlockSpec((None, M_WIDTH, MCHUNK), lambda bi, i: (blk(bi, i), 1, 0)),
                pl.BlockSpec((None, ROW_PACK, MCHUNK), lambda bi, i: (blk(bi, i), 0, 0)),
                pl.BlockSpec((MCHUNK, LANES), lambda bi, i: (blk(bi, i), 0))]

    out = jax.ShapeDtypeStruct((b * s, M_WIDTH), F32)
    return pl.pallas_call(
        _mlstm_kernel,
        grid=(b, nc),
        in_specs=specs(fw) + specs(bw),
        out_specs=[pl.BlockSpec((MCHUNK, M_WIDTH), lambda bi, i: (fw(bi, i), 0)),
                   pl.BlockSpec((MCHUNK, M_WIDTH), lambda bi, i: (bw(bi, i), 0))],
        out_shape=[out, out],
        scratch_shapes=[pltpu.VMEM((2, M_HEADS, V_EXT, M_QK), F32),
                        pltpu.VMEM((GATE_CH, MCHUNK), F32)],
        compiler_params=_params(("parallel", "arbitrary")),
        name="mlstm",
    )(qvt, big, qvt, row, col, qvt, big, qvt, row, col)


def _merge_kernel(x_ref, ya_ref, hf_ref, hb_ref, om_ref, zm_ref, ga_ref, gb_ref, hn_ref,
                  woa_ref, wob_ref, wout_ref, nf_ref, o_ref, hnext_ref=None, *, final_norm):
    hm = hf_ref[...] + hb_ref[...]
    parts = []
    for h in range(M_HEADS):
        hh = hm[:, h * M_V:(h + 1) * M_V]
        ms = jnp.mean(hh * hh, axis=-1, keepdims=True)
        parts.append(hh * lax.rsqrt(ms + EPS))
    hn = jnp.concatenate(parts, axis=1) * hn_ref[...]
    zm = zm_ref[...].astype(F32)
    yb = hn * jax.nn.sigmoid(om_ref[...].astype(F32)) * (zm * jax.nn.sigmoid(zm))
    pa = jnp.dot(ya_ref[...], woa_ref[...], preferred_element_type=F32)
    pb = jnp.dot(yb.astype(BF16), wob_ref[...], preferred_element_type=F32)
    merged = (jax.nn.sigmoid(ga_ref[...].astype(F32)) * pa
              + jax.nn.sigmoid(gb_ref[...].astype(F32)) * pb)
    out = x_ref[...] + jnp.dot(merged.astype(BF16), wout_ref[...], preferred_element_type=F32)
    if final_norm:
        o_ref[...] = _rms(out, nf_ref[...])
    else:
        o_ref[...] = out
        hnext_ref[...] = _rms(out, nf_ref[...]).astype(hnext_ref.dtype)


def _merge(x2, ya, hf, hb, big, hn, woa, wob, wout, nf, final_norm):
    t = x2.shape[0]
    tm = min(MERGE_TM, t)
    row = lambda c: (lambda i: (i, c))
    const = lambda shape: pl.BlockSpec(shape, lambda i: (0, 0), pipeline_mode=pl.Buffered(1))
    o_spec = pl.BlockSpec((tm, D_MODEL), row(0))
    o_shape = jax.ShapeDtypeStruct((t, D_MODEL), F32)
    if not final_norm:
        o_spec = [o_spec, pl.BlockSpec((tm, D_MODEL), row(0))]
        o_shape = [o_shape, jax.ShapeDtypeStruct((t, D_MODEL), BF16)]
    return pl.pallas_call(
        functools.partial(_merge_kernel, final_norm=final_norm),
        grid=(t // tm,),
        in_specs=[pl.BlockSpec((tm, D_MODEL), row(0)),
                  pl.BlockSpec((tm, A_WIDTH), row(0)),
                  pl.BlockSpec((tm, M_WIDTH), row(0)),
                  pl.BlockSpec((tm, M_WIDTH), row(0)),
                  pl.BlockSpec((tm, M_WIDTH), row(2)),
                  pl.BlockSpec((tm, M_WIDTH), row(3)),
                  pl.BlockSpec((tm, D_MODEL), row(2)),
                  pl.BlockSpec((tm, D_MODEL), row(3)),
                  const((1, M_WIDTH)),
                  const((A_WIDTH, D_MODEL)),
                  const((M_WIDTH, D_MODEL)),
                  const((D_MODEL, D_MODEL)),
                  const((1, D_MODEL))],
        out_specs=o_spec,
        out_shape=o_shape,
        compiler_params=_params(("parallel",)),
        name="merge_out",
    )(x2, ya, hf, hb, big, big, big, big, hn, woa, wob, wout, nf)


def _rot_cols(w):
    half = QK_ROPE // 2
    return jnp.concatenate([-w[:, half:], w[:, :half]], axis=1)


def _pack_layer(w_in, w_uq, w_ukv):
    o = 0
    seg = {}
    for name, width in (("c_q", Q_LORA), ("c_kv", KV_LORA), ("k_rope", QK_ROPE), ("z_a", A_WIDTH),
                        ("q_m", M_WIDTH), ("k_m", M_WIDTH), ("v_m", M_WIDTH), ("o_m", M_WIDTH),
                        ("z_m", M_WIDTH), ("gates", 4 * M_HEADS), ("g_a", D_MODEL), ("g_b", D_MODEL)):
        seg[name] = w_in[:, o:o + width]
        o += width
    pad = jnp.zeros((D_MODEL, LANES - 4 * M_HEADS), w_in.dtype)
    w_small = jnp.concatenate([seg["c_q"], seg["c_kv"], seg["k_rope"], _rot_cols(seg["k_rope"]),
                               seg["gates"], pad], axis=1).astype(BF16)
    w_big = jnp.concatenate([seg[n] for n in ("z_a", "k_m", "o_m", "z_m", "g_a", "g_b")],
                            axis=1).astype(BF16)
    w_qvt = jnp.concatenate([seg["q_m"], seg["v_m"]], axis=1).T.astype(BF16)
    wq = w_uq.reshape(Q_LORA, A_HEADS, QK_NOPE + QK_ROPE)
    rope = wq[:, :, QK_NOPE:]
    rot = jnp.concatenate([-rope[:, :, QK_ROPE // 2:], rope[:, :, :QK_ROPE // 2]], axis=2)
    wqt = jnp.concatenate([wq, rot], axis=2).reshape(Q_LORA, A_HEADS * QK_PAD).T.astype(BF16)
    wkv = w_ukv.reshape(KV_LORA, A_HEADS, QK_NOPE + V_HEAD)
    wk = wkv[:, :, :QK_NOPE].reshape(KV_LORA, A_HEADS * QK_NOPE).astype(BF16)
    wvt = wkv[:, :, QK_NOPE:].reshape(KV_LORA, A_HEADS * V_HEAD).T.astype(BF16)
    return w_small, w_big, w_qvt, wqt, wk, wvt


def _rope_table(s):
    inv = ROPE_THETA ** (-jnp.arange(0, QK_ROPE, 2, dtype=F32) / QK_ROPE)
    ang = jnp.arange(s, dtype=F32)[:, None] * inv[None, :]
    cos, sin = jnp.cos(ang), jnp.sin(ang)
    return jnp.concatenate([cos, cos, sin, sin], axis=1)


def _trunk(x, layers, norm_f):
    b, s, d = x.shape
    assert d == D_MODEL and s % MCHUNK == 0 and s % (2 * LANES) == 0, x.shape
    x2 = x.reshape(b * s, D_MODEL)
    cs = _rope_table(s)
    cst = cs.T
    q_scale = float((QK_NOPE + QK_ROPE) ** -0.5 * 1.4426950408889634)
    h2 = _norm(x2, layers[0]["norm_in"])
    for li, ly in enumerate(layers):
        last = li == len(layers) - 1
        small = _project(h2, ly["w_small"], F32, PROJ_TM, SMALL_COLS)
        big = _project(h2, ly["w_big"], BF16, PROJ_TM, PROJ_TN)
        qt, kc, vt = _mla_prep(small, cs, cst, ly["q_a_norm"], ly["kv_a_norm"], ly["wqt"], ly["wk"],
                               ly["wvt"], b, s, q_scale)
        ya = _flash(qt, kc, vt, big, b, s)
        qvt = _project(h2, ly["w_qvt"], BF16, PROJ_TM, PROJ_TN, feature_major=True)
        col, row = _gates(small, ly["b_i"], ly["b_f"])
        hf, hb = _mlstm(qvt, big, row, col, b, s)
        res = _merge(x2, ya, hf, hb, big, ly["m_head_norm"], ly["w_oa"], ly["w_ob"], ly["w_out"],
                     norm_f if last else layers[li + 1]["norm_in"], last)
        x2, h2 = (res, None) if last else res
    return x2.reshape(b, s, D_MODEL)


def kernel(x_prompt, x_sample, norm_in, w_in, b_gates, q_a_norm, w_uq, kv_a_norm, w_ukv, w_oa,
           m_head_norm, w_ob, w_out, norm_f):
    layers = []
    for l in range(w_in.shape[0]):
        w_small, w_big, w_qvt, wqt, wk, wvt = _pack_layer(w_in[l], w_uq[l], w_ukv[l])
        layers.append(dict(
            norm_in=norm_in[l].reshape(1, D_MODEL), w_small=w_small, w_big=w_big, w_qvt=w_qvt,
            wqt=wqt, wk=wk, wvt=wvt,
            q_a_norm=q_a_norm[l].reshape(1, Q_LORA), kv_a_norm=kv_a_norm[l].reshape(1, KV_LORA),
            b_i=b_gates[l, :GATE_CH].reshape(GATE_CH, 1).astype(F32),
            b_f=b_gates[l, GATE_CH:].reshape(GATE_CH, 1).astype(F32),
            m_head_norm=m_head_norm[l].reshape(1, M_WIDTH),
            w_oa=w_oa[l].astype(BF16), w_ob=w_ob[l].astype(BF16), w_out=w_out[l].astype(BF16)))
    nf = norm_f.reshape(1, D_MODEL)
    return (_trunk(x_prompt, layers, nf), _trunk(x_sample, layers, nf))
```
